```python
import math
import jax, jax.numpy as jnp
from jax import lax
import numpy as np

D_MODEL = 1024
BATCH = 32
SEQ = 2048
DEPTH = 1
DEC_BATCH = 8
DEC_SEQ = 16
PAST_LEN = 2048

CHUNK = 64
Q_BLOCK = 128
N_MEM = 256
H_A = 4
DK_A = 64
DV_A = 2 * DK_A
W_A = H_A * DV_A
H_B = 4
DH_B = 64
W_B = H_B * DH_B
H_M = 4
DH_M = 64
W_M = H_M * DH_M
D_MIX = W_A + W_B + W_M
EPS = 1e-6
ALIBI_SLOPES = tuple(2.0 ** (-8.0 * (h + 1) / H_A) for h in range(H_A))
IN_SIZES = (W_A, W_A, W_A, W_A, W_B, W_B, W_B, W_B, H_B, H_B, W_B, W_M, W_M)
N_IN = 4 * W_A + 5 * W_B + 2 * H_B + 2 * W_M

kernel_name = "hymba_diffattn_mlstm_stream_step"

F32 = jnp.float32


def rms_norm(x, g):
    xf = x.astype(F32)
    y = xf * lax.rsqrt(jnp.mean(xf * xf, axis=-1, keepdims=True) + EPS)
    return (y * g.astype(F32)).astype(x.dtype)


def split_cols(z):
    out, start = [], 0
    for size in IN_SIZES:
        out.append(z[..., start:start + size])
        start += size
    return out


def diff_attention(q, k, v, q_pos, k_pos, lam):
    slopes = jnp.asarray(ALIBI_SLOPES, F32)
    scale = DK_A ** -0.5
    k_chunk = k_pos // CHUNK

    def block(args):
        qb, qp = args
        s = jnp.einsum('bqhmd,bkhmd->bhmqk', qb, k, preferred_element_type=F32) * scale
        dist = jnp.abs(qp[:, None] - k_pos[None, :]).astype(F32)
        bias = -slopes[:, None, None, None] * dist
        allowed = k_chunk[None, :] <= (qp // CHUNK)[:, None]
        s = jnp.where(allowed, s + bias, -jnp.inf)
        p = jax.nn.softmax(s, axis=-1)
        w = p[:, :, 0] - lam * p[:, :, 1]
        return jnp.einsum('bhqk,bkhd->bqhd', w.astype(v.dtype), v)

    B, Sq = q.shape[0], q.shape[1]
    if Sq <= Q_BLOCK:
        return block((q, q_pos))
    nb = Sq // Q_BLOCK
    qs = q.reshape((B, nb, Q_BLOCK) + q.shape[2:]).swapaxes(0, 1)
    ps = q_pos.reshape(nb, Q_BLOCK)
    out = lax.map(block, (qs, ps))
    return out.swapaxes(0, 1).reshape(B, Sq, H_A, DV_A)


def mlstm_chunk(carry, inp):
    C0, n0, m0 = carry
    q, k, v, ig, lf = inp
    L = q.shape[2]
    b = jnp.cumsum(lf, axis=-1)
    causal = jnp.tril(jnp.ones((L, L), dtype=bool))
    D = jnp.where(causal, b[..., :, None] - b[..., None, :] + ig[..., None, :], -jnp.inf)
    g = b + m0[..., None]
    m = jnp.maximum(g, jnp.max(D, axis=-1))
    Dw = jnp.exp(D - m[..., None])
    gw = jnp.exp(g - m)
    A = jnp.einsum('bhtd,bhsd->bhts', q, k) * Dw
    num = gw[..., None] * jnp.einsum('bhtd,bhde->bhte', q, C0) + jnp.einsum('bhts,bhse->bhte', A, v)
    den = gw * jnp.einsum('bhtd,bhd->bht', q, n0) + jnp.sum(A, axis=-1)
    h = num / jnp.maximum(jnp.abs(den), jnp.exp(-m))[..., None]
    mL = m[..., -1]
    wS = jnp.exp(b[..., -1:] - b + ig - mL[..., None])
    decay = jnp.exp(b[..., -1] + m0 - mL)
    C = decay[..., None, None] * C0 + jnp.einsum('bhs,bhsd,bhse->bhde', wS, k, v)
    n = decay[..., None] * n0 + jnp.einsum('bhs,bhsd->bhd', wS, k)
    return (C, n, mL), h


def mlstm_run(q, k, v, ig, lf, state):
    B, S = q.shape[0], q.shape[1]
    q, k, v, ig, lf = [jnp.moveaxis(a.astype(F32), 1, 2) for a in (q, k, v, ig, lf)]
    state = tuple(s.astype(F32) for s in state)
    if S <= CHUNK:
        state, h = mlstm_chunk(state, (q, k, v, ig, lf))
    else:
        nc = S // CHUNK

        def chunks(a):
            a = a.reshape(a.shape[:2] + (nc, CHUNK) + a.shape[3:])
            return jnp.moveaxis(a, 2, 0)

        state, h = lax.scan(mlstm_chunk, state, tuple(chunks(a) for a in (q, k, v, ig, lf)))
        h = jnp.moveaxis(h, 0, 2).reshape(B, H_B, S, DH_B)
    return jnp.moveaxis(h, 2, 1), state


def memory_kv(mem, g_mem, w_mk, w_mv, g_km):
    B, N, _ = mem.shape
    hm = rms_norm(mem, g_mem)
    mk = rms_norm(jnp.einsum('bnd,de->bne', hm, w_mk).reshape(B, N, H_M, DH_M), g_km)
    mv = jnp.einsum('bnd,de->bne', hm, w_mv).reshape(B, N, H_M, DH_M)
    return mk, mv


def mixer_layer(x, q_pos, past_k, past_v, mlstm_state, mem_k, mem_v, lam_init,
                g_norm, w_in, w_out, g_qa, g_ka, lam_q1, lam_k1, lam_q2, lam_k2, g_subln,
                b_i, b_f, g_mh, g_qm):
    B, S, _ = x.shape
    h = rms_norm(x, g_norm)
    z = jnp.einsum('bsd,de->bse', h, w_in)
    qa, ka, va, ga, qb, kb, vb, ob, ib, fb, gb, qm, gm = split_cols(z)

    qa = rms_norm(qa.reshape(B, S, H_A, 2, DK_A), g_qa)
    ka = rms_norm(ka.reshape(B, S, H_A, 2, DK_A), g_ka)
    va = va.reshape(B, S, H_A, DV_A)
    new_k = ka.reshape(B, S, H_A, 2 * DK_A)
    if past_k is None:
        keys, vals, k_pos = ka, va, q_pos
    else:
        P = past_k.shape[1]
        keys = jnp.concatenate([past_k.reshape(B, P, H_A, 2, DK_A).astype(ka.dtype), ka], axis=1)
        vals = jnp.concatenate([past_v.astype(va.dtype), va], axis=1)
        k_pos = jnp.arange(P + S, dtype=jnp.int32)
    lam = (jnp.exp(jnp.sum(lam_q1.astype(F32) * lam_k1.astype(F32)))
           - jnp.exp(jnp.sum(lam_q2.astype(F32) * lam_k2.astype(F32))) + lam_init)
    oa = diff_attention(qa, keys, vals, q_pos, k_pos, lam)
    oa = rms_norm(oa, g_subln) * (1.0 - lam_init)
    oa = oa.reshape(B, S, W_A) * jax.nn.silu(ga)

    qb = qb.reshape(B, S, H_B, DH_B)
    kb = kb.reshape(B, S, H_B, DH_B) * (DH_B ** -0.5)
    vb = vb.reshape(B, S, H_B, DH_B)
    ig = (ib + b_i).astype(F32)
    lf = jax.nn.log_sigmoid((fb + b_f).astype(F32))
    hb, new_state = mlstm_run(qb, kb, vb, ig, lf, mlstm_state)
    hb = rms_norm(hb.astype(x.dtype), g_mh) * jax.nn.sigmoid(ob.reshape(B, S, H_B, DH_B))
    hb = hb.reshape(B, S, W_B) * jax.nn.silu(gb)

    qm = rms_norm(qm.reshape(B, S, H_M, DH_M), g_qm)
    sm = jnp.einsum('bshd,bnhd->bhsn', qm, mem_k.astype(qm.dtype), preferred_element_type=F32) * (DH_M ** -0.5)
    pm = jax.nn.softmax(sm, axis=-1)
    om = jnp.einsum('bhsn,bnhd->bshd', pm.astype(x.dtype), mem_v.astype(x.dtype)).reshape(B, S, W_M)
    om = om * jax.nn.silu(gm)

    mix = jnp.concatenate([oa, hb, om], axis=-1)
    y = x + jnp.einsum('bse,ed->bsd', mix, w_out)
    return y, new_k, va, new_state


def setup_inputs(seed: int = 0) -> dict:
    key = jax.random.key(seed)
    ks = iter(jax.random.split(key, 40))

    def nrm(shape, s=1.0):
        return s * jax.random.normal(next(ks), shape, F32)

    def gain(shape):
        return 1.0 + 0.05 * jax.random.normal(next(ks), shape, F32)

    return {
        "x_prompt": nrm((BATCH, SEQ, D_MODEL)),
        "x_sample": nrm((DEC_BATCH, DEC_SEQ, D_MODEL)),
        "cache_attn_k": nrm((DEPTH, DEC_BATCH, PAST_LEN, H_A, 2 * DK_A)),
        "cache_attn_v": nrm((DEPTH, DEC_BATCH, PAST_LEN, H_A, DV_A)),
        "state_mlstm_C": nrm((DEPTH, DEC_BATCH, H_B, DH_B, DH_B), 0.1),
        "state_mlstm_n": nrm((DEPTH, DEC_BATCH, H_B, DH_B), 0.1),
        "state_mlstm_m": nrm((DEPTH, DEC_BATCH, H_B), 0.5),
        "cache_mem_k": nrm((DEPTH, DEC_BATCH, N_MEM, H_M, DH_M)),
        "cache_mem_v": nrm((DEPTH, DEC_BATCH, N_MEM, H_M, DH_M)),
        "mem_prompt": nrm((BATCH, N_MEM, D_MODEL)),
        "g_norm": gain((DEPTH, D_MODEL)),
        "w_in": nrm((DEPTH, D_MODEL, N_IN), D_MODEL ** -0.5),
        "w_out": nrm((DEPTH, D_MIX, D_MODEL), D_MIX ** -0.5),
        "g_qa": gain((DEPTH, DK_A)),
        "g_ka": gain((DEPTH, DK_A)),
        "lam_q1": nrm((DEPTH, DK_A), 0.1),
        "lam_k1": nrm((DEPTH, DK_A), 0.1),
        "lam_q2": nrm((DEPTH, DK_A), 0.1),
        "lam_k2": nrm((DEPTH, DK_A), 0.1),
        "g_subln": gain((DEPTH, DV_A)),
        "b_i": nrm((DEPTH, H_B), 0.1),
        "b_f": jnp.linspace(3.0, 6.0, H_B, dtype=F32)[None, :] + nrm((DEPTH, H_B), 0.1),
        "g_mh": gain((DEPTH, DH_B)),
        "g_qm": gain((DEPTH, DH_M)),
        "g_km": gain((DEPTH, DH_M)),
        "g_mem": gain((DEPTH, D_MODEL)),
        "w_mk": nrm((DEPTH, D_MODEL, W_M), D_MODEL ** -0.5),
        "w_mv": nrm((DEPTH, D_MODEL, W_M), D_MODEL ** -0.5),
    }


def reference(x_prompt, x_sample, cache_attn_k, cache_attn_v, state_mlstm_C, state_mlstm_n,
              state_mlstm_m, cache_mem_k, cache_mem_v, mem_prompt, g_norm, w_in, w_out, g_qa,
              g_ka, lam_q1, lam_k1, lam_q2, lam_k2, g_subln, b_i, b_f, g_mh, g_qm, g_km, g_mem,
              w_mk, w_mv):
    Bp, Sp = x_prompt.shape[0], x_prompt.shape[1]
    Ss = x_sample.shape[1]
    P = cache_attn_k.shape[2]
    pos_p = jnp.arange(Sp, dtype=jnp.int32)
    pos_s = P + jnp.arange(Ss, dtype=jnp.int32)
    xp, xs = x_prompt, x_sample
    pk, pv, pC, pn, pm, pmk, pmv = [], [], [], [], [], [], []
    sk, sv, sC, sn, sm = [], [], [], [], []
    for l in range(DEPTH):
        lam_init = 0.8 - 0.6 * math.exp(-0.3 * l)
        w = (g_norm[l], w_in[l], w_out[l], g_qa[l], g_ka[l], lam_q1[l], lam_k1[l], lam_q2[l],
             lam_k2[l], g_subln[l], b_i[l], b_f[l], g_mh[l], g_qm[l])
        mk, mv = memory_kv(mem_prompt, g_mem[l], w_mk[l], w_mv[l], g_km[l])
        zero_state = (jnp.zeros((Bp, H_B, DH_B, DH_B), F32), jnp.zeros((Bp, H_B, DH_B), F32),
                      jnp.zeros((Bp, H_B), F32))
        xp, k_p, v_p, st_p = mixer_layer(xp, pos_p, None, None, zero_state, mk, mv, lam_init, *w)
        xs, k_s, v_s, st_s = mixer_layer(
            xs, pos_s, cache_attn_k[l], cache_attn_v[l],
            (state_mlstm_C[l], state_mlstm_n[l], state_mlstm_m[l]),
            cache_mem_k[l], cache_mem_v[l], lam_init, *w)
        pk.append(k_p); pv.append(v_p)
        pC.append(st_p[0].astype(x_prompt.dtype)); pn.append(st_p[1].astype(x_prompt.dtype))
        pm.append(st_p[2].astype(x_prompt.dtype))
        pmk.append(mk); pmv.append(mv)
        sk.append(k_s); sv.append(v_s)
        sC.append(st_s[0].astype(state_mlstm_C.dtype)); sn.append(st_s[1].astype(state_mlstm_n.dtype))
        sm.append(st_s[2].astype(state_mlstm_m.dtype))
    return (xp, xs, jnp.stack(pk), jnp.stack(pv), jnp.stack(pC), jnp.stack(pn), jnp.stack(pm),
            jnp.stack(pmk), jnp.stack(pmv), jnp.stack(sk), jnp.stack(sv), jnp.stack(sC),
            jnp.stack(sn), jnp.stack(sm))
```

```python
import functools
import math

import jax
import jax.numpy as jnp
from jax import lax
from jax.experimental import pallas as pl
from jax.experimental.pallas import tpu as pltpu

F32 = jnp.float32
BF16 = jnp.bfloat16

D_MODEL = 1024
CHUNK = 64
N_MEM = 256
H_A, DK_A = 4, 64
DV_A = 2 * DK_A
W_A = H_A * DV_A
H_B, DH_B = 4, 64
W_B = H_B * DH_B
H_M, DH_M = 4, 64
W_M = H_M * DH_M
D_MIX = W_A + W_B + W_M
EPS = 1e-6
ALIBI_SLOPES = tuple(2.0 ** (-8.0 * (h + 1) / H_A) for h in range(H_A))
N_IN = 4 * W_A + 5 * W_B + 2 * H_B + 2 * W_M

LANES = 128
MXU_DIM = 256
NEG_BIG = -1e30

C_QA, C_KA, C_VA, C_GA = 0, W_A, 2 * W_A, 3 * W_A
C_QB = 4 * W_A
C_KB, C_VB, C_OB, C_GB = C_QB + W_B, C_QB + 2 * W_B, C_QB + 3 * W_B, C_QB + 4 * W_B
C_QM = C_GB + W_B
C_GM = C_QM + W_M
C_IF = C_GM + W_M
N_IN_PAD = C_IF + LANES

VMEM_LIMIT = 56 * 1024 * 1024


def _dot(a, b):
    return jnp.dot(a, b, preferred_element_type=F32)


def _dot_nt(a, b):
    return lax.dot_general(a, b, (((1,), (1,)), ((), ())), preferred_element_type=F32)


def _split3(x):
    hi = x.astype(BF16)
    r1 = x - hi.astype(F32)
    mid = r1.astype(BF16)
    lo = (r1 - mid.astype(F32)).astype(BF16)
    return hi, mid, lo


def _group_mean_sq(z, g_mat):
    zz = (z * z).astype(BF16)
    parts = [_dot(zz[:, c:c + MXU_DIM], g_mat) for c in range(0, z.shape[1], MXU_DIM)]
    return parts[0] if len(parts) == 1 else jnp.concatenate(parts, axis=1)


def _group_rms_norm(z, gain_row, g_mat):
    return z * lax.rsqrt(_group_mean_sq(z, g_mat) + EPS) * gain_row


def _log_sigmoid(u):
    return -(jnp.maximum(-u, 0.0) + jnp.log1p(jnp.exp(-jnp.abs(u))))


def _silu(u):
    return u * jax.nn.sigmoid(u)


def _memory_kv_kernel(mem_ref, gmem_ref, wk_ref, wv_ref, gkm_ref, gmat_ref, mk_ref, mv_ref):
    x = mem_ref[...]
    ms = jnp.mean(x * x, axis=-1, keepdims=True)
    hm = (x * lax.rsqrt(ms + EPS) * gmem_ref[...]).astype(BF16)
    mk = _dot(hm, wk_ref[...])
    mk_ref[...] = _group_rms_norm(mk, gkm_ref[...], gmat_ref[...])
    mv_ref[...] = _dot(hm, wv_ref[...])


def _memory_kv(mem, g_mem, w_mk, w_mv, g_km, g_mat):
    b, n, d = mem.shape
    row = lambda i: (i, 0, 0)
    const2 = lambda i: (0, 0)
    return pl.pallas_call(
        _memory_kv_kernel,
        grid=(b,),
        in_specs=[
            pl.BlockSpec((None, n, d), row),
            pl.BlockSpec((1, d), const2),
            pl.BlockSpec((d, W_M), const2),
            pl.BlockSpec((d, W_M), const2),
            pl.BlockSpec((1, W_M), const2),
            pl.BlockSpec((MXU_DIM, MXU_DIM), const2),
        ],
        out_specs=[pl.BlockSpec((None, n, W_M), row), pl.BlockSpec((None, n, W_M), row)],
        out_shape=[jax.ShapeDtypeStruct((b, n, W_M), F32)] * 2,
        compiler_params=pltpu.CompilerParams(
            dimension_semantics=("arbitrary",), vmem_limit_bytes=VMEM_LIMIT),
        name="memory_kv",
    )(mem, g_mem.reshape(1, d), w_mk.astype(BF16), w_mv.astype(BF16),
      jnp.tile(g_km, H_M).reshape(1, W_M), g_mat)


def _in_proj_kernel(x_ref, gn_ref, w_ref, gmat_ref, gqa_ref, gka_ref, gqm_ref, bif_ref,
                    qaug_ref, k_ref, v_ref, ga_ref, mb_ref, gl_ref, og_ref, qmg_ref):
    x = x_ref[...]
    ms = jnp.mean(x * x, axis=-1, keepdims=True)
    h = (x * lax.rsqrt(ms + EPS) * gn_ref[...]).astype(BF16)
    g_mat = gmat_ref[...]

    def proj(start, width):
        return _dot(h, w_ref[:, start:start + width])

    qn = _group_rms_norm(proj(C_QA, W_A), gqa_ref[...], g_mat) * (DK_A ** -0.5)
    lane = lax.broadcasted_iota(jnp.int32, (x.shape[0], LANES), 1)
    aug = jnp.where(lane == DK_A, 1.0, 0.0).astype(F32)
    for hd in range(H_A):
        slab = qn[:, hd * DV_A:(hd + 1) * DV_A]
        q1 = jnp.where(lane < DK_A, slab, aug)
        q2 = jnp.where(lane < DK_A, pltpu.roll(slab, DK_A, 1), aug)
        qaug_ref[:, (2 * hd) * LANES:(2 * hd + 1) * LANES] = q1.astype(BF16)
        qaug_ref[:, (2 * hd + 1) * LANES:(2 * hd + 2) * LANES] = q2.astype(BF16)

    k_ref[...] = _group_rms_norm(proj(C_KA, W_A), gka_ref[...], g_mat)
    v_ref[...] = proj(C_VA, W_A)
    ga_ref[...] = _silu(proj(C_GA, W_A))

    mb_ref[:, 0:W_B] = proj(C_QB, W_B)
    mb_ref[:, W_B:2 * W_B] = proj(C_KB, W_B) * (DH_B ** -0.5)
    mb_ref[:, 2 * W_B:3 * W_B] = proj(C_VB, W_B)
    og_ref[:, 0:W_B] = jax.nn.sigmoid(proj(C_OB, W_B))
    og_ref[:, W_B:2 * W_B] = _silu(proj(C_GB, W_B))

    qmg_ref[:, 0:W_M] = _group_rms_norm(proj(C_QM, W_M), gqm_ref[...], g_mat) * (DH_M ** -0.5)
    qmg_ref[:, W_M:2 * W_M] = _silu(proj(C_GM, W_M))

    u = proj(C_IF, LANES) + bif_ref[...]
    gl_ref[...] = jnp.where(lane < H_B, u, _log_sigmoid(u))


def _in_proj(x2d, tm, g_norm, w_perm, g_mat, g_qa, g_ka, g_qm, b_if):
    n = x2d.shape[0]
    row = lambda i: (i, 0)
    const = lambda i: (0, 0)
    widths = (2 * W_A, W_A, W_A, W_A, 3 * W_B, LANES, 2 * W_B, 2 * W_M)
    dtypes = (BF16,) + (F32,) * 7
    return pl.pallas_call(
        _in_proj_kernel,
        grid=(n // tm,),
        in_specs=[
            pl.BlockSpec((tm, D_MODEL), row),
            pl.BlockSpec((1, D_MODEL), const),
            pl.BlockSpec((D_MODEL, N_IN_PAD), const),
            pl.BlockSpec((MXU_DIM, MXU_DIM), const),
            pl.BlockSpec((1, W_A), const),
            pl.BlockSpec((1, W_A), const),
            pl.BlockSpec((1, W_M), const),
            pl.BlockSpec((1, LANES), const),
        ],
        out_specs=[pl.BlockSpec((tm, w), row) for w in widths],
        out_shape=[jax.ShapeDtypeStruct((n, w), dt) for w, dt in zip(widths, dtypes)],
        compiler_params=pltpu.CompilerParams(
            dimension_semantics=("arbitrary",), vmem_limit_bytes=VMEM_LIMIT),
        name="in_proj",
    )(x2d, g_norm, w_perm, g_mat, g_qa, g_ka, g_qm, b_if)


def _lambda_value(lq1_ref, lk1_ref, lq2_ref, lk2_ref, lam_init):
    s1 = jnp.sum(lq1_ref[...] * lk1_ref[...], axis=-1, keepdims=True)
    s2 = jnp.sum(lq2_ref[...] * lk2_ref[...], axis=-1, keepdims=True)
    return jnp.exp(s1) - jnp.exp(s2) + lam_init


def _subln(o, gsub_row, lam_init):
    ms = jnp.mean(o * o, axis=-1, keepdims=True)
    return o * lax.rsqrt(ms + EPS) * gsub_row * (1.0 - lam_init)


def _prompt_attn_kernel(lq1_ref, lk1_ref, lq2_ref, lk2_ref, gsub_ref, q_ref, k_ref, v_ref,
                        o_ref, kaug_ref, vbf_ref, *, tile, lam_init):
    i = pl.program_id(1)
    seq = k_ref.shape[0]

    @pl.when(i == 0)
    def _stage_keys():
        lane = lax.broadcasted_iota(jnp.int32, (seq, LANES), 1)
        pos_in_tile = (lax.broadcasted_iota(jnp.int32, (seq, LANES), 0) % tile).astype(F32)
        for hd in range(H_A):
            slab = k_ref[:, hd * DV_A:(hd + 1) * DV_A]
            aug = jnp.where(lane == DK_A, ALIBI_SLOPES[hd] * pos_in_tile, 0.0)
            kaug_ref[2 * hd] = jnp.where(lane < DK_A, slab, aug).astype(BF16)
            kaug_ref[2 * hd + 1] = jnp.where(
                lane < DK_A, pltpu.roll(slab, DK_A, 1), aug).astype(BF16)
        vbf_ref[...] = v_ref[...].astype(BF16)

    lam = _lambda_value(lq1_ref, lk1_ref, lq2_ref, lk2_ref, lam_init)
    row = lax.broadcasted_iota(jnp.int32, (tile, tile), 0)
    col = lax.broadcasted_iota(jnp.int32, (tile, tile), 1)
    ahead = (col - row).astype(F32)
    diag_start = pl.multiple_of(i * tile, tile)

    for hd in range(H_A):
        slope = ALIBI_SLOPES[hd]
        diag_bias = jnp.where(col // CHUNK > row // CHUNK, NEG_BIG,
                              jnp.where(col > row, (-2.0 * slope) * ahead, 0.0))
        v_diag = vbf_ref[pl.ds(diag_start, tile), hd * DV_A:(hd + 1) * DV_A]
        diag_off = slope * (i * tile).astype(F32)
        state = []
        qs = []
        for m in range(2):
            q = q_ref[:, (2 * hd + m) * LANES:(2 * hd + m + 1) * LANES]
            qs.append(q)
            s = _dot_nt(q, kaug_ref[2 * hd + m, pl.ds(diag_start, tile), :]) + diag_bias
            mx = jnp.max(s, axis=-1, keepdims=True)
            p = jnp.exp(s - mx)
            state += [mx + diag_off, jnp.sum(p, axis=-1, keepdims=True),
                      _dot(p.astype(BF16), v_diag)]

        def past_tile(j, carry, hd=hd, slope=slope, qs=qs):
            start = pl.multiple_of(j * tile, tile)
            v_t = vbf_ref[pl.ds(start, tile), hd * DV_A:(hd + 1) * DV_A]
            off = slope * (j * tile).astype(F32)
            out = []
            for m in range(2):
                mx, l, acc = carry[3 * m:3 * m + 3]
                s = _dot_nt(qs[m], kaug_ref[2 * hd + m, pl.ds(start, tile), :])
                mx_new = jnp.maximum(mx, jnp.max(s, axis=-1, keepdims=True) + off)
                p = jnp.exp(s - (mx_new - off))
                alpha = jnp.exp(mx - mx_new)
                out += [mx_new, alpha * l + jnp.sum(p, axis=-1, keepdims=True),
                        alpha * acc + _dot(p.astype(BF16), v_t)]
            return tuple(out)

        m1, l1, a1, m2, l2, a2 = lax.fori_loop(0, i, past_tile, tuple(state))
        o = a1 * (1.0 / l1) - lam * (a2 * (1.0 / l2))
        o_ref[:, hd * DV_A:(hd + 1) * DV_A] = _subln(o, gsub_ref[...], lam_init)


def _prompt_attn(qaug, k, v, lam_vecs, g_subln, lam_init, tile):
    b, s, _ = k.shape
    const = lambda bi, qi: (0, 0)
    kern = functools.partial(_prompt_attn_kernel, tile=tile, lam_init=lam_init)
    return pl.pallas_call(
        kern,
        grid=(b, s // tile),
        in_specs=[pl.BlockSpec((1, DK_A), const)] * 4 + [
            pl.BlockSpec((1, DV_A), const),
            pl.BlockSpec((None, tile, 2 * W_A), lambda bi, qi: (bi, qi, 0)),
            pl.BlockSpec((None, s, W_A), lambda bi, qi: (bi, 0, 0)),
            pl.BlockSpec((None, s, W_A), lambda bi, qi: (bi, 0, 0)),
        ],
        out_specs=pl.BlockSpec((None, tile, W_A), lambda bi, qi: (bi, qi, 0)),
        out_shape=jax.ShapeDtypeStruct((b, s, W_A), F32),
        scratch_shapes=[pltpu.VMEM((2 * H_A, s, LANES), BF16), pltpu.VMEM((s, W_A), BF16)],
        compiler_params=pltpu.CompilerParams(
            dimension_semantics=("arbitrary", "arbitrary"), vmem_limit_bytes=VMEM_LIMIT),
        name="diff_attn_prompt",
    )(*lam_vecs, g_subln, qaug, k, v)


def _sample_attn_kernel(lq1_ref, lk1_ref, lq2_ref, lk2_ref, gsub_ref, q_ref, kc_ref, vc_ref,
                        kn_ref, vn_ref, o_ref, *, lam_init):
    sq = q_ref.shape[0]
    past = kc_ref.shape[0]
    lam = _lambda_value(lq1_ref, lk1_ref, lq2_ref, lk2_ref, lam_init)
    lane = lax.broadcasted_iota(jnp.int32, (sq, LANES), 1)
    qpos_c = past + lax.broadcasted_iota(jnp.int32, (sq, past), 0)
    dist_c = jnp.abs(qpos_c - lax.broadcasted_iota(jnp.int32, (sq, past), 1)).astype(F32)
    dist_n = jnp.abs(lax.broadcasted_iota(jnp.int32, (sq, sq), 0)
                     - lax.broadcasted_iota(jnp.int32, (sq, sq), 1)).astype(F32)
    for hd in range(H_A):
        slope = ALIBI_SLOPES[hd]
        kc = kc_ref[:, hd * DV_A:(hd + 1) * DV_A].astype(BF16)
        kn = kn_ref[:, hd * DV_A:(hd + 1) * DV_A].astype(BF16)
        vc = vc_ref[:, hd * DV_A:(hd + 1) * DV_A].astype(BF16)
        vn = vn_ref[:, hd * DV_A:(hd + 1) * DV_A].astype(BF16)
        outs = []
        for m in range(2):
            q = q_ref[:, (2 * hd + m) * LANES:(2 * hd + m + 1) * LANES].astype(F32)
            if m == 0:
                q = jnp.where(lane < DK_A, q, 0.0)
            else:
                q = jnp.where(lane >= DK_A, pltpu.roll(q, DK_A, 1), 0.0)
            q = q.astype(BF16)
            s_c = _dot_nt(q, kc) - slope * dist_c
            s_n = _dot_nt(q, kn) - slope * dist_n
            mx = jnp.maximum(jnp.max(s_c, axis=-1, keepdims=True),
                             jnp.max(s_n, axis=-1, keepdims=True))
            p_c = jnp.exp(s_c - mx)
            p_n = jnp.exp(s_n - mx)
            l = jnp.sum(p_c, axis=-1, keepdims=True) + jnp.sum(p_n, axis=-1, keepdims=True)
            acc = _dot(p_c.astype(BF16), vc) + _dot(p_n.astype(BF16), vn)
            outs.append(acc * (1.0 / l))
        o = outs[0] - lam * outs[1]
        o_ref[:, hd * DV_A:(hd + 1) * DV_A] = _subln(o, gsub_ref[...], lam_init)


def _sample_attn(qaug, k_cache, v_cache, k_new, v_new, lam_vecs, g_subln, lam_init):
    b, past, _ = k_cache.shape
    sq = k_new.shape[1]
    assert (past + sq - 1) // CHUNK <= past // CHUNK
    const = lambda bi: (0, 0)
    per_b = lambda bi: (bi, 0, 0)
    kern = functools.partial(_sample_attn_kernel, lam_init=lam_init)
    return pl.pallas_call(
        kern,
        grid=(b,),
        in_specs=[pl.BlockSpec((1, DK_A), const)] * 4 + [
            pl.BlockSpec((1, DV_A), const),
            pl.BlockSpec((None, sq, 2 * W_A), per_b),
            pl.BlockSpec((None, past, W_A), per_b),
            pl.BlockSpec((None, past, W_A), per_b),
            pl.BlockSpec((None, sq, W_A), per_b),
            pl.BlockSpec((None, sq, W_A), per_b),
        ],
        out_specs=pl.BlockSpec((None, sq, W_A), per_b),
        out_shape=jax.ShapeDtypeStruct((b, sq, W_A), F32),
        compiler_params=pltpu.CompilerParams(
            dimension_semantics=("arbitrary",), vmem_limit_bytes=VMEM_LIMIT),
        name="diff_attn_sample",
    )(*lam_vecs, g_subln, qaug, k_cache, v_cache, k_new, v_new)


def _mlstm_kernel(mb_ref, gl_ref, c0_ref, n0_ref, m0_ref,
                  h_ref, c_ref, n_ref, m_ref, *, valid):
    c_idx = pl.program_id(1)
    L = CHUNK
    W = W_B

    @pl.when(c_idx == 0)
    def _load_state():
        c_ref[...] = c0_ref[...]
        n_ref[...] = n0_ref[...]
        m_ref[...] = m0_ref[...]

    lane_head = lax.broadcasted_iota(jnp.int32, (L, W), 1) // DH_B
    row_t = lax.broadcasted_iota(jnp.int32, (L, W), 0)
    lane_s = lax.broadcasted_iota(jnp.int32, (L, W), 1) % DH_B
    blk = (lax.broadcasted_iota(jnp.int32, (W, W), 0) // DH_B
           == lax.broadcasted_iota(jnp.int32, (W, W), 1) // DH_B)

    def head_bcast(cols):
        out = jnp.zeros((L, W), F32)
        for hd in range(H_B):
            out = jnp.where(lane_head == hd, cols[hd], out)
        return out

    def group_reduce(x, fn, fill):
        return head_bcast([fn(jnp.where(lane_head == hd, x, fill), axis=-1, keepdims=True)
                           for hd in range(H_B)])

    q = mb_ref[:, 0:W]
    k = mb_ref[:, W:2 * W]
    v = mb_ref[:, 2 * W:3 * W]
    gl = gl_ref[...]
    t_col = lax.broadcasted_iota(jnp.int32, (L, 1), 0)
    ig_cols = [jnp.where(t_col < valid, gl[:, hd:hd + 1], NEG_BIG) for hd in range(H_B)]
    lf_cols = [jnp.where(t_col < valid, gl[:, H_B + hd:H_B + hd + 1], 0.0) for hd in range(H_B)]
    ig_all = head_bcast(ig_cols)
    lf_all = head_bcast(lf_cols)

    tri = (lax.broadcasted_iota(jnp.int32, (L, L), 1)
           <= lax.broadcasted_iota(jnp.int32, (L, L), 0)).astype(BF16)
    b_all = sum(_dot(tri, part) for part in _split3(lf_all))

    r_all = b_all - ig_all
    r_row = jnp.sum(jnp.where(lane_s == row_t, r_all, 0.0), axis=0, keepdims=True)

    m0_row = m_ref[...]
    c0 = c_ref[...]
    n0_row = n_ref[...]

    d_mat = jnp.where(lane_s <= row_t, b_all - r_row, NEG_BIG)
    g_all = b_all + m0_row
    m_all = jnp.maximum(g_all, group_reduce(d_mat, jnp.max, NEG_BIG))
    d_w = jnp.exp(d_mat - m_all)
    g_w = jnp.exp(g_all - m_all)

    q_bf = q.astype(BF16)
    k_bd = jnp.where(blk, jnp.concatenate([k.T] * H_B, axis=1), 0.0).astype(BF16)
    v_bd = jnp.where(blk, jnp.concatenate([v] * H_B, axis=0), 0.0).astype(BF16)
    a_mat = _dot(q_bf, k_bd) * d_w
    num = g_w * _dot(q_bf, c0.astype(BF16)) + _dot(a_mat.astype(BF16), v_bd)
    den = g_w * group_reduce(q * n0_row, jnp.sum, 0.0) + group_reduce(a_mat, jnp.sum, 0.0)
    h_ref[...] = num / jnp.maximum(jnp.abs(den), jnp.exp(-m_all))

    m_last = m_all[L - 1:L, :]
    b_last = b_all[L - 1:L, :]
    w_s = jnp.exp(b_last - b_all + ig_all - m_last)
    decay = jnp.exp(b_last + m0_row - m_last)
    kw = k * w_s
    c_ref[...] = decay * c0 + jnp.where(blk, _dot(kw.T.astype(BF16), v.astype(BF16)), 0.0)
    n_ref[...] = decay * n0_row + jnp.sum(kw, axis=0, keepdims=True)
    m_ref[...] = m_last


def _mlstm(mb, gl, c0_bd, n0_row, m0_row, valid):
    b, s, _ = mb.shape
    nc = s // CHUNK
    tok =lambda bi, ci: (bi, ci, 0)
    st = lambda bi, ci: (bi, 0, 0)
    kern = functools.partial(_mlstm_kernel, valid=valid)
    return pl.pallas_call(
        kern,
        grid=(b, nc),
        in_specs=[
            pl.BlockSpec((None, CHUNK, 3 * W_B), tok),
            pl.BlockSpec((None, CHUNK, LANES), tok),
            pl.BlockSpec((None, W_B, W_B), st),
            pl.BlockSpec((None, 1, W_B), st),
            pl.BlockSpec((None, 1, W_B), st),
        ],
        out_specs=[
            pl.BlockSpec((None, CHUNK, W_B), tok),
            pl.BlockSpec((None, W_B, W_B), st),
            pl.BlockSpec((None, 1, W_B), st),
            pl.BlockSpec((None, 1, W_B), st),
        ],
        out_shape=[
            jax.ShapeDtypeStruct((b, s, W_B), F32),
            jax.ShapeDtypeStruct((b, W_B, W_B), F32),
            jax.ShapeDtypeStruct((b, 1, W_B), F32),
            jax.ShapeDtypeStruct((b, 1, W_B), F32),
        ],
        compiler_params=pltpu.CompilerParams(
            dimension_semantics=("arbitrary", "arbitrary"), vmem_limit_bytes=VMEM_LIMIT),
        name="mlstm",
    )(mb, gl, c0_bd, n0_row, m0_row)


def _mix_out_kernel(x_ref, attn_ref, ga_ref, hb_ref, og_ref, qmg_ref, mk_ref, mv_ref,
                    gmh_ref, gmat_ref, wo_ref, y_ref):
    oa = attn_ref[...] * ga_ref[...]
    hb = _group_rms_norm(hb_ref[...], gmh_ref[...], gmat_ref[...])
    hb = hb * og_ref[:, 0:W_B] * og_ref[:, W_B:2 * W_B]

    qm = qmg_ref[:, 0:W_M].astype(BF16)
    mk = mk_ref[...]
    mv = mv_ref[...]
    feat_head = lax.broadcasted_iota(jnp.int32, mk.shape, 1) // DH_M
    om = jnp.zeros((x_ref.shape[0], W_M), F32)
    for hd in range(H_M):
        s = _dot_nt(qm, jnp.where(feat_head == hd, mk, 0.0).astype(BF16))
        p = jnp.exp(s - jnp.max(s, axis=-1, keepdims=True))
        p = p * (1.0 / jnp.sum(p, axis=-1, keepdims=True))
        om = om + _dot(p.astype(BF16), jnp.where(feat_head == hd, mv, 0.0).astype(BF16))
    om = om * qmg_ref[:, W_M:2 * W_M]

    y = x_ref[...] + _dot(oa.astype(BF16), wo_ref[0:W_A, :])
    y = y + _dot(hb.astype(BF16), wo_ref[W_A:W_A + W_B, :])
    y_ref[...] = y + _dot(om.astype(BF16), wo_ref[W_A + W_B:D_MIX, :])


def _mix_out(x, attn, ga, hb, og, qmg, mem_k, mem_v, g_mh_row, g_mat, w_out_bf, tm):
    b, s, _ = x.shape
    tok = lambda bi, ti: (bi, ti, 0)
    per_b = lambda bi, ti: (bi, 0, 0)
    const = lambda bi, ti: (0, 0)
    return pl.pallas_call(
        _mix_out_kernel,
        grid=(b, s // tm),
        in_specs=[
            pl.BlockSpec((None, tm, D_MODEL), tok),
            pl.BlockSpec((None, tm, W_A), tok),
            pl.BlockSpec((None, tm, W_A), tok),
            pl.BlockSpec((None, tm, W_B), tok),
            pl.BlockSpec((None, tm, 2 * W_B), tok),
            pl.BlockSpec((None, tm, 2 * W_M), tok),
            pl.BlockSpec((None, N_MEM, W_M), per_b),
            pl.BlockSpec((None, N_MEM, W_M), per_b),
            pl.BlockSpec((1, W_B), const),
            pl.BlockSpec((MXU_DIM, MXU_DIM), const),
            pl.BlockSpec((D_MIX, D_MODEL), const),
        ],
        out_specs=pl.BlockSpec((None, tm, D_MODEL), tok),
        out_shape=jax.ShapeDtypeStruct((b, s, D_MODEL), F32),
        compiler_params=pltpu.CompilerParams(
            dimension_semantics=("arbitrary", "arbitrary"), vmem_limit_bytes=VMEM_LIMIT),
        name="mix_out",
    )(x, attn, ga, hb, og, qmg, mem_k, mem_v, g_mh_row, g_mat, w_out_bf)


def _block_diag_state(c):
    b = c.shape[0]
    eye = jnp.eye(H_B, dtype=c.dtype)
    return jnp.einsum('bhde,hg->bhdge', c, eye).reshape(b, W_B, W_B)


def _diag_blocks(c_bd):
    b = c_bd.shape[0]
    c5 = c_bd.reshape(b, H_B, DH_B, H_B, DH_B)
    return jnp.stack([c5[:, h, :, h, :] for h in range(H_B)], axis=1)


def _mixer_layer(x, past_kv, mlstm_state, mem_k, mem_v, lam_init, p):
    b, s, _ = x.shape
    n = b * s
    tm = min(256, n)
    qaug, k_new, v_new, ga, mb, gl, og, qmg = _in_proj(
        x.reshape(n, D_MODEL), tm, p["g_norm"], p["w_perm"], p["g_mat"],
        p["g_qa"], p["g_ka"], p["g_qm"], p["b_if"])
    k_new = k_new.reshape(b, s, W_A)
    v_new = v_new.reshape(b, s, W_A)
    qaug = qaug.reshape(b, s, 2 * W_A)

    if past_kv is None:
        attn = _prompt_attn(qaug, k_new, v_new, p["lam_vecs"], p["g_subln"], lam_init, tile=256)
    else:
        attn = _sample_attn(qaug, past_kv[0], past_kv[1], k_new, v_new,
                            p["lam_vecs"], p["g_subln"], lam_init)

    c0, n0, m0 = mlstm_state
    s_pad = -(-s // CHUNK) * CHUNK
    mb3 = mb.reshape(b, s, 3 * W_B)
    gl3 = gl.reshape(b, s, LANES)
    if s_pad != s:
        mb3 = jnp.pad(mb3, ((0, 0), (0, s_pad - s), (0, 0)))
        gl3 = jnp.pad(gl3, ((0, 0), (0, s_pad - s), (0, 0)))
    valid = CHUNK if s_pad == s else s
    hb, c_bd, n_row, m_row = _mlstm(
        mb3, gl3, _block_diag_state(c0), n0.reshape(b, 1, W_B),
        jnp.repeat(m0, DH_B, axis=-1).reshape(b, 1, W_B), valid)
    hb = hb[:, :s]
    new_state = (_diag_blocks(c_bd), n_row.reshape(b, H_B, DH_B),
                 m_row.reshape(b, H_B, DH_B)[:, :, 0])

    y = _mix_out(x, attn, ga.reshape(b, s, W_A), hb, og.reshape(b, s, 2 * W_B),
                 qmg.reshape(b, s, 2 * W_M), mem_k, mem_v, p["g_mh"], p["g_mat"],
                 p["w_out"], tm=min(256, s))
    return (y, k_new.reshape(b, s, H_A, DV_A), v_new.reshape(b, s, H_A, DV_A), new_state)


def kernel(x_prompt, x_sample, cache_attn_k, cache_attn_v, state_mlstm_C, state_mlstm_n,
           state_mlstm_m, cache_mem_k, cache_mem_v, mem_prompt, g_norm, w_in, w_out, g_qa,
           g_ka, lam_q1, lam_k1, lam_q2, lam_k2, g_subln, b_i, b_f, g_mh, g_qm, g_km, g_mem,
           w_mk, w_mv):
    depth = w_in.shape[0]
    bp = x_prompt.shape[0]
    bs = x_sample.shape[0]
    past = cache_attn_k.shape[2]
    gi = lax.broadcasted_iota(jnp.int32, (MXU_DIM, MXU_DIM), 0) // DH_B
    gj = lax.broadcasted_iota(jnp.int32, (MXU_DIM, MXU_DIM), 1) // DH_B
    g_mat = jnp.where(gi == gj, 1.0 / DH_B, 0.0).astype(BF16)

    xp, xs = x_prompt, x_sample
    outs = {name: [] for name in ("pk", "pv", "pC", "pn", "pm", "pmk", "pmv",
                                  "sk", "sv", "sC", "sn", "sm")}
    for l in range(depth):
        lam_init = 0.8 - 0.6 * math.exp(-0.3 * l)
        w = w_in[l]
        w_perm = jnp.concatenate(
            [w[:, :4 * W_A + 4 * W_B], w[:, 4 * W_A + 4 * W_B + 2 * H_B:],
             w[:, 4 * W_A + 4 * W_B:4 * W_A + 4 * W_B + 2 * H_B],
             jnp.zeros((D_MODEL, LANES - 2 * H_B), F32)], axis=1).astype(BF16)
        p = {
            "g_norm": g_norm[l].reshape(1, D_MODEL),
            "w_perm": w_perm,
            "g_mat": g_mat,
            "g_qa": jnp.tile(g_qa[l], 2 * H_A).reshape(1, W_A),
            "g_ka": jnp.tile(g_ka[l], 2 * H_A).reshape(1, W_A),
            "g_qm": jnp.tile(g_qm[l], H_M).reshape(1, W_M),
            "b_if": jnp.concatenate(
                [b_i[l], b_f[l], jnp.zeros((LANES - 2 * H_B,), F32)]).reshape(1, LANES),
            "lam_vecs": tuple(v[l].reshape(1, DK_A) for v in (lam_q1, lam_k1, lam_q2, lam_k2)),
            "g_subln": g_subln[l].reshape(1, DV_A),
            "g_mh": jnp.tile(g_mh[l], H_B).reshape(1, W_B),
            "w_out": w_out[l].astype(BF16),
        }
        mk, mv = _memory_kv(mem_prompt, g_mem[l], w_mk[l], w_mv[l], g_km[l], g_mat)
        zero_state = (jnp.zeros((bp, H_B, DH_B, DH_B), F32), jnp.zeros((bp, H_B, DH_B), F32),
                      jnp.zeros((bp, H_B), F32))
        xp, k_p, v_p, st_p = _mixer_layer(xp, None, zero_state, mk, mv, lam_init, p)
        xs, k_s, v_s, st_s = _mixer_layer(
            xs, (cache_attn_k[l].reshape(bs, past, W_A), cache_attn_v[l].reshape(bs, past, W_A)),
            (state_mlstm_C[l], state_mlstm_n[l], state_mlstm_m[l]),
            cache_mem_k[l].reshape(bs, N_MEM, W_M), cache_mem_v[l].reshape(bs, N_MEM, W_M),
            lam_init, p)
        outs["pk"].append(k_p); outs["pv"].append(v_p)
        outs["pC"].append(st_p[0]); outs["pn"].append(st_p[1]); outs["pm"].append(st_p[2])
        outs["pmk"].append(mk.reshape(bp, N_MEM, H_M, DH_M))
        outs["pmv"].append(mv.reshape(bp, N_MEM, H_M, DH_M))
        outs["sk"].append(k_s); outs["sv"].append(v_s)
        outs["sC"].append(st_s[0]); outs["sn"].append(st_s[1]); outs["sm"].append(st_s[2])
    stk = {name: jnp.stack(vals) for name, vals in outs.items()}
    return (xp, xs, stk["pk"], stk["pv"], stk["pC"], stk["pn"], stk["pm"], stk["pmk"],
            stk["pmv"], stk["sk"], stk["sv"], stk["sC"], stk["sn"], stk["sm"])
```

```python
import functools
import math

import jax
import jax.numpy as jnp
import numpy as np
from jax import lax
from jax.experimental import pallas as pl
from jax.experimental.pallas import tpu as pltpu

F32 = jnp.float32
BF16 = jnp.bfloat16

D_MODEL = 1024
CHUNK = 64
N_MEM = 256
H_A, DK_A = 4, 64
DV_A = 2 * DK_A
W_A = H_A * DV_A
H_B, DH_B = 4, 64
W_B = H_B * DH_B
H_M, DH_M = 4, 64
W_M = H_M * DH_M
D_MIX = W_A + W_B + W_M
EPS = 1e-6
ALIBI_SLOPES = tuple(2.0 ** (-8.0 * (h + 1) / H_A) for h in range(H_A))
N_IN = 4 * W_A + 5 * W_B + 2 * H_B + 2 * W_M

LANES = 128
MXU_DIM = 256
NEG_BIG = -1e30
LOG2E = math.log2(math.e)


def _bf16_pieces(value, n=3):
    pieces, rest = [], np.float32(value)
    for _ in range(n):
        piece = np.float32(rest.astype(jnp.bfloat16))
        pieces.append(float(piece))
        rest = np.float32(rest - piece)
    return tuple(pieces)


LOG2E_BF16_PIECES = _bf16_pieces(LOG2E)

C_QA, C_KA, C_VA, C_GA = 0, W_A, 2 * W_A, 3 * W_A
C_QB = 4 * W_A
C_KB, C_VB, C_OB, C_GB = C_QB + W_B, C_QB + 2 * W_B, C_QB + 3 * W_B, C_QB + 4 * W_B
C_QM = C_GB + W_B
C_GM = C_QM + W_M
C_IF = C_GM + W_M
N_IN_PAD = C_IF + LANES

VMEM_LIMIT = 56 * 1024 * 1024


def _dot(a, b):
    return jnp.dot(a, b, preferred_element_type=F32)


def _dot_nt(a, b):
    return lax.dot_general(a, b, (((1,), (1,)), ((), ())), preferred_element_type=F32)


def _split3(x):
    hi = x.astype(BF16)
    r1 = x - hi.astype(F32)
    mid = r1.astype(BF16)
    lo = (r1 - mid.astype(F32)).astype(BF16)
    return hi, mid, lo


def _group_mean_sq(z, g_mat):
    zz = (z * z).astype(BF16)
    parts = [_dot(zz[:, c:c + MXU_DIM], g_mat) for c in range(0, z.shape[1], MXU_DIM)]
    return parts[0] if len(parts) == 1 else jnp.concatenate(parts, axis=1)


def _group_rms_norm(z, gain_row, g_mat):
    return z * lax.rsqrt(_group_mean_sq(z, g_mat) + EPS) * gain_row


def _log_sigmoid(u):
    return -(jnp.maximum(-u, 0.0) + jnp.log1p(jnp.exp(-jnp.abs(u))))


def _silu(u):
    return u * jax.nn.sigmoid(u)


def _memory_kv_kernel(mem_ref, gmem_ref, wk_ref, wv_ref, gkm_ref, gmat_ref, mk_ref, mv_ref):
    x = mem_ref[...]
    ms = jnp.mean(x * x, axis=-1, keepdims=True)
    hm = (x * lax.rsqrt(ms + EPS) * gmem_ref[...]).astype(BF16)
    mk = _dot(hm, wk_ref[...])
    mk_ref[...] = _group_rms_norm(mk, gkm_ref[...], gmat_ref[...])
    mv_ref[...] = _dot(hm, wv_ref[...])


def _memory_kv(mem, g_mem, w_mk, w_mv, g_km, g_mat):
    b, n, d = mem.shape
    row = lambda i: (i, 0, 0)
    const2 = lambda i: (0, 0)
    return pl.pallas_call(
        _memory_kv_kernel,
        grid=(b,),
        in_specs=[
            pl.BlockSpec((None, n, d), row),
            pl.BlockSpec((1, d), const2),
            pl.BlockSpec((d, W_M), const2),
            pl.BlockSpec((d, W_M), const2),
            pl.BlockSpec((1, W_M), const2),
            pl.BlockSpec((MXU_DIM, MXU_DIM), const2),
        ],
        out_specs=[pl.BlockSpec((None, n, W_M), row), pl.BlockSpec((None, n, W_M), row)],
        out_shape=[jax.ShapeDtypeStruct((b, n, W_M), F32)] * 2,
        compiler_params=pltpu.CompilerParams(
            dimension_semantics=("arbitrary",), vmem_limit_bytes=VMEM_LIMIT),
        name="memory_kv",
    )(mem, g_mem.reshape(1, d), w_mk.astype(BF16), w_mv.astype(BF16),
      jnp.tile(g_km, H_M).reshape(1, W_M), g_mat)


def _in_proj_kernel(x_ref, gn_ref, w_ref, gmat_ref, gqa_ref, gka_ref, gqm_ref, bif_ref,
                    qaug_ref, k_ref, v_ref, ga_ref, mb_ref, gl_ref, og_ref, qmg_ref):
    x = x_ref[...]
    ms = jnp.mean(x * x, axis=-1, keepdims=True)
    h = (x * lax.rsqrt(ms + EPS) * gn_ref[...]).astype(BF16)
    g_mat = gmat_ref[...]

    def proj(start, width):
        return _dot(h, w_ref[:, start:start + width])

    qn = _group_rms_norm(proj(C_QA, W_A), gqa_ref[...], g_mat) * (DK_A ** -0.5 * LOG2E)
    lane = lax.broadcasted_iota(jnp.int32, (x.shape[0], LANES), 1)
    aug = jnp.zeros((x.shape[0], LANES), F32)
    for rep in range(2):
        for piece, val in enumerate(LOG2E_BF16_PIECES):
            aug = jnp.where(lane == DK_A + 3 * rep + piece, val, aug)
    for hd in range(H_A):
        slab = qn[:, hd * DV_A:(hd + 1) * DV_A]
        q1 = jnp.where(lane < DK_A, slab, aug)
        q2 = jnp.where(lane < DK_A, pltpu.roll(slab, DK_A, 1), aug)
        qaug_ref[:, (2 * hd) * LANES:(2 * hd + 1) * LANES] = q1.astype(BF16)
        qaug_ref[:, (2 * hd + 1) * LANES:(2 * hd + 2) * LANES] = q2.astype(BF16)

    k_ref[...] = _group_rms_norm(proj(C_KA, W_A), gka_ref[...], g_mat)
    v_ref[...] = proj(C_VA, W_A)
    ga_ref[...] = _silu(proj(C_GA, W_A))

    mb_ref[:, 0:W_B] = proj(C_QB, W_B)
    mb_ref[:, W_B:2 * W_B] = proj(C_KB, W_B) * (DH_B ** -0.5)
    mb_ref[:, 2 * W_B:3 * W_B] = proj(C_VB, W_B)
    og_ref[:, 0:W_B] = jax.nn.sigmoid(proj(C_OB, W_B))
    og_ref[:, W_B:2 * W_B] = _silu(proj(C_GB, W_B))

    qmg_ref[:, 0:W_M] = _group_rms_norm(proj(C_QM, W_M), gqm_ref[...], g_mat) * (DH_M ** -0.5)
    qmg_ref[:, W_M:2 * W_M] = _silu(proj(C_GM, W_M))

    u = proj(C_IF, LANES) + bif_ref[...]
    gl_ref[...] = jnp.where(lane < H_B, u, _log_sigmoid(u))


def _in_proj(x2d, tm, g_norm, w_perm, g_mat, g_qa, g_ka, g_qm, b_if):
    n = x2d.shape[0]
    row = lambda i: (i, 0)
    const = lambda i: (0, 0)
    widths = (2 * W_A, W_A, W_A, W_A, 3 * W_B, LANES, 2 * W_B, 2 * W_M)
    dtypes = (BF16,) + (F32,) * 7
    return pl.pallas_call(
        _in_proj_kernel,
        grid=(n // tm,),
        in_specs=[
            pl.BlockSpec((tm, D_MODEL), row),
            pl.BlockSpec((1, D_MODEL), const),
            pl.BlockSpec((D_MODEL, N_IN_PAD), const),
            pl.BlockSpec((MXU_DIM, MXU_DIM), const),
            pl.BlockSpec((1, W_A), const),
            pl.BlockSpec((1, W_A), const),
            pl.BlockSpec((1, W_M), const),
            pl.BlockSpec((1, LANES), const),
        ],
        out_specs=[pl.BlockSpec((tm, w), row) for w in widths],
        out_shape=[jax.ShapeDtypeStruct((n, w), dt) for w, dt in zip(widths, dtypes)],
        compiler_params=pltpu.CompilerParams(
            dimension_semantics=("arbitrary",), vmem_limit_bytes=VMEM_LIMIT),
        name="in_proj",
    )(x2d, g_norm, w_perm, g_mat, g_qa, g_ka, g_qm, b_if)


def _lambda_value(lq1_ref, lk1_ref, lq2_ref, lk2_ref, lam_init):
    s1 = jnp.sum(lq1_ref[...] * lk1_ref[...], axis=-1, keepdims=True)
    s2 = jnp.sum(lq2_ref[...] * lk2_ref[...], axis=-1, keepdims=True)
    return jnp.exp(s1) - jnp.exp(s2) + lam_init


def _subln(o, gsub_row, lam_init):
    ms = jnp.mean(o * o, axis=-1, keepdims=True)
    return o * lax.rsqrt(ms + EPS) * gsub_row * (1.0 - lam_init)


def _prompt_attn_kernel(lq1_ref, lk1_ref, lq2_ref, lk2_ref, gsub_ref, q_ref, k_ref, v_ref,
                        o_ref, kaug_ref, vt_ref, m_ref, l_ref, acc_ref, s_ref, *, tile, lam_init):
    i = pl.program_id(1)
    seq = k_ref.shape[0]

    @pl.when(i == 0)
    def _stage_keys_values():
        lane = lax.broadcasted_iota(jnp.int32, (seq, LANES), 1)
        pos = lax.broadcasted_iota(jnp.int32, (seq, LANES), 0)
        in_tile = (pos % tile).astype(F32)
        tile_base = (pos - pos % tile).astype(F32)
        for hd in range(H_A):
            slab = k_ref[:, hd * DV_A:(hd + 1) * DV_A]
            aug = jnp.where((lane >= DK_A) & (lane < DK_A + 3), ALIBI_SLOPES[hd] * in_tile,
                            jnp.where((lane >= DK_A + 3) & (lane < DK_A + 6),
                                      ALIBI_SLOPES[hd] * tile_base, 0.0))
            kaug_ref[2 * hd] = jnp.where(lane < DK_A, slab, aug).astype(BF16)
            kaug_ref[2 * hd + 1] = jnp.where(
                lane < DK_A, pltpu.roll(slab, DK_A, 1), aug).astype(BF16)
        for t in range(seq // tile):
            vt_ref[t] = v_ref[t * tile:(t + 1) * tile, :].T.astype(BF16)

    n_chain = 2 * H_A

    def scores(c, j):
        start = pl.multiple_of(j * tile, tile)
        s_ref[c] = _dot_nt(kaug_ref[c, pl.ds(start, tile), :],
                           q_ref[:, c * LANES:(c + 1) * LANES])

    def softmax_values(c, j, bias_of_head):
        s = s_ref[c]
        if bias_of_head is not None:
            s = s + bias_of_head[c // 2]
        mx = m_ref[c]
        mx_new = jnp.maximum(mx, jnp.max(s, axis=0, keepdims=True))
        p = jnp.exp2(s - mx_new)
        alpha = jnp.exp2(mx - mx_new)
        m_ref[c] = mx_new
        l_ref[c] = alpha * l_ref[c] + jnp.sum(p, axis=0, keepdims=True)
        v_t = vt_ref[j, (c // 2) * DV_A:(c // 2 + 1) * DV_A, :]
        acc_ref[c] = alpha * acc_ref[c] + _dot(v_t, p.astype(BF16))

    m_ref[...] = jnp.full(m_ref.shape, NEG_BIG, F32)
    l_ref[...] = jnp.zeros(l_ref.shape, F32)
    acc_ref[...] = jnp.zeros(acc_ref.shape, F32)

    for c in range(n_chain):
        scores(c, 0)

    def past_tile(j, carry):
        for c in range(n_chain):
            softmax_values(c, j, None)
            scores(c, j + 1)
        return carry

    lax.fori_loop(0, i, past_tile, 0)

    key = lax.broadcasted_iota(jnp.int32, (tile, tile), 0)
    qry = lax.broadcasted_iota(jnp.int32, (tile, tile), 1)
    ahead = (key - qry).astype(F32)
    diag_bias = [jnp.where(key // CHUNK > qry // CHUNK, NEG_BIG,
                           jnp.where(key > qry, (-2.0 * LOG2E * ALIBI_SLOPES[hd]) * ahead, 0.0))
                 for hd in range(H_A)]
    for c in range(n_chain):
        softmax_values(c, i, diag_bias)

    lam = _lambda_value(lq1_ref, lk1_ref, lq2_ref, lk2_ref, lam_init)
    for hd in range(H_A):
        o1 = acc_ref[2 * hd] * (1.0 / l_ref[2 * hd])
        o2 = acc_ref[2 * hd + 1] * (1.0 / l_ref[2 * hd + 1])
        o = (o1 - lam * o2).T
        o_ref[:, hd * DV_A:(hd + 1) * DV_A] = _subln(o, gsub_ref[...], lam_init)


def _prompt_attn(qaug, k, v, lam_vecs, g_subln, lam_init, tile):
    b, s, _ = k.shape
    const = lambda bi, qi: (0, 0)
    kern = functools.partial(_prompt_attn_kernel, tile=tile, lam_init=lam_init)
    return pl.pallas_call(
        kern,
        grid=(b, s // tile),
        in_specs=[pl.BlockSpec((1, DK_A), const)] * 4 + [
            pl.BlockSpec((1, DV_A), const),
            pl.BlockSpec((None, tile, 2 * W_A), lambda bi, qi: (bi, qi, 0)),
            pl.BlockSpec((None, s, W_A), lambda bi, qi: (bi, 0, 0)),
            pl.BlockSpec((None, s, W_A), lambda bi, qi: (bi, 0, 0)),
        ],
        out_specs=pl.BlockSpec((None, tile, W_A), lambda bi, qi: (bi, qi, 0)),
        out_shape=jax.ShapeDtypeStruct((b, s, W_A), F32),
        scratch_shapes=[
            pltpu.VMEM((2 * H_A, s, LANES), BF16),
            pltpu.VMEM((s // tile, W_A, tile), BF16),
            pltpu.VMEM((2 * H_A, 1, tile), F32),
            pltpu.VMEM((2 * H_A, 1, tile), F32),
            pltpu.VMEM((2 * H_A, DV_A, tile), F32),
            pltpu.VMEM((2 * H_A, tile, tile), F32),
        ],
        compiler_params=pltpu.CompilerParams(
            dimension_semantics=("arbitrary", "arbitrary"), vmem_limit_bytes=VMEM_LIMIT),
        name="diff_attn_prompt",
    )(*lam_vecs, g_subln, qaug, k, v)


def _sample_attn_kernel(lq1_ref, lk1_ref, lq2_ref, lk2_ref, gsub_ref, q_ref, kc_ref, vc_ref,
                        kn_ref, vn_ref, o_ref, *, lam_init):
    sq = q_ref.shape[0]
    past = kc_ref.shape[0]
    lam = _lambda_value(lq1_ref, lk1_ref, lq2_ref, lk2_ref, lam_init)
    lane = lax.broadcasted_iota(jnp.int32, (sq, LANES), 1)
    qpos_c = past + lax.broadcasted_iota(jnp.int32, (sq, past), 0)
    dist_c = jnp.abs(qpos_c - lax.broadcasted_iota(jnp.int32, (sq, past), 1)).astype(F32)
    dist_n = jnp.abs(lax.broadcasted_iota(jnp.int32, (sq, sq), 0)
                     - lax.broadcasted_iota(jnp.int32, (sq, sq), 1)).astype(F32)
    for hd in range(H_A):
        slope = ALIBI_SLOPES[hd]
        kc = kc_ref[:, hd * DV_A:(hd + 1) * DV_A].astype(BF16)
        kn = kn_ref[:, hd * DV_A:(hd + 1) * DV_A].astype(BF16)
        vc = vc_ref[:, hd * DV_A:(hd + 1) * DV_A].astype(BF16)
        vn = vn_ref[:, hd * DV_A:(hd + 1) * DV_A].astype(BF16)
        outs = []
        for m in range(2):
            q = q_ref[:, (2 * hd + m) * LANES:(2 * hd + m + 1) * LANES].astype(F32)
            if m == 0:
                q = jnp.where(lane < DK_A, q, 0.0)
            else:
                q = jnp.where(lane >= DK_A, pltpu.roll(q, DK_A, 1), 0.0)
            q = q.astype(BF16)
            s_c = _dot_nt(q, kc) - (slope * LOG2E) * dist_c
            s_n = _dot_nt(q, kn) - (slope * LOG2E) * dist_n
            mx = jnp.maximum(jnp.max(s_c, axis=-1, keepdims=True),
                             jnp.max(s_n, axis=-1, keepdims=True))
            p_c = jnp.exp2(s_c - mx)
            p_n = jnp.exp2(s_n - mx)
            l = jnp.sum(p_c, axis=-1, keepdims=True) + jnp.sum(p_n, axis=-1, keepdims=True)
            acc = _dot(p_c.astype(BF16), vc) + _dot(p_n.astype(BF16), vn)
            outs.append(acc * (1.0 / l))
        o = outs[0] - lam * outs[1]
        o_ref[:, hd * DV_A:(hd + 1) * DV_A] = _subln(o, gsub_ref[...], lam_init)


def _sample_attn(qaug, k_cache, v_cache, k_new, v_new, lam_vecs, g_subln, lam_init):
    b, past, _ = k_cache.shape
    sq = k_new.shape[1]
    assert (past + sq - 1) // CHUNK <= past // CHUNK
    const = lambda bi: (0, 0)
    per_b = lambda bi: (bi, 0, 0)
    kern = functools.partial(_sample_attn_kernel, lam_init=lam_init)
    return pl.pallas_call(
        kern,
        grid=(b,),
        in_specs=[pl.BlockSpec((1, DK_A), const)] * 4 + [
            pl.BlockSpec((1, DV_A), const),
            pl.BlockSpec((None, sq, 2 * W_A), per_b),
            pl.BlockSpec((None, past, W_A), per_b),
            pl.BlockSpec((None, past, W_A), per_b),
            pl.BlockSpec((None, sq, W_A), per_b),
            pl.BlockSpec((None, sq, W_A), per_b),
        ],
        out_specs=pl.BlockSpec((None, sq, W_A), per_b),
        out_shape=jax.ShapeDtypeStruct((b, sq, W_A), F32),
        compiler_params=pltpu.CompilerParams(
            dimension_semantics=("arbitrary",), vmem_limit_bytes=VMEM_LIMIT),
        name="diff_attn_sample",
    )(*lam_vecs, g_subln, qaug, k_cache, v_cache, k_new, v_new)


def _mlstm_kernel(mb_ref, gl_ref, c0_ref, n0_ref, m0_ref,
                  h_ref, c_ref, n_ref, m_ref, *, valid):
    c_idx = pl.program_id(1)
    L = CHUNK
    W = W_B
    n_seq = mb_ref.shape[0]

    @pl.when(c_idx == 0)
    def _load_state():
        c_ref[...] = c0_ref[...]
        n_ref[...] = n0_ref[...]
        m_ref[...] = m0_ref[...]

    lane_head = lax.broadcasted_iota(jnp.int32, (L, W), 1) // DH_B
    row_t = lax.broadcasted_iota(jnp.int32, (L, W), 0)
    lane_s = lax.broadcasted_iota(jnp.int32, (L, W), 1) % DH_B
    wi = lax.broadcasted_iota(jnp.int32, (W, W), 0)
    wj = lax.broadcasted_iota(jnp.int32, (W, W), 1)
    blk = wi // DH_B == wj // DH_B
    ones_bd = blk.astype(BF16)
    ei = lax.broadcasted_iota(jnp.int32, (LANES, 2 * W), 0)
    ej = lax.broadcasted_iota(jnp.int32, (LANES, 2 * W), 1)
    expand = (ei == ej // DH_B).astype(BF16)
    tri = (lax.broadcasted_iota(jnp.int32, (L, L), 1)
           <= lax.broadcasted_iota(jnp.int32, (L, L), 0)).astype(BF16)
    nar_lane = lax.broadcasted_iota(jnp.int32, (L, LANES), 1)
    nar_t = lax.broadcasted_iota(jnp.int32, (L, LANES), 0)

    def dot3(x, mat, left=False):
        return sum((_dot(mat, part) if left else _dot(part, mat)) for part in _split3(x))

    def head_max(x):
        out = jnp.zeros((L, W), F32)
        for hd in range(H_B):
            mx = jnp.max(jnp.where(lane_head == hd, x, NEG_BIG), axis=-1, keepdims=True)
            out = jnp.where(lane_head == hd, mx, out)
        return out

    def chunk_stages(g):
        q = mb_ref[g, :, 0:W]
        k = mb_ref[g, :, W:2 * W]
        v = mb_ref[g, :, 2 * W:3 * W]
        gl = jnp.where(nar_lane < 2 * H_B, gl_ref[g], 0.0)
        gl = jnp.where(nar_t < valid, gl, jnp.where(nar_lane < H_B, NEG_BIG, 0.0))
        narrow = jnp.where(nar_lane < H_B, gl, dot3(gl, tri, left=True))
        wide = dot3(narrow, expand)
        ig_all = wide[:, 0:W]
        b_all = wide[:, W:2 * W]
        yield

        r_all = b_all - ig_all
        r_row = jnp.sum(jnp.where(lane_s == row_t, r_all, 0.0), axis=0, keepdims=True)
        m0_row = m_ref[g]
        c0 = c_ref[g]
        n0_row = n_ref[g]
        d_mat = jnp.where(lane_s <= row_t, b_all - r_row, NEG_BIG)
        g_all = b_all + m0_row
        m_all = jnp.maximum(g_all, head_max(d_mat))
        d_w = jnp.exp(d_mat - m_all)
        g_w = jnp.exp(g_all - m_all)
        yield

        q_bf = q.astype(BF16)
        k_bd = jnp.where(blk, jnp.concatenate([k.T] * H_B, axis=1), 0.0).astype(BF16)
        v_bd = jnp.where(blk, jnp.concatenate([v] * H_B, axis=0), 0.0).astype(BF16)
        a_mat = _dot(q_bf, k_bd) * d_w
        num = g_w * _dot(q_bf, c0.astype(BF16)) + _dot(a_mat.astype(BF16), v_bd)
        yield

        den = g_w * dot3(q * n0_row, ones_bd) + dot3(a_mat, ones_bd)
        h_ref[g] = num / jnp.maximum(jnp.abs(den), jnp.exp(-m_all))
        yield

        m_last = m_all[L - 1:L, :]
        b_last = b_all[L - 1:L, :]
        w_s = jnp.exp(b_last - b_all + ig_all - m_last)
        decay = jnp.exp(b_last + m0_row - m_last)
        kw = k * w_s
        c_ref[g] = decay * c0 + jnp.where(blk, _dot(kw.T.astype(BF16), v.astype(BF16)), 0.0)
        n_ref[g] = decay * n0_row + jnp.sum(kw, axis=0, keepdims=True)
        m_ref[g] = m_last
        yield

    stages = [chunk_stages(g) for g in range(n_seq)]
    for _ in range(5):
        for stage in stages:
            next(stage)


def _mlstm(mb, gl, c0_bd, n0_row, m0_row, valid, n_seq):
    b, s, _ = mb.shape
    nc = s // CHUNK
    tok = lambda bi, ci: (bi, ci, 0)
    st = lambda bi, ci: (bi, 0, 0)
    kern = functools.partial(_mlstm_kernel, valid=valid)
    return pl.pallas_call(
        kern,
        grid=(b // n_seq, nc),
        in_specs=[
            pl.BlockSpec((n_seq, CHUNK, 3 * W_B), tok),
            pl.BlockSpec((n_seq, CHUNK, LANES), tok),
            pl.BlockSpec((n_seq, W_B, W_B), st),
            pl.BlockSpec((n_seq, 1, W_B), st),
            pl.BlockSpec((n_seq, 1, W_B), st),
        ],
        out_specs=[
            pl.BlockSpec((n_seq, CHUNK, W_B), tok),
            pl.BlockSpec((n_seq, W_B, W_B), st),
            pl.BlockSpec((n_seq, 1, W_B), st),
            pl.BlockSpec((n_seq, 1, W_B), st),
        ],
        out_shape=[
            jax.ShapeDtypeStruct((b, s, W_B), F32),
            jax.ShapeDtypeStruct((b, W_B, W_B), F32),
            jax.ShapeDtypeStruct((b, 1, W_B), F32),
            jax.ShapeDtypeStruct((b, 1, W_B), F32),
        ],
        compiler_params=pltpu.CompilerParams(
            dimension_semantics=("arbitrary", "arbitrary"), vmem_limit_bytes=VMEM_LIMIT),
        name="mlstm",
    )(mb, gl, c0_bd, n0_row, m0_row)


def _mix_out_kernel(x_ref, attn_ref, ga_ref, hb_ref, og_ref, qmg_ref, mk_ref, mv_ref,
                    gmh_ref, gmat_ref, wo_ref, y_ref):
    oa = attn_ref[...] * ga_ref[...]
    hb = _group_rms_norm(hb_ref[...], gmh_ref[...], gmat_ref[...])
    hb = hb * og_ref[:, 0:W_B] * og_ref[:, W_B:2 * W_B]

    qm = qmg_ref[:, 0:W_M].astype(BF16)
    mk = mk_ref[...]
    mv = mv_ref[...]
    feat_head = lax.broadcasted_iota(jnp.int32, mk.shape, 1) // DH_M
    om = jnp.zeros((x_ref.shape[0], W_M), F32)
    for hd in range(H_M):
        s = _dot_nt(qm, jnp.where(feat_head == hd, mk, 0.0).astype(BF16))
        p = jnp.exp(s - jnp.max(s, axis=-1, keepdims=True))
        p = p * (1.0 / jnp.sum(p, axis=-1, keepdims=True))
        om = om + _dot(p.astype(BF16), jnp.where(feat_head == hd, mv, 0.0).astype(BF16))
    om = om * qmg_ref[:, W_M:2 * W_M]

    y = x_ref[...] + _dot(oa.astype(BF16), wo_ref[0:W_A, :])
    y = y + _dot(hb.astype(BF16), wo_ref[W_A:W_A + W_B, :])
    y_ref[...] = y + _dot(om.astype(BF16), wo_ref[W_A + W_B:D_MIX, :])


def _mix_out(x, attn, ga, hb, og, qmg, mem_k, mem_v, g_mh_row, g_mat, w_out_bf, tm):
    b, s, _ = x.shape
    tok = lambda bi, ti: (bi, ti, 0)
    per_b = lambda bi, ti: (bi, 0, 0)
    const = lambda bi, ti: (0, 0)
    return pl.pallas_call(
        _mix_out_kernel,
        grid=(b, s // tm),
        in_specs=[
            pl.BlockSpec((None, tm, D_MODEL), tok),
            pl.BlockSpec((None, tm, W_A), tok),
            pl.BlockSpec((None, tm, W_A), tok),
            pl.BlockSpec((None, tm, W_B), tok),
            pl.BlockSpec((None, tm, 2 * W_B), tok),
            pl.BlockSpec((None, tm, 2 * W_M), tok),
            pl.BlockSpec((None, N_MEM, W_M), per_b),
            pl.BlockSpec((None, N_MEM, W_M), per_b),
            pl.BlockSpec((1, W_B), const),
            pl.BlockSpec((MXU_DIM, MXU_DIM), const),
            pl.BlockSpec((D_MIX, D_MODEL), const),
        ],
        out_specs=pl.BlockSpec((None, tm, D_MODEL), tok),
        out_shape=jax.ShapeDtypeStruct((b, s, D_MODEL), F32),
        compiler_params=pltpu.CompilerParams(
            dimension_semantics=("arbitrary", "arbitrary"), vmem_limit_bytes=VMEM_LIMIT),
        name="mix_out",
    )(x, attn, ga, hb, og, qmg, mem_k, mem_v, g_mh_row, g_mat, w_out_bf)


def _block_diag_state(c):
    b = c.shape[0]
    eye = jnp.eye(H_B, dtype=c.dtype)
    return jnp.einsum('bhde,hg->bhdge', c, eye).reshape(b, W_B, W_B)


def _diag_blocks(c_bd):
    b = c_bd.shape[0]
    c5 = c_bd.reshape(b, H_B, DH_B, H_B, DH_B)
    return jnp.stack([c5[:, h, :, h, :] for h in range(H_B)], axis=1)


def _mixer_layer(x, past_kv, mlstm_state, mem_k, mem_v, lam_init, p):
    b, s, _ = x.shape
    n = b * s
    tm = min(256, n)
    qaug, k_new, v_new, ga, mb, gl, og, qmg = _in_proj(
        x.reshape(n, D_MODEL), tm, p["g_norm"], p["w_perm"], p["g_mat"],
        p["g_qa"], p["g_ka"], p["g_qm"], p["b_if"])
    k_new = k_new.reshape(b, s, W_A)
    v_new = v_new.reshape(b, s, W_A)
    qaug = qaug.reshape(b, s, 2 * W_A)

    if past_kv is None:
        attn = _prompt_attn(qaug, k_new, v_new, p["lam_vecs"], p["g_subln"], lam_init, tile=256)
    else:
        attn = _sample_attn(qaug, past_kv[0], past_kv[1], k_new, v_new,
                            p["lam_vecs"], p["g_subln"], lam_init)

    c0, n0, m0 = mlstm_state
    s_pad = -(-s // CHUNK) * CHUNK
    mb3 = mb.reshape(b, s, 3 * W_B)
    gl3 = gl.reshape(b, s, LANES)
    if s_pad != s:
        mb3 = jnp.pad(mb3, ((0, 0), (0, s_pad - s), (0, 0)))
        gl3 = jnp.pad(gl3, ((0, 0), (0, s_pad - s), (0, 0)))
    valid = CHUNK if s_pad == s else s
    hb, c_bd, n_row, m_row = _mlstm(
        mb3, gl3, _block_diag_state(c0), n0.reshape(b, 1, W_B),
        jnp.repeat(m0, DH_B, axis=-1).reshape(b, 1, W_B), valid, n_seq=4)
    hb = hb[:, :s]
    new_state = (_diag_blocks(c_bd), n_row.reshape(b, H_B, DH_B),
                 m_row.reshape(b, H_B, DH_B)[:, :, 0])

    y = _mix_out(x, attn, ga.reshape(b, s, W_A), hb, og.reshape(b, s, 2 * W_B),
                 qmg.reshape(b, s, 2 * W_M), mem_k, mem_v, p["g_mh"], p["g_mat"],
                 p["w_out"], tm=min(256, s))
    return (y, k_new.reshape(b, s, H_A, DV_A), v_new.reshape(b, s, H_A, DV_A), new_state)


def kernel(x_prompt, x_sample, cache_attn_k, cache_attn_v, state_mlstm_C, state_mlstm_n,
           state_mlstm_m, cache_mem_k, cache_mem_v, mem_prompt, g_norm, w_in, w_out, g_qa,
           g_ka, lam_q1, lam_k1, lam_q2, lam_k2, g_subln, b_i, b_f, g_mh, g_qm, g_km, g_mem,
           w_mk, w_mv):
    depth = w_in.shape[0]
    bp = x_prompt.shape[0]
    bs = x_sample.shape[0]
    past = cache_attn_k.shape[2]
    gi = lax.broadcasted_iota(jnp.int32, (MXU_DIM, MXU_DIM), 0) // DH_B
    gj = lax.broadcasted_iota(jnp.int32, (MXU_DIM, MXU_DIM), 1) // DH_B
    g_mat = jnp.where(gi == gj, 1.0 / DH_B, 0.0).astype(BF16)

    xp, xs = x_prompt, x_sample
    outs = {name: [] for name in ("pk", "pv", "pC", "pn", "pm", "pmk", "pmv",
                                  "sk", "sv", "sC", "sn", "sm")}
    for l in range(depth):
        lam_init = 0.8 - 0.6 * math.exp(-0.3 * l)
        w = w_in[l]
        w_perm = jnp.concatenate(
            [w[:, :4 * W_A + 4 * W_B], w[:, 4 * W_A + 4 * W_B + 2 * H_B:],
             w[:, 4 * W_A + 4 * W_B:4 * W_A + 4 * W_B + 2 * H_B],
             jnp.zeros((D_MODEL, LANES - 2 * H_B), F32)], axis=1).astype(BF16)
        p = {
            "g_norm": g_norm[l].reshape(1, D_MODEL),
            "w_perm": w_perm,
            "g_mat": g_mat,
            "g_qa": jnp.tile(g_qa[l], 2 * H_A).reshape(1, W_A),
            "g_ka": jnp.tile(g_ka[l], 2 * H_A).reshape(1, W_A),
            "g_qm": jnp.tile(g_qm[l], H_M).reshape(1, W_M),
            "b_if": jnp.concatenate(
                [b_i[l], b_f[l], jnp.zeros((LANES - 2 * H_B,), F32)]).reshape(1, LANES),
            "lam_vecs": tuple(v[l].reshape(1, DK_A) for v in (lam_q1, lam_k1, lam_q2, lam_k2)),
            "g_subln": g_subln[l].reshape(1, DV_A),
            "g_mh": jnp.tile(g_mh[l], H_B).reshape(1, W_B),
            "w_out": w_out[l].astype(BF16),
        }
        mk, mv = _memory_kv(mem_prompt, g_mem[l], w_mk[l], w_mv[l], g_km[l], g_mat)
        zero_state = (jnp.zeros((bp, H_B, DH_B, DH_B), F32), jnp.zeros((bp, H_B, DH_B), F32),
                      jnp.zeros((bp, H_B), F32))
        xp, k_p, v_p, st_p = _mixer_layer(xp, None, zero_state, mk, mv, lam_init, p)
        xs, k_s, v_s, st_s = _mixer_layer(
            xs, (cache_attn_k[l].reshape(bs, past, W_A), cache_attn_v[l].reshape(bs, past, W_A)),
            (state_mlstm_C[l], state_mlstm_n[l], state_mlstm_m[l]),
            cache_mem_k[l].reshape(bs, N_MEM, W_M), cache_mem_v[l].reshape(bs, N_MEM, W_M),
            lam_init, p)
        outs["pk"].append(k_p); outs["pv"].append(v_p)
        outs["pC"].append(st_p[0]); outs["pn"].append(st_p[1]); outs["pm"].append(st_p[2])
        outs["pmk"].append(mk.reshape(bp, N_MEM, H_M, DH_M))
        outs["pmv"].append(mv.reshape(bp, N_MEM, H_M, DH_M))
        outs["sk"].append(k_s); outs["sv"].append(v_s)
        outs["sC"].append(st_s[0]); outs["sn"].append(st_s[1]); outs["sm"].append(st_s[2])
    stk = {name: jnp.stack(vals) for name, vals in outs.items()}
    return (xp, xs, stk["pk"], stk["pv"], stk["pC"], stk["pn"], stk["pm"], stk["pmk"],
            stk["pmv"], stk["sk"], stk["sv"], stk["sC"], stk["sn"], stk["sm"])
```

```python
import functools
import math

import jax
import jax.numpy as jnp
import numpy as np
from jax import lax
from jax.experimental import pallas as pl
from jax.experimental.pallas import tpu as pltpu

F32 = jnp.float32
BF16 = jnp.bfloat16

D_MODEL = 1024
CHUNK = 64
N_MEM = 256
H_A, DK_A = 4, 64
DV_A = 2 * DK_A
W_A = H_A * DV_A
H_B, DH_B = 4, 64
W_B = H_B * DH_B
H_M, DH_M = 4, 64
W_M = H_M * DH_M
D_MIX = W_A + W_B + W_M
EPS = 1e-6
ALIBI_SLOPES = tuple(2.0 ** (-8.0 * (h + 1) / H_A) for h in range(H_A))
N_IN = 4 * W_A + 5 * W_B + 2 * H_B + 2 * W_M

LANES = 128
MXU_DIM = 256
NEG_BIG = -1e30
LOG2E = math.log2(math.e)


def _bf16_pieces(value, n=3):
    pieces, rest = [], np.float32(value)
    for _ in range(n):
        piece = np.float32(rest.astype(jnp.bfloat16))
        pieces.append(float(piece))
        rest = np.float32(rest - piece)
    return tuple(pieces)


LOG2E_BF16_PIECES = _bf16_pieces(LOG2E)

C_QA, C_KA, C_VA, C_GA = 0, W_A, 2 * W_A, 3 * W_A
C_QB = 4 * W_A
C_KB, C_VB, C_OB, C_GB = C_QB + W_B, C_QB + 2 * W_B, C_QB + 3 * W_B, C_QB + 4 * W_B
C_QM = C_GB + W_B
C_GM = C_QM + W_M
C_IF = C_GM + W_M
N_IN_PAD = C_IF + LANES

VMEM_LIMIT = 56 * 1024 * 1024


def _dot(a, b):
    return jnp.dot(a, b, preferred_element_type=F32)


def _dot_nt(a, b):
    return lax.dot_general(a, b, (((1,), (1,)), ((), ())), preferred_element_type=F32)


def _split3(x):
    hi = x.astype(BF16)
    r1 = x - hi.astype(F32)
    mid = r1.astype(BF16)
    lo = (r1 - mid.astype(F32)).astype(BF16)
    return hi, mid, lo


def _group_mean_sq(z, g_mat):
    zz = (z * z).astype(BF16)
    parts = [_dot(zz[:, c:c + MXU_DIM], g_mat) for c in range(0, z.shape[1], MXU_DIM)]
    return parts[0] if len(parts) == 1 else jnp.concatenate(parts, axis=1)


def _group_rms_norm(z, gain_row, g_mat):
    return z * lax.rsqrt(_group_mean_sq(z, g_mat) + EPS) * gain_row


def _log_sigmoid(u):
    return -(jnp.maximum(-u, 0.0) + jnp.log1p(jnp.exp(-jnp.abs(u))))


def _silu(u):
    return u * jax.nn.sigmoid(u)


def _memory_kv_kernel(mem_ref, gmem_ref, wk_ref, wv_ref, gkm_ref, gmat_ref,
                      mk_ref, mkt_ref, mvt_ref, mv_scr):
    x = mem_ref[...]
    ms = jnp.mean(x * x, axis=-1, keepdims=True)
    hm = (x * lax.rsqrt(ms + EPS) * gmem_ref[...]).astype(BF16)
    mk_ref[...] = _group_rms_norm(_dot(hm, wk_ref[...]), gkm_ref[...], gmat_ref[...])
    mv_scr[...] = _dot(hm, wv_ref[...])
    mkt_ref[...] = mk_ref[...].T
    mvt_ref[...] = mv_scr[...].T


def _memory_kv(mem, g_mem, w_mk, w_mv, g_km, g_mat):
    b, n, d = mem.shape
    row = lambda i: (i, 0, 0)
    const2 = lambda i: (0, 0)
    return pl.pallas_call(
        _memory_kv_kernel,
        grid=(b,),
        in_specs=[
            pl.BlockSpec((None, n, d), row),
            pl.BlockSpec((1, d), const2),
            pl.BlockSpec((d, W_M), const2),
            pl.BlockSpec((d, W_M), const2),
            pl.BlockSpec((1, W_M), const2),
            pl.BlockSpec((MXU_DIM, MXU_DIM), const2),
        ],
        out_specs=[pl.BlockSpec((None, n, W_M), row), pl.BlockSpec((None, W_M, n), row),
                   pl.BlockSpec((None, W_M, n), row)],
        out_shape=[jax.ShapeDtypeStruct((b, n, W_M), F32),
                   jax.ShapeDtypeStruct((b, W_M, n), F32),
                   jax.ShapeDtypeStruct((b, W_M, n), F32)],
        scratch_shapes=[pltpu.VMEM((n, W_M), F32)],
        compiler_params=pltpu.CompilerParams(
            dimension_semantics=("arbitrary",), vmem_limit_bytes=VMEM_LIMIT),
        name="memory_kv",
    )(mem, g_mem.reshape(1, d), w_mk.astype(BF16), w_mv.astype(BF16),
      jnp.tile(g_km, H_M).reshape(1, W_M), g_mat)


def _in_proj_kernel(x_ref, gn_ref, w_ref, gmat_ref, gqa_ref, gka_ref, gqm_ref, bif_ref,
                    qaug_ref, k_ref, v_ref, ga_ref, mb_ref, gl_ref, og_ref, qmg_ref,
                    k4_ref, v4_ref):
    x = x_ref[...]
    ms = jnp.mean(x * x, axis=-1, keepdims=True)
    h = (x * lax.rsqrt(ms + EPS) * gn_ref[...]).astype(BF16)
    g_mat = gmat_ref[...]

    def proj(start, width):
        return _dot(h, w_ref[:, start:start + width])

    qn = _group_rms_norm(proj(C_QA, W_A), gqa_ref[...], g_mat) * (DK_A ** -0.5 * LOG2E)
    lane = lax.broadcasted_iota(jnp.int32, (x.shape[0], LANES), 1)
    aug = jnp.zeros((x.shape[0], LANES), F32)
    for rep in range(2):
        for piece, val in enumerate(LOG2E_BF16_PIECES):
            aug = jnp.where(lane == DK_A + 3 * rep + piece, val, aug)
    for hd in range(H_A):
        slab = qn[:, hd * DV_A:(hd + 1) * DV_A]
        q1 = jnp.where(lane < DK_A, slab, aug)
        q2 = jnp.where(lane < DK_A, pltpu.roll(slab, DK_A, 1), aug)
        qaug_ref[:, (2 * hd) * LANES:(2 * hd + 1) * LANES] = q1.astype(BF16)
        qaug_ref[:, (2 * hd + 1) * LANES:(2 * hd + 2) * LANES] = q2.astype(BF16)

    k_ref[...] = _group_rms_norm(proj(C_KA, W_A), gka_ref[...], g_mat)
    v_ref[...] = proj(C_VA, W_A)

    def cache_copy(dst_ref, src_ref, hd):
        dst_ref[:, hd, :] = src_ref[:, hd * DV_A:(hd + 1) * DV_A]

    ga_ref[...] = _silu(proj(C_GA, W_A))
    cache_copy(k4_ref, k_ref, 0)
    mb_ref[:, 0:W_B] = proj(C_QB, W_B)
    cache_copy(k4_ref, k_ref, 1)
    mb_ref[:, W_B:2 * W_B] = proj(C_KB, W_B) * (DH_B ** -0.5)
    cache_copy(k4_ref, k_ref, 2)
    mb_ref[:, 2 * W_B:3 * W_B] = proj(C_VB, W_B)
    cache_copy(k4_ref, k_ref, 3)
    og_ref[:, 0:W_B] = jax.nn.sigmoid(proj(C_OB, W_B))
    cache_copy(v4_ref, v_ref, 0)
    og_ref[:, W_B:2 * W_B] = _silu(proj(C_GB, W_B))
    cache_copy(v4_ref, v_ref, 1)
    qmg_ref[:, 0:W_M] = (_group_rms_norm(proj(C_QM, W_M), gqm_ref[...], g_mat)
                         * (DH_M ** -0.5 * LOG2E))
    cache_copy(v4_ref, v_ref, 2)
    qmg_ref[:, W_M:2 * W_M] = _silu(proj(C_GM, W_M))
    cache_copy(v4_ref, v_ref, 3)

    u = proj(C_IF, LANES) + bif_ref[...]
    gl_ref[...] = jnp.where(lane < H_B, u, _log_sigmoid(u))


def _in_proj(x2d, tm, g_norm, w_perm, g_mat, g_qa, g_ka, g_qm, b_if):
    n = x2d.shape[0]
    row = lambda i: (i, 0)
    const = lambda i: (0, 0)
    widths = (2 * W_A, W_A, W_A, W_A, 3 * W_B, LANES, 2 * W_B, 2 * W_M)
    dtypes = (BF16,) + (F32,) * 7
    return pl.pallas_call(
        _in_proj_kernel,
        grid=(n // tm,),
        in_specs=[
            pl.BlockSpec((tm, D_MODEL), row),
            pl.BlockSpec((1, D_MODEL), const),
            pl.BlockSpec((D_MODEL, N_IN_PAD), const),
            pl.BlockSpec((MXU_DIM, MXU_DIM), const),
            pl.BlockSpec((1, W_A), const),
            pl.BlockSpec((1, W_A), const),
            pl.BlockSpec((1, W_M), const),
            pl.BlockSpec((1, LANES), const),
        ],
        out_specs=[pl.BlockSpec((tm, w), row) for w in widths]
        + [pl.BlockSpec((tm, H_A, DV_A), lambda i: (i, 0, 0))] * 2,
        out_shape=[jax.ShapeDtypeStruct((n, w), dt) for w, dt in zip(widths, dtypes)]
        + [jax.ShapeDtypeStruct((n, H_A, DV_A), F32)] * 2,
        compiler_params=pltpu.CompilerParams(
            dimension_semantics=("arbitrary",), vmem_limit_bytes=VMEM_LIMIT),
        name="in_proj",
    )(x2d, g_norm, w_perm, g_mat, g_qa, g_ka, g_qm, b_if)


def _lambda_value(lq1_ref, lk1_ref, lq2_ref, lk2_ref, lam_init):
    s1 = jnp.sum(lq1_ref[...] * lk1_ref[...], axis=-1, keepdims=True)
    s2 = jnp.sum(lq2_ref[...] * lk2_ref[...], axis=-1, keepdims=True)
    return jnp.exp(s1) - jnp.exp(s2) + lam_init


def _subln(o, gsub_row, lam_init):
    ms = jnp.mean(o * o, axis=-1, keepdims=True)
    return o * lax.rsqrt(ms + EPS) * gsub_row * (1.0 - lam_init)


def _prompt_attn_kernel(lq1_ref, lk1_ref, lq2_ref, lk2_ref, gsub_ref, q_ref, k_ref, v_ref,
                        o_ref, kaug_ref, vt_ref, m_ref, l_ref, acc_ref, s_ref, *, tile, lam_init):
    i = pl.program_id(1)
    seq = k_ref.shape[0]

    @pl.when(i == 0)
    def _stage_keys_values():
        lane = lax.broadcasted_iota(jnp.int32, (seq, LANES), 1)
        pos = lax.broadcasted_iota(jnp.int32, (seq, LANES), 0)
        in_tile = (pos % tile).astype(F32)
        tile_base = (pos - pos % tile).astype(F32)
        for hd in range(H_A):
            slab = k_ref[:, hd * DV_A:(hd + 1) * DV_A]
            aug = jnp.where((lane >= DK_A) & (lane < DK_A + 3), ALIBI_SLOPES[hd] * in_tile,
                            jnp.where((lane >= DK_A + 3) & (lane < DK_A + 6),
                                      ALIBI_SLOPES[hd] * tile_base, 0.0))
            kaug_ref[2 * hd] = jnp.where(lane < DK_A, slab, aug).astype(BF16)
            kaug_ref[2 * hd + 1] = jnp.where(
                lane < DK_A, pltpu.roll(slab, DK_A, 1), aug).astype(BF16)
        for t in range(seq // tile):
            vt_ref[t] = v_ref[t * tile:(t + 1) * tile, :].T.astype(BF16)

    n_chain = 2 * H_A

    def scores(c, j):
        start = pl.multiple_of(j * tile, tile)
        s_ref[c] = _dot_nt(kaug_ref[c, pl.ds(start, tile), :],
                           q_ref[:, c * LANES:(c + 1) * LANES])

    def softmax_values(c, j, bias_of_head):
        s = s_ref[c]
        if bias_of_head is not None:
            s = s + bias_of_head[c // 2]
        mx = m_ref[c]
        mx_new = jnp.maximum(mx, jnp.max(s, axis=0, keepdims=True))
        p = jnp.exp2(s - mx_new)
        alpha = jnp.exp2(mx - mx_new)
        m_ref[c] = mx_new
        l_ref[c] = alpha * l_ref[c] + jnp.sum(p, axis=0, keepdims=True)
        v_t = vt_ref[j, (c // 2) * DV_A:(c // 2 + 1) * DV_A, :]
        acc_ref[c] = alpha * acc_ref[c] + _dot(v_t, p.astype(BF16))

    m_ref[...] = jnp.full(m_ref.shape, NEG_BIG, F32)
    l_ref[...] = jnp.zeros(l_ref.shape, F32)
    acc_ref[...] = jnp.zeros(acc_ref.shape, F32)

    for c in range(n_chain):
        scores(c, 0)

    def past_tile(j, carry):
        for c in range(n_chain):
            softmax_values(c, j, None)
            scores(c, j + 1)
        return carry

    lax.fori_loop(0, i, past_tile, 0)

    key = lax.broadcasted_iota(jnp.int32, (tile, tile), 0)
    qry = lax.broadcasted_iota(jnp.int32, (tile, tile), 1)
    ahead = (key - qry).astype(F32)
    diag_bias = [jnp.where(key // CHUNK > qry // CHUNK, NEG_BIG,
                           jnp.where(key > qry, (-2.0 * LOG2E * ALIBI_SLOPES[hd]) * ahead, 0.0))
                 for hd in range(H_A)]
    for c in range(n_chain):
        softmax_values(c, i, diag_bias)

    lam = _lambda_value(lq1_ref, lk1_ref, lq2_ref, lk2_ref, lam_init)
    for hd in range(H_A):
        o1 = acc_ref[2 * hd] * (1.0 / l_ref[2 * hd])
        o2 = acc_ref[2 * hd + 1] * (1.0 / l_ref[2 * hd + 1])
        o = (o1 - lam * o2).T
        o_ref[:, hd * DV_A:(hd + 1) * DV_A] = _subln(o, gsub_ref[...], lam_init)


def _prompt_attn(qaug, k, v, lam_vecs, g_subln, lam_init, tile):
    b, s, _ = k.shape
    const = lambda bi, qi: (0, 0)
    kern = functools.partial(_prompt_attn_kernel, tile=tile, lam_init=lam_init)
    return pl.pallas_call(
        kern,
        grid=(b, s // tile),
        in_specs=[pl.BlockSpec((1, DK_A), const)] * 4 + [
            pl.BlockSpec((1, DV_A), const),
            pl.BlockSpec((None, tile, 2 * W_A), lambda bi, qi: (bi, qi, 0)),
            pl.BlockSpec((None, s, W_A), lambda bi, qi: (bi, 0, 0)),
            pl.BlockSpec((None, s, W_A), lambda bi, qi: (bi, 0, 0)),
        ],
        out_specs=pl.BlockSpec((None, tile, W_A), lambda bi, qi: (bi, qi, 0)),
        out_shape=jax.ShapeDtypeStruct((b, s, W_A), F32),
        scratch_shapes=[
            pltpu.VMEM((2 * H_A, s, LANES), BF16),
            pltpu.VMEM((s // tile, W_A, tile), BF16),
            pltpu.VMEM((2 * H_A, 1, tile), F32),
            pltpu.VMEM((2 * H_A, 1, tile), F32),
            pltpu.VMEM((2 * H_A, DV_A, tile), F32),
            pltpu.VMEM((2 * H_A, tile, tile), F32),
        ],
        compiler_params=pltpu.CompilerParams(
            dimension_semantics=("arbitrary", "arbitrary"), vmem_limit_bytes=VMEM_LIMIT),
        name="diff_attn_prompt",
    )(*lam_vecs, g_subln, qaug, k, v)


def _sample_attn_kernel(lq1_ref, lk1_ref, lq2_ref, lk2_ref, gsub_ref, q_ref, kc_ref, vc_ref,
                        kn_ref, vn_ref, o_ref, *, lam_init):
    sq = q_ref.shape[0]
    past = kc_ref.shape[0]
    lam = _lambda_value(lq1_ref, lk1_ref, lq2_ref, lk2_ref, lam_init)
    lane = lax.broadcasted_iota(jnp.int32, (sq, LANES), 1)
    qpos_c = past + lax.broadcasted_iota(jnp.int32, (sq, past), 0)
    dist_c = jnp.abs(qpos_c - lax.broadcasted_iota(jnp.int32, (sq, past), 1)).astype(F32)
    dist_n = jnp.abs(lax.broadcasted_iota(jnp.int32, (sq, sq), 0)
                     - lax.broadcasted_iota(jnp.int32, (sq, sq), 1)).astype(F32)
    for hd in range(H_A):
        slope = ALIBI_SLOPES[hd]
        kc = kc_ref[:, hd, :].astype(BF16)
        kn = kn_ref[:, hd * DV_A:(hd + 1) * DV_A].astype(BF16)
        vc = vc_ref[:, hd, :].astype(BF16)
        vn = vn_ref[:, hd * DV_A:(hd + 1) * DV_A].astype(BF16)
        outs = []
        for m in range(2):
            q = q_ref[:, (2 * hd + m) * LANES:(2 * hd + m + 1) * LANES].astype(F32)
            if m == 0:
                q = jnp.where(lane < DK_A, q, 0.0)
            else:
                q = jnp.where(lane >= DK_A, pltpu.roll(q, DK_A, 1), 0.0)
            q = q.astype(BF16)
            s_c = _dot_nt(q, kc) - (slope * LOG2E) * dist_c
            s_n = _dot_nt(q, kn) - (slope * LOG2E) * dist_n
            mx = jnp.maximum(jnp.max(s_c, axis=-1, keepdims=True),
                             jnp.max(s_n, axis=-1, keepdims=True))
            p_c = jnp.exp2(s_c - mx)
            p_n = jnp.exp2(s_n - mx)
            l = jnp.sum(p_c, axis=-1, keepdims=True) + jnp.sum(p_n, axis=-1, keepdims=True)
            acc = _dot(p_c.astype(BF16), vc) + _dot(p_n.astype(BF16), vn)
            outs.append(acc * (1.0 / l))
        o = outs[0] - lam * outs[1]
        o_ref[:, hd * DV_A:(hd + 1) * DV_A] = _subln(o, gsub_ref[...], lam_init)


def _sample_attn(qaug, k_cache, v_cache, k_new, v_new, lam_vecs, g_subln, lam_init):
    b, past = k_cache.shape[:2]
    sq = k_new.shape[1]
    per_b4 = lambda bi: (bi, 0, 0, 0)
    assert (past + sq - 1) // CHUNK <= past // CHUNK
    const = lambda bi: (0, 0)
    per_b = lambda bi: (bi, 0, 0)
    kern = functools.partial(_sample_attn_kernel, lam_init=lam_init)
    return pl.pallas_call(
        kern,
        grid=(b,),
        in_specs=[pl.BlockSpec((1, DK_A), const)] * 4 + [
            pl.BlockSpec((1, DV_A), const),
            pl.BlockSpec((None, sq, 2 * W_A), per_b),
            pl.BlockSpec((None, past, H_A, DV_A), per_b4),
            pl.BlockSpec((None, past, H_A, DV_A), per_b4),
            pl.BlockSpec((None, sq, W_A), per_b),
            pl.BlockSpec((None, sq, W_A), per_b),
        ],
        out_specs=pl.BlockSpec((None, sq, W_A), per_b),
        out_shape=jax.ShapeDtypeStruct((b, sq, W_A), F32),
        compiler_params=pltpu.CompilerParams(
            dimension_semantics=("arbitrary",), vmem_limit_bytes=VMEM_LIMIT),
        name="diff_attn_sample",
    )(*lam_vecs, g_subln, qaug, k_cache, v_cache, k_new, v_new)


def _mlstm_kernel(mb_ref, gl_ref, c0_ref, n0_ref, m0_ref,
                  h_ref, c_ref, n_ref, m_ref, *, valid):
    c_idx = pl.program_id(1)
    L = CHUNK
    W = W_B
    n_seq = mb_ref.shape[0]

    @pl.when(c_idx == 0)
    def _load_state():
        c_ref[...] = c0_ref[...]
        n_ref[...] = n0_ref[...]
        m_ref[...] = m0_ref[...]

    lane_head = lax.broadcasted_iota(jnp.int32, (L, W), 1) // DH_B
    row_t = lax.broadcasted_iota(jnp.int32, (L, W), 0)
    lane_s = lax.broadcasted_iota(jnp.int32, (L, W), 1) % DH_B
    wi = lax.broadcasted_iota(jnp.int32, (W, W), 0)
    wj = lax.broadcasted_iota(jnp.int32, (W, W), 1)
    blk = wi // DH_B == wj // DH_B
    ones_bd = blk.astype(BF16)
    ei = lax.broadcasted_iota(jnp.int32, (LANES, 2 * W), 0)
    ej = lax.broadcasted_iota(jnp.int32, (LANES, 2 * W), 1)
    expand = (ei == ej // DH_B).astype(BF16)
    tri = (lax.broadcasted_iota(jnp.int32, (L, L), 1)
           <= lax.broadcasted_iota(jnp.int32, (L, L), 0)).astype(BF16)
    nar_lane = lax.broadcasted_iota(jnp.int32, (L, LANES), 1)
    nar_t = lax.broadcasted_iota(jnp.int32, (L, LANES), 0)

    def dot3(x, mat, left=False):
        return sum((_dot(mat, part) if left else _dot(part, mat)) for part in _split3(x))

    def head_max(x):
        out = jnp.zeros((L, W), F32)
        for hd in range(H_B):
            mx = jnp.max(jnp.where(lane_head == hd, x, NEG_BIG), axis=-1, keepdims=True)
            out = jnp.where(lane_head == hd, mx, out)
        return out

    def chunk_stages(g):
        q = mb_ref[g, :, 0:W]
        k = mb_ref[g, :, W:2 * W]
        v = mb_ref[g, :, 2 * W:3 * W]
        gl = jnp.where(nar_lane < 2 * H_B, gl_ref[g], 0.0)
        gl = jnp.where(nar_t < valid, gl, jnp.where(nar_lane < H_B, NEG_BIG, 0.0))
        narrow = jnp.where(nar_lane < H_B, gl, dot3(gl, tri, left=True))
        wide = dot3(narrow, expand)
        ig_all = wide[:, 0:W]
        b_all = wide[:, W:2 * W]
        yield

        r_all = b_all - ig_all
        r_row = jnp.sum(jnp.where(lane_s == row_t, r_all, 0.0), axis=0, keepdims=True)
        m0_row = m_ref[g]
        c0 = c_ref[g]
        n0_row = n_ref[g]
        d_mat = jnp.where(lane_s <= row_t, b_all - r_row, NEG_BIG)
        g_all = b_all + m0_row
        m_all = jnp.maximum(g_all, head_max(d_mat))
        d_w = jnp.exp(d_mat - m_all)
        g_w = jnp.exp(g_all - m_all)
        yield

        q_bf = q.astype(BF16)
        k_bd = jnp.where(blk, jnp.concatenate([k.T] * H_B, axis=1), 0.0).astype(BF16)
        v_bd = jnp.where(blk, jnp.concatenate([v] * H_B, axis=0), 0.0).astype(BF16)
        a_mat = _dot(q_bf, k_bd) * d_w
        num = g_w * _dot(q_bf, c0.astype(BF16)) + _dot(a_mat.astype(BF16), v_bd)
        yield

        den = g_w * dot3(q * n0_row, ones_bd) + dot3(a_mat, ones_bd)
        h_ref[g] = num / jnp.maximum(jnp.abs(den), jnp.exp(-m_all))
        yield

        m_last = m_all[L - 1:L, :]
        b_last = b_all[L - 1:L, :]
        w_s = jnp.exp(b_last - b_all + ig_all - m_last)
        decay = jnp.exp(b_last + m0_row - m_last)
        kw = k * w_s
        c_ref[g] = decay * c0 + jnp.where(blk, _dot(kw.T.astype(BF16), v.astype(BF16)), 0.0)
        n_ref[g] = decay * n0_row + jnp.sum(kw, axis=0, keepdims=True)
        m_ref[g] = m_last
        yield

    stages = [chunk_stages(g) for g in range(n_seq)]
    for _ in range(5):
        for stage in stages:
            next(stage)


def _mlstm(mb, gl, c0_bd, n0_row, m0_row, valid, n_seq):
    b, s, _ = mb.shape
    nc = s // CHUNK
    tok = lambda bi, ci: (bi, ci, 0)
    st = lambda bi, ci: (bi, 0, 0)
    kern = functools.partial(_mlstm_kernel, valid=valid)
    return pl.pallas_call(
        kern,
        grid=(b // n_seq, nc),
        in_specs=[
            pl.BlockSpec((n_seq, CHUNK, 3 * W_B), tok),
            pl.BlockSpec((n_seq, CHUNK, LANES), tok),
            pl.BlockSpec((n_seq, W_B, W_B), st),
            pl.BlockSpec((n_seq, 1, W_B), st),
            pl.BlockSpec((n_seq, 1, W_B), st),
        ],
        out_specs=[
            pl.BlockSpec((n_seq, CHUNK, W_B), tok),
            pl.BlockSpec((n_seq, W_B, W_B), st),
            pl.BlockSpec((n_seq, 1, W_B), st),
            pl.BlockSpec((n_seq, 1, W_B), st),
        ],
        out_shape=[
            jax.ShapeDtypeStruct((b, s, W_B), F32),
            jax.ShapeDtypeStruct((b, W_B, W_B), F32),
            jax.ShapeDtypeStruct((b, 1, W_B), F32),
            jax.ShapeDtypeStruct((b, 1, W_B), F32),
        ],
        compiler_params=pltpu.CompilerParams(
            dimension_semantics=("arbitrary", "arbitrary"), vmem_limit_bytes=VMEM_LIMIT),
        name="mlstm",
    )(mb, gl, c0_bd, n0_row, m0_row)


def _mix_out_kernel(x_ref, attn_ref, ga_ref, hb_ref, og_ref, qmg_ref, mk_ref, mvt_ref,
                    gmh_ref, gmat_ref, wo_ref, y_ref, *, sub_rows):
    tm = x_ref.shape[0]
    sub = min(tm, sub_rows)
    mk = mk_ref[...]
    mvt = mvt_ref[...]
    k_feat_head = lax.broadcasted_iota(jnp.int32, mk.shape, 1) // DH_M
    v_feat_head = lax.broadcasted_iota(jnp.int32, mvt.shape, 0) // DH_M
    k_of_head = [jnp.where(k_feat_head == hd, mk, 0.0).astype(BF16) for hd in range(H_M)]
    vt_of_head = [jnp.where(v_feat_head == hd, mvt, 0.0).astype(BF16) for hd in range(H_M)]

    def probs(s):
        p = jnp.exp2(s - jnp.max(s, axis=0, keepdims=True))
        return (p * (1.0 / jnp.sum(p, axis=0, keepdims=True))).astype(BF16)

    def sub_tile_stages(r0):
        rows = slice(r0, r0 + sub)
        qm = qmg_ref[rows, 0:W_M]
        if sub < LANES:
            qm = jnp.concatenate([qm, jnp.zeros((LANES - sub, W_M), F32)], axis=0)
        qm = qm.astype(BF16)
        s0 = _dot_nt(k_of_head[0], qm)
        oa = attn_ref[rows, :] * ga_ref[rows, :]
        y = x_ref[rows, :] + _dot(oa.astype(BF16), wo_ref[0:W_A, :])
        yield
        s1 = _dot_nt(k_of_head[1], qm)
        p0 = probs(s0)
        hb = _group_rms_norm(hb_ref[rows, :], gmh_ref[...], gmat_ref[...])
        hb = hb * og_ref[rows, 0:W_B] * og_ref[rows, W_B:2 * W_B]
        y = y + _dot(hb.astype(BF16), wo_ref[W_A:W_A + W_B, :])
        yield
        s2 = _dot_nt(k_of_head[2], qm)
        p1 = probs(s1)
        omt = _dot(vt_of_head[0], p0)
        yield
        s3 = _dot_nt(k_of_head[3], qm)
        p2 = probs(s2)
        omt = omt + _dot(vt_of_head[1], p1)
        yield
        p3 = probs(s3)
        omt = omt + _dot(vt_of_head[2], p2)
        omt = omt + _dot(vt_of_head[3], p3)
        yield
        om = omt.T[0:sub] * qmg_ref[rows, W_M:2 * W_M]
        y_ref[rows, :] = y + _dot(om.astype(BF16), wo_ref[W_A + W_B:D_MIX, :])
        yield

    n_stage = 6
    gens = [sub_tile_stages(r0) for r0 in range(0, tm, sub)]
    for step in range(n_stage + len(gens) - 1):
        for idx, gen in enumerate(gens):
            if 0 <= step - idx < n_stage:
                next(gen)


def _mix_out(x, attn, ga, hb, og, qmg, mem_k, mem_vt, g_mh_row, g_mat, w_out_bf, tm):
    b, s, _ = x.shape
    tok = lambda bi, ti: (bi, ti, 0)
    per_b = lambda bi, ti: (bi, 0, 0)
    const = lambda bi, ti: (0, 0)
    return pl.pallas_call(
        functools.partial(_mix_out_kernel, sub_rows=MXU_DIM),
        grid=(b, s // tm),
        in_specs=[
            pl.BlockSpec((None, tm, D_MODEL), tok),
            pl.BlockSpec((None, tm, W_A), tok),
            pl.BlockSpec((None, tm, W_A), tok),
            pl.BlockSpec((None, tm, W_B), tok),
            pl.BlockSpec((None, tm, 2 * W_B), tok),
            pl.BlockSpec((None, tm, 2 * W_M), tok),
            pl.BlockSpec((None, N_MEM, W_M), per_b),
            pl.BlockSpec((None, N_MEM, W_M), per_b),
            pl.BlockSpec((1, W_B), const),
            pl.BlockSpec((MXU_DIM, MXU_DIM), const),
            pl.BlockSpec((D_MIX, D_MODEL), const),
        ],
        out_specs=pl.BlockSpec((None, tm, D_MODEL), tok),
        out_shape=jax.ShapeDtypeStruct((b, s, D_MODEL), F32),
        compiler_params=pltpu.CompilerParams(
            dimension_semantics=("arbitrary", "arbitrary"), vmem_limit_bytes=VMEM_LIMIT),
        name="mix_out",
    )(x, attn, ga, hb, og, qmg, mem_k, mem_vt, g_mh_row, g_mat, w_out_bf)


def _block_diag_state(c):
    b = c.shape[0]
    eye = jnp.eye(H_B, dtype=c.dtype)
    return jnp.einsum('bhde,hg->bhdge', c, eye).reshape(b, W_B, W_B)


def _diag_blocks(c_bd):
    b = c_bd.shape[0]
    c5 = c_bd.reshape(b, H_B, DH_B, H_B, DH_B)
    return jnp.stack([c5[:, h, :, h, :] for h in range(H_B)], axis=1)


def _mixer_layer(x, past_kv, mlstm_state, mem_k, mem_vt, lam_init, p):
    b, s, _ = x.shape
    n = b * s
    tm = min(512, n)
    qaug, k_new, v_new, ga, mb, gl, og, qmg, k4, v4 = _in_proj(
        x.reshape(n, D_MODEL), tm, p["g_norm"], p["w_perm"], p["g_mat"],
        p["g_qa"], p["g_ka"], p["g_qm"], p["b_if"])
    k_new = k_new.reshape(b, s, W_A)
    v_new = v_new.reshape(b, s, W_A)
    qaug = qaug.reshape(b, s, 2 * W_A)

    if past_kv is None:
        attn = _prompt_attn(qaug, k_new, v_new, p["lam_vecs"], p["g_subln"], lam_init, tile=256)
    else:
        attn = _sample_attn(qaug, past_kv[0], past_kv[1], k_new, v_new,
                            p["lam_vecs"], p["g_subln"], lam_init)

    c0, n0, m0 = mlstm_state
    s_pad = -(-s // CHUNK) * CHUNK
    mb3 = mb.reshape(b, s, 3 * W_B)
    gl3 = gl.reshape(b, s, LANES)
    if s_pad != s:
        mb3 = jnp.pad(mb3, ((0, 0), (0, s_pad - s), (0, 0)))
        gl3 = jnp.pad(gl3, ((0, 0), (0, s_pad - s), (0, 0)))
    valid = CHUNK if s_pad == s else s
    hb, c_bd, n_row, m_row = _mlstm(
        mb3, gl3, _block_diag_state(c0), n0.reshape(b, 1, W_B),
        jnp.repeat(m0, DH_B, axis=-1).reshape(b, 1, W_B), valid, n_seq=4)
    hb = hb[:, :s]
    new_state = (_diag_blocks(c_bd), n_row.reshape(b, H_B, DH_B),
                 m_row.reshape(b, H_B, DH_B)[:, :, 0])

    y = _mix_out(x, attn, ga.reshape(b, s, W_A), hb, og.reshape(b, s, 2 * W_B),
                 qmg.reshape(b, s, 2 * W_M), mem_k, mem_vt, p["g_mh"], p["g_mat"],
                 p["w_out"], tm=min(1024, s))
    return (y, k4.reshape(b, s, H_A, DV_A), v4.reshape(b, s, H_A, DV_A), new_state)


def kernel(x_prompt, x_sample, cache_attn_k, cache_attn_v, state_mlstm_C, state_mlstm_n,
           state_mlstm_m, cache_mem_k, cache_mem_v, mem_prompt, g_norm, w_in, w_out, g_qa,
           g_ka, lam_q1, lam_k1, lam_q2, lam_k2, g_subln, b_i, b_f, g_mh, g_qm, g_km, g_mem,
           w_mk, w_mv):
    depth = w_in.shape[0]
    bp = x_prompt.shape[0]
    bs = x_sample.shape[0]
    past = cache_attn_k.shape[2]
    gi = lax.broadcasted_iota(jnp.int32, (MXU_DIM, MXU_DIM), 0) // DH_B
    gj = lax.broadcasted_iota(jnp.int32, (MXU_DIM, MXU_DIM), 1) // DH_B
    g_mat = jnp.where(gi == gj, 1.0 / DH_B, 0.0).astype(BF16)

    xp, xs = x_prompt, x_sample
    outs = {name: [] for name in ("pk", "pv", "pC", "pn", "pm", "pmk", "pmv",
                                  "sk", "sv", "sC", "sn", "sm")}
    for l in range(depth):
        lam_init = 0.8 - 0.6 * math.exp(-0.3 * l)
        w = w_in[l]
        w_perm = jnp.concatenate(
            [w[:, :4 * W_A + 4 * W_B], w[:, 4 * W_A + 4 * W_B + 2 * H_B:],
             w[:, 4 * W_A + 4 * W_B:4 * W_A + 4 * W_B + 2 * H_B],
             jnp.zeros((D_MODEL, LANES - 2 * H_B), F32)], axis=1).astype(BF16)
        p = {
            "g_norm": g_norm[l].reshape(1, D_MODEL),
            "w_perm": w_perm,
            "g_mat": g_mat,
            "g_qa": jnp.tile(g_qa[l], 2 * H_A).reshape(1, W_A),
            "g_ka": jnp.tile(g_ka[l], 2 * H_A).reshape(1, W_A),
            "g_qm": jnp.tile(g_qm[l], H_M).reshape(1, W_M),
            "b_if": jnp.concatenate(
                [b_i[l], b_f[l], jnp.zeros((LANES - 2 * H_B,), F32)]).reshape(1, LANES),
            "lam_vecs": tuple(v[l].reshape(1, DK_A) for v in (lam_q1, lam_k1, lam_q2, lam_k2)),
            "g_subln": g_subln[l].reshape(1, DV_A),
            "g_mh": jnp.tile(g_mh[l], H_B).reshape(1, W_B),
            "w_out": w_out[l].astype(BF16),
        }
        mk, mkt, mvt = _memory_kv(mem_prompt, g_mem[l], w_mk[l], w_mv[l], g_km[l], g_mat)
        zero_state = (jnp.zeros((bp, H_B, DH_B, DH_B), F32), jnp.zeros((bp, H_B, DH_B), F32),
                      jnp.zeros((bp, H_B), F32))
        xp, k_p, v_p, st_p = _mixer_layer(xp, None, zero_state, mk, mvt, lam_init, p)
        xs, k_s, v_s, st_s = _mixer_layer(
            xs, (cache_attn_k[l], cache_attn_v[l]),
            (state_mlstm_C[l], state_mlstm_n[l], state_mlstm_m[l]),
            cache_mem_k[l].reshape(bs, N_MEM, W_M),
            cache_mem_v[l].transpose(0, 2, 3, 1).reshape(bs, W_M, N_MEM),
            lam_init, p)

        def tokens_first(t):
            return t.reshape(bp, H_M, DH_M, N_MEM).transpose(0, 3, 1, 2)

        outs["pk"].append(k_p); outs["pv"].append(v_p)
        outs["pC"].append(st_p[0]); outs["pn"].append(st_p[1]); outs["pm"].append(st_p[2])
        outs["pmk"].append(tokens_first(mkt))
        outs["pmv"].append(tokens_first(mvt))
        outs["sk"].append(k_s); outs["sv"].append(v_s)
        outs["sC"].append(st_s[0]); outs["sn"].append(st_s[1]); outs["sm"].append(st_s[2])
    stk = {name: jnp.stack(vals) for name, vals in outs.items()}
    return (xp, xs, stk["pk"], stk["pv"], stk["pC"], stk["pn"], stk["pm"], stk["pmk"],
            stk["pmv"], stk["sk"], stk["sv"], stk["sC"], stk["sn"], stk["sm"])
```

```python
import functools
import math

import jax
import jax.numpy as jnp
import numpy as np
from jax import lax
from jax.experimental import pallas as pl
from jax.experimental.pallas import tpu as pltpu

F32 = jnp.float32
BF16 = jnp.bfloat16

D_MODEL = 1024
CHUNK = 64
N_MEM = 256
H_A, DK_A = 4, 64
DV_A = 2 * DK_A
W_A = H_A * DV_A
H_B, DH_B = 4, 64
W_B = H_B * DH_B
H_M, DH_M = 4, 64
W_M = H_M * DH_M
D_MIX = W_A + W_B + W_M
EPS = 1e-6
ALIBI_SLOPES = tuple(2.0 ** (-8.0 * (h + 1) / H_A) for h in range(H_A))
N_IN = 4 * W_A + 5 * W_B + 2 * H_B + 2 * W_M

LANES = 128
MXU_DIM = 256
BF16_SUBLANES = 16
VT_ROWS = DV_A + BF16_SUBLANES
NEG_BIG = -1e30
LOG2E = math.log2(math.e)


def _bf16_pieces(value, n=3):
    pieces, rest = [], np.float32(value)
    for _ in range(n):
        piece = np.float32(rest.astype(jnp.bfloat16))
        pieces.append(float(piece))
        rest = np.float32(rest - piece)
    return tuple(pieces)


LOG2E_BF16_PIECES = _bf16_pieces(LOG2E)

C_QA, C_KA, C_VA, C_GA = 0, W_A, 2 * W_A, 3 * W_A
C_QB = 4 * W_A
C_KB, C_VB, C_OB, C_GB = C_QB + W_B, C_QB + 2 * W_B, C_QB + 3 * W_B, C_QB + 4 * W_B
C_QM = C_GB + W_B
C_GM = C_QM + W_M
C_IF = C_GM + W_M
N_IN_PAD = C_IF + LANES

VMEM_LIMIT = 56 * 1024 * 1024


def _dot(a, b):
    return jnp.dot(a, b, preferred_element_type=F32)


def _dot_nt(a, b):
    return lax.dot_general(a, b, (((1,), (1,)), ((), ())), preferred_element_type=F32)


def _split3(x):
    hi = x.astype(BF16)
    r1 = x - hi.astype(F32)
    mid = r1.astype(BF16)
    lo = (r1 - mid.astype(F32)).astype(BF16)
    return hi, mid, lo


def _group_mean_sq(z, g_mat):
    zz = (z * z).astype(BF16)
    parts = [_dot(zz[:, c:c + MXU_DIM], g_mat) for c in range(0, z.shape[1], MXU_DIM)]
    return parts[0] if len(parts) == 1 else jnp.concatenate(parts, axis=1)


def _group_rms_norm(z, gain_row, g_mat):
    return z * lax.rsqrt(_group_mean_sq(z, g_mat) + EPS) * gain_row


def _log_sigmoid(u):
    return -(jnp.maximum(-u, 0.0) + jnp.log1p(jnp.exp(-jnp.abs(u))))


def _silu(u):
    return u * jax.nn.sigmoid(u)


def _memory_kv_kernel(mem_ref, gmem_ref, wk_ref, wv_ref, gkm_ref, gmat_ref,
                      mk_ref, mkt_ref, mvt_ref, mv_scr):
    x = mem_ref[...]
    ms = jnp.mean(x * x, axis=-1, keepdims=True)
    hm = (x * lax.rsqrt(ms + EPS) * gmem_ref[...]).astype(BF16)
    mk_ref[...] = _group_rms_norm(_dot(hm, wk_ref[...]), gkm_ref[...], gmat_ref[...])
    mv_scr[...] = _dot(hm, wv_ref[...])
    mkt_ref[...] = mk_ref[...].T
    mvt_ref[...] = mv_scr[...].T


def _memory_kv(mem, g_mem, w_mk, w_mv, g_km, g_mat):
    b, n, d = mem.shape
    row = lambda i: (i, 0, 0)
    const2 = lambda i: (0, 0)
    return pl.pallas_call(
        _memory_kv_kernel,
        grid=(b,),
        in_specs=[
            pl.BlockSpec((None, n, d), row),
            pl.BlockSpec((1, d), const2),
            pl.BlockSpec((d, W_M), const2),
            pl.BlockSpec((d, W_M), const2),
            pl.BlockSpec((1, W_M), const2),
            pl.BlockSpec((MXU_DIM, MXU_DIM), const2),
        ],
        out_specs=[pl.BlockSpec((None, n, W_M), row), pl.BlockSpec((None, W_M, n), row),
                   pl.BlockSpec((None, W_M, n), row)],
        out_shape=[jax.ShapeDtypeStruct((b, n, W_M), F32),
                   jax.ShapeDtypeStruct((b, W_M, n), F32),
                   jax.ShapeDtypeStruct((b, W_M, n), F32)],
        scratch_shapes=[pltpu.VMEM((n, W_M), F32)],
        compiler_params=pltpu.CompilerParams(
            dimension_semantics=("arbitrary",), vmem_limit_bytes=VMEM_LIMIT),
        name="memory_kv",
    )(mem, g_mem.reshape(1, d), w_mk.astype(BF16), w_mv.astype(BF16),
      jnp.tile(g_km, H_M).reshape(1, W_M), g_mat)


def _in_proj_kernel(x_ref, gn_ref, w_ref, gmat_ref, gqa_ref, gka_ref, gqm_ref, bif_ref,
                    qaug_ref, k_ref, v_ref, ga_ref, mb_ref, gl_ref, og_ref, qm_ref, gm_ref,
                    k4_ref, v4_ref):
    x = x_ref[...]
    h = (x * gn_ref[...]).astype(BF16)
    inv_rms = lax.rsqrt(jnp.mean(x * x, axis=-1, keepdims=True) + EPS)
    g_mat = gmat_ref[...]

    def proj(start, width):
        return _dot(h, w_ref[:, start:start + width]) * inv_rms

    qn = _group_rms_norm(proj(C_QA, W_A), gqa_ref[...], g_mat) * (DK_A ** -0.5 * LOG2E)
    lane = lax.broadcasted_iota(jnp.int32, (x.shape[0], LANES), 1)
    aug = jnp.zeros((x.shape[0], LANES), F32)
    for rep in range(2):
        for piece, val in enumerate(LOG2E_BF16_PIECES):
            aug = jnp.where(lane == DK_A + 3 * rep + piece, val, aug)
    for hd in range(H_A):
        slab = qn[:, hd * DV_A:(hd + 1) * DV_A]
        q1 = jnp.where(lane < DK_A, slab, aug)
        q2 = jnp.where(lane < DK_A, pltpu.roll(slab, DK_A, 1), aug)
        qaug_ref[:, (2 * hd) * LANES:(2 * hd + 1) * LANES] = q1.astype(BF16)
        qaug_ref[:, (2 * hd + 1) * LANES:(2 * hd + 2) * LANES] = q2.astype(BF16)

    k_ref[...] = _group_rms_norm(proj(C_KA, W_A), gka_ref[...], g_mat)
    v_ref[...] = proj(C_VA, W_A)

    def cache_copy(dst_ref, src_ref, hd):
        dst_ref[:, hd, :] = src_ref[:, hd * DV_A:(hd + 1) * DV_A]

    ga_ref[...] = _silu(proj(C_GA, W_A))
    cache_copy(k4_ref, k_ref, 0)
    mb_ref[:, 0:W_B] = proj(C_QB, W_B)
    cache_copy(k4_ref, k_ref, 1)
    mb_ref[:, W_B:2 * W_B] = proj(C_KB, W_B) * (DH_B ** -0.5)
    cache_copy(k4_ref, k_ref, 2)
    mb_ref[:, 2 * W_B:3 * W_B] = proj(C_VB, W_B)
    cache_copy(k4_ref, k_ref, 3)
    og_ref[:, 0:W_B] = jax.nn.sigmoid(proj(C_OB, W_B))
    cache_copy(v4_ref, v_ref, 0)
    og_ref[:, W_B:2 * W_B] = _silu(proj(C_GB, W_B))
    cache_copy(v4_ref, v_ref, 1)
    qm_ref[...] = (_group_rms_norm(proj(C_QM, W_M), gqm_ref[...], g_mat)
                   * (DH_M ** -0.5 * LOG2E)).astype(BF16)
    cache_copy(v4_ref, v_ref, 2)
    gm_ref[...] = _silu(proj(C_GM, W_M))
    cache_copy(v4_ref, v_ref, 3)

    u = proj(C_IF, LANES) + bif_ref[...]
    gl_ref[...] = jnp.where(lane < H_B, u, _log_sigmoid(u))


def _in_proj(x2d, tm, g_norm, w_perm, g_mat, g_qa, g_ka, g_qm, b_if):
    n = x2d.shape[0]
    row = lambda i: (i, 0)
    const = lambda i: (0, 0)
    widths = (2 * W_A, W_A, W_A, W_A, 3 * W_B, LANES, 2 * W_B, W_M, W_M)
    dtypes = (BF16,) + (F32,) * 6 + (BF16, F32)
    return pl.pallas_call(
        _in_proj_kernel,
        grid=(n // tm,),
        in_specs=[
            pl.BlockSpec((tm, D_MODEL), row),
            pl.BlockSpec((1, D_MODEL), const),
            pl.BlockSpec((D_MODEL, N_IN_PAD), const),
            pl.BlockSpec((MXU_DIM, MXU_DIM), const),
            pl.BlockSpec((1, W_A), const),
            pl.BlockSpec((1, W_A), const),
            pl.BlockSpec((1, W_M), const),
            pl.BlockSpec((1, LANES), const),
        ],
        out_specs=[pl.BlockSpec((tm, w), row) for w in widths]
        + [pl.BlockSpec((tm, H_A, DV_A), lambda i: (i, 0, 0))] * 2,
        out_shape=[jax.ShapeDtypeStruct((n, w), dt) for w, dt in zip(widths, dtypes)]
        + [jax.ShapeDtypeStruct((n, H_A, DV_A), F32)] * 2,
        compiler_params=pltpu.CompilerParams(
            dimension_semantics=("arbitrary",), vmem_limit_bytes=VMEM_LIMIT),
        name="in_proj",
    )(x2d, g_norm, w_perm, g_mat, g_qa, g_ka, g_qm, b_if)


def _lambda_value(lq1_ref, lk1_ref, lq2_ref, lk2_ref, lam_init):
    s1 = jnp.sum(lq1_ref[...] * lk1_ref[...], axis=-1, keepdims=True)
    s2 = jnp.sum(lq2_ref[...] * lk2_ref[...], axis=-1, keepdims=True)
    return jnp.exp(s1) - jnp.exp(s2) + lam_init


def _subln(o, gsub_row, lam_init):
    ms = jnp.mean(o * o, axis=-1, keepdims=True)
    return o * lax.rsqrt(ms + EPS) * gsub_row * (1.0 - lam_init)


def _prompt_attn_kernel(lq1_ref, lk1_ref, lq2_ref, lk2_ref, gsub_ref, q_ref, k_ref, v_ref,
                        ga_ref, o_ref, kaug_ref, vt_ref, m_ref, acc_ref, s_ref, qt_ref,
                        *, tile, lam_init):
    i = pl.program_id(1)
    seq = k_ref.shape[0]

    @pl.when(i == 0)
    def _stage_keys_values():
        lane = lax.broadcasted_iota(jnp.int32, (seq, LANES), 1)
        pos = lax.broadcasted_iota(jnp.int32, (seq, LANES), 0)
        in_tile = (pos % tile).astype(F32)
        tile_base = (pos - pos % tile).astype(F32)
        for hd in range(H_A):
            slab = k_ref[:, hd * DV_A:(hd + 1) * DV_A]
            aug = jnp.where((lane >= DK_A) & (lane < DK_A + 3), ALIBI_SLOPES[hd] * in_tile,
                            jnp.where((lane >= DK_A + 3) & (lane < DK_A + 6),
                                      ALIBI_SLOPES[hd] * tile_base, 0.0))
            kaug_ref[2 * hd] = jnp.where(lane < DK_A, slab, aug).astype(BF16)
            kaug_ref[2 * hd + 1] = jnp.where(
                lane < DK_A, pltpu.roll(slab, DK_A, 1), aug).astype(BF16)
        for t in range(seq // tile):
            v_t = v_ref[t * tile:(t + 1) * tile, :].T.astype(BF16)
            for hd in range(H_A):
                vt_ref[t, hd, 0:DV_A, :] = v_t[hd * DV_A:(hd + 1) * DV_A, :]
                vt_ref[t, hd, DV_A:VT_ROWS, :] = jnp.ones((VT_ROWS - DV_A, tile), BF16)

    n_chain = 2 * H_A

    def scores(c, j):
        start = pl.multiple_of(j * tile, tile)
        s_ref[c] = _dot(kaug_ref[c, pl.ds(start, tile), :], qt_ref[c])

    def softmax_values(c, j, bias_of_head):
        s = s_ref[c]
        if bias_of_head is not None:
            s = s + bias_of_head[c // 2]
        mx = m_ref[c]
        mx_new = jnp.maximum(mx, jnp.max(s, axis=0, keepdims=True))
        p = jnp.exp2(s - mx_new)
        alpha = jnp.exp2(mx - mx_new)
        m_ref[c] = mx_new
        acc_ref[c] = alpha * acc_ref[c] + _dot(vt_ref[j, c // 2], p.astype(BF16))

    for c in range(n_chain):
        qt_ref[c] = q_ref[:, c * LANES:(c + 1) * LANES].astype(F32).T.astype(BF16)

    m_ref[...] = jnp.full(m_ref.shape, NEG_BIG, F32)
    acc_ref[...] = jnp.zeros(acc_ref.shape, F32)

    for c in range(n_chain):
        scores(c, 0)

    def past_tile(j, carry):
        for c in range(n_chain):
            softmax_values(c, j, None)
            scores(c, j + 1)
        return carry

    lax.fori_loop(0, i, past_tile, 0)

    key = lax.broadcasted_iota(jnp.int32, (tile, tile), 0)
    qry = lax.broadcasted_iota(jnp.int32, (tile, tile), 1)
    ahead = (key - qry).astype(F32)
    diag_bias = [jnp.where(key // CHUNK > qry // CHUNK, NEG_BIG,
                           jnp.where(key > qry, (-2.0 * LOG2E * ALIBI_SLOPES[hd]) * ahead, 0.0))
                 for hd in range(H_A)]
    for c in range(n_chain):
        softmax_values(c, i, diag_bias)

    lam = _lambda_value(lq1_ref, lk1_ref, lq2_ref, lk2_ref, lam_init)
    for hd in range(H_A):
        a1 = acc_ref[2 * hd]
        a2 = acc_ref[2 * hd + 1]
        o1 = a1[0:DV_A] * (1.0 / a1[DV_A:DV_A + 1])
        o2 = a2[0:DV_A] * (1.0 / a2[DV_A:DV_A + 1])
        o = (o1 - lam * o2).T
        gate = ga_ref[:, hd * DV_A:(hd + 1) * DV_A]
        o_ref[:, hd * DV_A:(hd + 1) * DV_A] = (
            _subln(o, gsub_ref[...], lam_init) * gate).astype(BF16)


def _prompt_attn(qaug, k, v, ga, lam_vecs, g_subln, lam_init, tile):
    b, s, _ = k.shape
    const = lambda bi, qi: (0, 0)
    tok = lambda bi, qi: (bi, qi, 0)
    per_b = lambda bi, qi: (bi, 0, 0)
    kern = functools.partial(_prompt_attn_kernel, tile=tile, lam_init=lam_init)
    return pl.pallas_call(
        kern,
        grid=(b, s // tile),
        in_specs=[pl.BlockSpec((1, DK_A), const)] * 4 + [
            pl.BlockSpec((1, DV_A), const),
            pl.BlockSpec((None, tile, 2 * W_A), tok),
            pl.BlockSpec((None, s, W_A), per_b),
            pl.BlockSpec((None, s, W_A), per_b),
            pl.BlockSpec((None, tile, W_A), tok),
        ],
        out_specs=pl.BlockSpec((None, tile, W_A), tok),
        out_shape=jax.ShapeDtypeStruct((b, s, W_A), BF16),
        scratch_shapes=[
            pltpu.VMEM((2 * H_A, s, LANES), BF16),
            pltpu.VMEM((s // tile, H_A, VT_ROWS, tile), BF16),
            pltpu.VMEM((2 * H_A, 1, tile), F32),
            pltpu.VMEM((2 * H_A, VT_ROWS, tile), F32),
            pltpu.VMEM((2 * H_A, tile, tile), F32),
            pltpu.VMEM((2 * H_A, LANES, tile), BF16),
        ],
        compiler_params=pltpu.CompilerParams(
            dimension_semantics=("arbitrary", "arbitrary"), vmem_limit_bytes=VMEM_LIMIT),
        name="diff_attn_prompt",
    )(*lam_vecs, g_subln, qaug, k, v, ga)


def _sample_attn_kernel(lq1_ref, lk1_ref, lq2_ref, lk2_ref, gsub_ref, q_ref, kc_ref, vc_ref,
                        kn_ref, vn_ref, ga_ref, o_ref, *, lam_init):
    sq = q_ref.shape[0]
    past = kc_ref.shape[0]
    lam = _lambda_value(lq1_ref, lk1_ref, lq2_ref, lk2_ref, lam_init)
    lane = lax.broadcasted_iota(jnp.int32, (sq, LANES), 1)
    qpos_c = past + lax.broadcasted_iota(jnp.int32, (sq, past), 0)
    dist_c = jnp.abs(qpos_c - lax.broadcasted_iota(jnp.int32, (sq, past), 1)).astype(F32)
    dist_n = jnp.abs(lax.broadcasted_iota(jnp.int32, (sq, sq), 0)
                     - lax.broadcasted_iota(jnp.int32, (sq, sq), 1)).astype(F32)
    for hd in range(H_A):
        slope = ALIBI_SLOPES[hd]
        kc = kc_ref[:, hd, :].astype(BF16)
        kn = kn_ref[:, hd * DV_A:(hd + 1) * DV_A].astype(BF16)
        vc = vc_ref[:, hd, :].astype(BF16)
        vn = vn_ref[:, hd * DV_A:(hd + 1) * DV_A].astype(BF16)
        outs = []
        for m in range(2):
            q = q_ref[:, (2 * hd + m) * LANES:(2 * hd + m + 1) * LANES].astype(F32)
            if m == 0:
                q = jnp.where(lane < DK_A, q, 0.0)
            else:
                q = jnp.where(lane >= DK_A, pltpu.roll(q, DK_A, 1), 0.0)
            q = q.astype(BF16)
            s_c = _dot_nt(q, kc) - (slope * LOG2E) * dist_c
            s_n = _dot_nt(q, kn) - (slope * LOG2E) * dist_n
            mx = jnp.maximum(jnp.max(s_c, axis=-1, keepdims=True),
                             jnp.max(s_n, axis=-1, keepdims=True))
            p_c = jnp.exp2(s_c - mx)
            p_n = jnp.exp2(s_n - mx)
            l = jnp.sum(p_c, axis=-1, keepdims=True) + jnp.sum(p_n, axis=-1, keepdims=True)
            acc = _dot(p_c.astype(BF16), vc) + _dot(p_n.astype(BF16), vn)
            outs.append(acc * (1.0 / l))
        o = outs[0] - lam * outs[1]
        gate = ga_ref[:, hd * DV_A:(hd + 1) * DV_A]
        o_ref[:, hd * DV_A:(hd + 1) * DV_A] = (
            _subln(o, gsub_ref[...], lam_init) * gate).astype(BF16)


def _sample_attn(qaug, k_cache, v_cache, k_new, v_new, ga, lam_vecs, g_subln, lam_init):
    b, past = k_cache.shape[:2]
    sq = k_new.shape[1]
    per_b4 = lambda bi: (bi, 0, 0, 0)
    assert (past + sq - 1) // CHUNK <= past // CHUNK
    const = lambda bi: (0, 0)
    per_b = lambda bi: (bi, 0, 0)
    kern = functools.partial(_sample_attn_kernel, lam_init=lam_init)
    return pl.pallas_call(
        kern,
        grid=(b,),
        in_specs=[pl.BlockSpec((1, DK_A), const)] * 4 + [
            pl.BlockSpec((1, DV_A), const),
            pl.BlockSpec((None, sq, 2 * W_A), per_b),
            pl.BlockSpec((None, past, H_A, DV_A), per_b4),
            pl.BlockSpec((None, past, H_A, DV_A), per_b4),
            pl.BlockSpec((None, sq, W_A), per_b),
            pl.BlockSpec((None, sq, W_A), per_b),
            pl.BlockSpec((None, sq, W_A), per_b),
        ],
        out_specs=pl.BlockSpec((None, sq, W_A), per_b),
        out_shape=jax.ShapeDtypeStruct((b, sq, W_A), BF16),
        compiler_params=pltpu.CompilerParams(
            dimension_semantics=("arbitrary",), vmem_limit_bytes=VMEM_LIMIT),
        name="diff_attn_sample",
    )(*lam_vecs, g_subln, qaug, k_cache, v_cache, k_new, v_new, ga)


def _mlstm_kernel(mb_ref, gl_ref, og_ref, gmh_ref, c0_ref, n0_ref, m0_ref,
                  h_ref, c_ref, n_ref, m_ref, *, valid):
    c_idx = pl.program_id(1)
    L = CHUNK
    W = W_B
    G = mb_ref.shape[0]
    R = G * L

    @pl.when(c_idx == 0)
    def _load_state():
        c_ref[...] = c0_ref[...]
        n_ref[...] = n0_ref[...]
        m_ref[...] = m0_ref[...]

    lane_head = lax.broadcasted_iota(jnp.int32, (R, W), 1) // DH_B
    row_t = lax.broadcasted_iota(jnp.int32, (R, W), 0) % L
    lane_s = lax.broadcasted_iota(jnp.int32, (R, W), 1) % DH_B
    wi = lax.broadcasted_iota(jnp.int32, (W, W), 0)
    wj = lax.broadcasted_iota(jnp.int32, (W, W), 1)
    blk = wi // DH_B == wj // DH_B
    ones_bd = blk.astype(BF16)
    ei = lax.broadcasted_iota(jnp.int32, (LANES, 2 * W), 0)
    ej = lax.broadcasted_iota(jnp.int32, (LANES, 2 * W), 1)
    expand = (ei == ej // DH_B).astype(BF16)
    tri = (lax.broadcasted_iota(jnp.int32, (L, L), 1)
           <= lax.broadcasted_iota(jnp.int32, (L, L), 0)).astype(BF16)
    nar_lane = lax.broadcasted_iota(jnp.int32, (R, LANES), 1)
    nar_t = lax.broadcasted_iota(jnp.int32, (R, LANES), 0) % L

    def per_seq(x):
        return x.reshape(G, L, x.shape[-1])

    def stacked(x):
        return x.reshape(R, x.shape[-1])

    def seq_rows(x, g):
        return x[g * L:(g + 1) * L]

    def exact_matmul(xs, mat):
        n = xs[0].shape[0]
        pieces = [part for x in xs for part in _split3(x)]
        out = _dot(jnp.concatenate(pieces, axis=0), mat)
        return [out[(3 * i) * n:(3 * i + 1) * n] + out[(3 * i + 1) * n:(3 * i + 2) * n]
                + out[(3 * i + 2) * n:(3 * i + 3) * n] for i in range(len(xs))]

    def head_max(x):
        out = jnp.zeros((R, W), F32)
        for hd in range(H_B):
            mx = jnp.max(jnp.where(lane_head == hd, x, NEG_BIG), axis=-1, keepdims=True)
            out = jnp.where(lane_head == hd, mx, out)
        return out

    q = stacked(mb_ref[:, :, 0:W])
    k = stacked(mb_ref[:, :, W:2 * W])
    v = stacked(mb_ref[:, :, 2 * W:3 * W])
    gl = jnp.where(nar_lane < 2 * H_B, stacked(gl_ref[...]), 0.0)
    gl = jnp.where(nar_t < valid, gl, jnp.where(nar_lane < H_B, NEG_BIG, 0.0))
    cums = []
    for g in range(G):
        out = _dot(tri, jnp.concatenate(_split3(seq_rows(gl, g)), axis=1))
        cums.append(out[:, 0:LANES] + out[:, LANES:2 * LANES] + out[:, 2 * LANES:3 * LANES])
    narrow = jnp.where(nar_lane < H_B, gl, jnp.concatenate(cums, axis=0))
    (wide,) = exact_matmul([narrow], expand)
    ig_all = wide[:, 0:W]
    b_all = wide[:, W:2 * W]

    r_all = b_all - ig_all
    r_row = jnp.sum(per_seq(jnp.where(lane_s == row_t, r_all, 0.0)), axis=1, keepdims=True)
    m0_row = m_ref[...]
    n0_row = n_ref[...]
    b_seq = per_seq(b_all)
    d_mat = jnp.where(lane_s <= row_t, stacked(b_seq - r_row), NEG_BIG)
    g_all = stacked(b_seq + m0_row)
    m_all = jnp.maximum(g_all, head_max(d_mat))
    d_w = jnp.exp(d_mat - m_all)
    g_w = jnp.exp(g_all - m_all)

    q_bf = q.astype(BF16)
    zero_bf = jnp.zeros((W, W), BF16)
    scores, inter, v_bds = [], [], []
    for g in range(G):
        k_t = seq_rows(k, g).T.astype(BF16)
        k_bd = jnp.where(blk, jnp.concatenate([k_t] * H_B, axis=1), zero_bf)
        v_bds.append(jnp.where(
            blk, jnp.concatenate([seq_rows(v, g).astype(BF16)] * H_B, axis=0), zero_bf))
        c0_bd = jnp.where(blk, c_ref[g].astype(BF16), zero_bf)
        scores.append(_dot(seq_rows(q_bf, g), k_bd))
        inter.append(_dot(seq_rows(q_bf, g), c0_bd))
    a_mat = jnp.concatenate(scores, axis=0) * d_w
    a_bf = a_mat.astype(BF16)
    intra = [_dot(seq_rows(a_bf, g), v_bds[g]) for g in range(G)]
    num = g_w * jnp.concatenate(inter, axis=0) + jnp.concatenate(intra, axis=0)

    qn, a_sum = exact_matmul([stacked(per_seq(q) * n0_row), a_mat], ones_bd)
    den = g_w * qn + a_sum
    h = num / jnp.maximum(jnp.abs(den), jnp.exp(-m_all))
    h_ms = _dot((h * h).astype(BF16), ones_bd) * (1.0 / DH_B)
    h = h * lax.rsqrt(h_ms + EPS) * gmh_ref[...]
    og = stacked(og_ref[...])
    h_ref[...] = per_seq((h * og[:, 0:W] * og[:, W:2 * W]).astype(BF16))

    m_last = per_seq(m_all)[:, L - 1:L, :]
    b_last = b_seq[:, L - 1:L, :]
    w_s = jnp.exp(stacked(b_last - b_seq + per_seq(ig_all) - m_last))
    decay = jnp.exp(b_last + m0_row - m_last)
    kw = k * w_s
    v_bf = v.astype(BF16)
    for g in range(G):
        c_ref[g] = decay[g] * c_ref[g] + _dot(seq_rows(kw, g).T.astype(BF16), seq_rows(v_bf, g))
    n_ref[...] = decay * n0_row + jnp.sum(per_seq(kw), axis=1, keepdims=True)
    m_ref[...] = m_last


def _mlstm(mb, gl, og, g_mh_row, c0_bd, n0_row, m0_row, valid, n_seq):
    b, s, _ = mb.shape
    nc = s // CHUNK
    tok = lambda bi, ci: (bi, ci, 0)
    st = lambda bi, ci: (bi, 0, 0)
    kern = functools.partial(_mlstm_kernel, valid=valid)
    return pl.pallas_call(
        kern,
        grid=(b // n_seq, nc),
        in_specs=[
            pl.BlockSpec((n_seq, CHUNK, 3 * W_B), tok),
            pl.BlockSpec((n_seq, CHUNK, LANES), tok),
            pl.BlockSpec((n_seq, CHUNK, 2 * W_B), tok),
            pl.BlockSpec((1, W_B), lambda bi, ci: (0, 0)),
            pl.BlockSpec((n_seq, W_B, W_B), st),
            pl.BlockSpec((n_seq, 1, W_B), st),
            pl.BlockSpec((n_seq, 1, W_B), st),
        ],
        out_specs=[
            pl.BlockSpec((n_seq, CHUNK, W_B), tok),
            pl.BlockSpec((n_seq, W_B, W_B), st),
            pl.BlockSpec((n_seq, 1, W_B), st),
            pl.BlockSpec((n_seq, 1, W_B), st),
        ],
        out_shape=[
            jax.ShapeDtypeStruct((b, s, W_B), BF16),
            jax.ShapeDtypeStruct((b, W_B, W_B), F32),
            jax.ShapeDtypeStruct((b, 1, W_B), F32),
            jax.ShapeDtypeStruct((b, 1, W_B), F32),
        ],
        compiler_params=pltpu.CompilerParams(
            dimension_semantics=("arbitrary", "arbitrary"), vmem_limit_bytes=VMEM_LIMIT),
        name="mlstm",
    )(mb, gl, og, g_mh_row, c0_bd, n0_row, m0_row)


def _mix_out_kernel(x_ref, oa_ref, hb_ref, qm_ref, gm_ref, mk_ref, mvt_ref, wo_ref, y_ref,
                    *, sub_rows):
    tm = x_ref.shape[0]
    sub = min(tm, sub_rows)
    mk = mk_ref[...]
    mvt = mvt_ref[...]
    k_feat_head = lax.broadcasted_iota(jnp.int32, mk.shape, 1) // DH_M
    v_feat_head = lax.broadcasted_iota(jnp.int32, mvt.shape, 0) // DH_M
    k_of_head = [jnp.where(k_feat_head == hd, mk, 0.0).astype(BF16) for hd in range(H_M)]
    vt_of_head = [jnp.where(v_feat_head == hd, mvt, 0.0).astype(BF16) for hd in range(H_M)]

    def probs(s):
        p = jnp.exp2(s - jnp.max(s, axis=0, keepdims=True))
        return (p * (1.0 / jnp.sum(p, axis=0, keepdims=True))).astype(BF16)

    def sub_tile_stages(r0):
        rows = slice(r0, r0 + sub)
        qm = qm_ref[rows, :]
        if sub < LANES:
            qm = jnp.concatenate([qm, jnp.zeros((LANES - sub, W_M), BF16)], axis=0)
        s0 = _dot_nt(k_of_head[0], qm)
        y = x_ref[rows, :] + _dot(oa_ref[rows, :], wo_ref[0:W_A, :])
        yield
        s1 = _dot_nt(k_of_head[1], qm)
        p0 = probs(s0)
        y = y + _dot(hb_ref[rows, :], wo_ref[W_A:W_A + W_B, :])
        yield
        s2 = _dot_nt(k_of_head[2], qm)
        p1 = probs(s1)
        omt = _dot(vt_of_head[0], p0)
        yield
        s3 = _dot_nt(k_of_head[3], qm)
        p2 = probs(s2)
        omt = omt + _dot(vt_of_head[1], p1)
        yield
        p3 = probs(s3)
        omt = omt + _dot(vt_of_head[2], p2)
        omt = omt + _dot(vt_of_head[3], p3)
        yield
        om = omt.T[0:sub] * gm_ref[rows, :]
        y_ref[rows, :] = y + _dot(om.astype(BF16), wo_ref[W_A + W_B:D_MIX, :])
        yield

    n_stage = 6
    gens = [sub_tile_stages(r0) for r0 in range(0, tm, sub)]
    for step in range(n_stage + len(gens) - 1):
        for idx, gen in enumerate(gens):
            if 0 <= step - idx < n_stage:
                next(gen)


def _mix_out(x, oa, hb, qm, gm, mem_k, mem_vt, w_out_bf, tm):
    b, s, _ = x.shape
    tok = lambda bi, ti: (bi, ti, 0)
    per_b = lambda bi, ti: (bi, 0, 0)
    const = lambda bi, ti: (0, 0)
    return pl.pallas_call(
        functools.partial(_mix_out_kernel, sub_rows=MXU_DIM),
        grid=(b, s // tm),
        in_specs=[
            pl.BlockSpec((None, tm, D_MODEL), tok),
            pl.BlockSpec((None, tm, W_A), tok),
            pl.BlockSpec((None, tm, W_B), tok),
            pl.BlockSpec((None, tm, W_M), tok),
            pl.BlockSpec((None, tm, W_M), tok),
            pl.BlockSpec((None, N_MEM, W_M), per_b),
            pl.BlockSpec((None, N_MEM, W_M), per_b),
            pl.BlockSpec((D_MIX, D_MODEL), const),
        ],
        out_specs=pl.BlockSpec((None, tm, D_MODEL), tok),
        out_shape=jax.ShapeDtypeStruct((b, s, D_MODEL), F32),
        compiler_params=pltpu.CompilerParams(
            dimension_semantics=("arbitrary", "arbitrary"), vmem_limit_bytes=VMEM_LIMIT),
        name="mix_out",
    )(x, oa, hb, qm, gm, mem_k, mem_vt, w_out_bf)


def _block_diag_state(c):
    b = c.shape[0]
    eye = jnp.eye(H_B, dtype=c.dtype)
    return jnp.einsum('bhde,hg->bhdge', c, eye).reshape(b, W_B, W_B)


def _diag_blocks(c_bd):
    b = c_bd.shape[0]
    c5 = c_bd.reshape(b, H_B, DH_B, H_B, DH_B)
    return jnp.stack([c5[:, h, :, h, :] for h in range(H_B)], axis=1)


def _mixer_layer(x, past_kv, mlstm_state, mem_k, mem_vt, lam_init, p):
    b, s, _ = x.shape
    n = b * s
    tm = min(512, n)
    qaug, k_new, v_new, ga, mb, gl, og, qm, gm, k4, v4 = _in_proj(
        x.reshape(n, D_MODEL), tm, p["g_norm"], p["w_perm"], p["g_mat"],
        p["g_qa"], p["g_ka"], p["g_qm"], p["b_if"])
    k_new = k_new.reshape(b, s, W_A)
    v_new = v_new.reshape(b, s, W_A)
    qaug = qaug.reshape(b, s, 2 * W_A)
    ga = ga.reshape(b, s, W_A)

    if past_kv is None:
        oa = _prompt_attn(qaug, k_new, v_new, ga, p["lam_vecs"], p["g_subln"], lam_init,
                          tile=256)
    else:
        oa = _sample_attn(qaug, past_kv[0], past_kv[1], k_new, v_new, ga,
                          p["lam_vecs"], p["g_subln"], lam_init)

    c0, n0, m0 = mlstm_state
    s_pad = -(-s // CHUNK) * CHUNK
    mb3 = mb.reshape(b, s, 3 * W_B)
    gl3 = gl.reshape(b, s, LANES)
    og3 = og.reshape(b, s, 2 * W_B)
    if s_pad != s:
        pad = ((0, 0), (0, s_pad - s), (0, 0))
        mb3, gl3, og3 = jnp.pad(mb3, pad), jnp.pad(gl3, pad), jnp.pad(og3, pad)
    valid = CHUNK if s_pad == s else s
    hb, c_bd, n_row, m_row = _mlstm(
        mb3, gl3, og3, p["g_mh"], _block_diag_state(c0), n0.reshape(b, 1, W_B),
        jnp.repeat(m0, DH_B, axis=-1).reshape(b, 1, W_B), valid, n_seq=8)
    hb = hb[:, :s]
    new_state = (_diag_blocks(c_bd), n_row.reshape(b, H_B, DH_B),
                 m_row.reshape(b, H_B, DH_B)[:, :, 0])

    y = _mix_out(x, oa, hb, qm.reshape(b, s, W_M), gm.reshape(b, s, W_M), mem_k, mem_vt,
                 p["w_out"], tm=min(1024, s))
    return (y, k4.reshape(b, s, H_A, DV_A), v4.reshape(b, s, H_A, DV_A), new_state)


def kernel(x_prompt, x_sample, cache_attn_k, cache_attn_v, state_mlstm_C, state_mlstm_n,
           state_mlstm_m, cache_mem_k, cache_mem_v, mem_prompt, g_norm, w_in, w_out, g_qa,
           g_ka, lam_q1, lam_k1, lam_q2, lam_k2, g_subln, b_i, b_f, g_mh, g_qm, g_km, g_mem,
           w_mk, w_mv):
    depth = w_in.shape[0]
    bp = x_prompt.shape[0]
    bs = x_sample.shape[0]
    past = cache_attn_k.shape[2]
    gi = lax.broadcasted_iota(jnp.int32, (MXU_DIM, MXU_DIM), 0) // DH_B
    gj = lax.broadcasted_iota(jnp.int32, (MXU_DIM, MXU_DIM), 1) // DH_B
    g_mat = jnp.where(gi == gj, 1.0 / DH_B, 0.0).astype(BF16)

    xp, xs = x_prompt, x_sample
    outs = {name: [] for name in ("pk", "pv", "pC", "pn", "pm", "pmk", "pmv",
                                  "sk", "sv", "sC", "sn", "sm")}
    for l in range(depth):
        lam_init = 0.8 - 0.6 * math.exp(-0.3 * l)
        w = w_in[l]
        w_perm = jnp.concatenate(
            [w[:, :4 * W_A + 4 * W_B], w[:, 4 * W_A + 4 * W_B + 2 * H_B:],
             w[:, 4 * W_A + 4 * W_B:4 * W_A + 4 * W_B + 2 * H_B],
             jnp.zeros((D_MODEL, LANES - 2 * H_B), F32)], axis=1).astype(BF16)
        p = {
            "g_norm": g_norm[l].reshape(1, D_MODEL),
            "w_perm": w_perm,
            "g_mat": g_mat,
            "g_qa": jnp.tile(g_qa[l], 2 * H_A).reshape(1, W_A),
            "g_ka": jnp.tile(g_ka[l], 2 * H_A).reshape(1, W_A),
            "g_qm": jnp.tile(g_qm[l], H_M).reshape(1, W_M),
            "b_if": jnp.concatenate(
                [b_i[l], b_f[l], jnp.zeros((LANES - 2 * H_B,), F32)]).reshape(1, LANES),
            "lam_vecs": tuple(v[l].reshape(1, DK_A) for v in (lam_q1, lam_k1, lam_q2, lam_k2)),
            "g_subln": g_subln[l].reshape(1, DV_A),
            "g_mh": jnp.tile(g_mh[l], H_B).reshape(1, W_B),
            "w_out": w_out[l].astype(BF16),
        }
        mk, mkt, mvt = _memory_kv(mem_prompt, g_mem[l], w_mk[l], w_mv[l], g_km[l], g_mat)
        zero_state = (jnp.zeros((bp, H_B, DH_B, DH_B), F32), jnp.zeros((bp, H_B, DH_B), F32),
                      jnp.zeros((bp, H_B), F32))
        xp, k_p, v_p, st_p = _mixer_layer(xp, None, zero_state, mk, mvt, lam_init, p)
        xs, k_s, v_s, st_s = _mixer_layer(
            xs, (cache_attn_k[l], cache_attn_v[l]),
            (state_mlstm_C[l], state_mlstm_n[l], state_mlstm_m[l]),
            cache_mem_k[l].reshape(bs, N_MEM, W_M),
            cache_mem_v[l].transpose(0, 2, 3, 1).reshape(bs, W_M, N_MEM),
            lam_init, p)

        def tokens_first(t):
            return t.reshape(bp, H_M, DH_M, N_MEM).transpose(0, 3, 1, 2)

        outs["pk"].append(k_p); outs["pv"].append(v_p)
        outs["pC"].append(st_p[0]); outs["pn"].append(st_p[1]); outs["pm"].append(st_p[2])
        outs["pmk"].append(tokens_first(mkt))
        outs["pmv"].append(tokens_first(mvt))
        outs["sk"].append(k_s); outs["sv"].append(v_s)
        outs["sC"].append(st_s[0]); outs["sn"].append(st_s[1]); outs["sm"].append(st_s[2])
    stk = {name: jnp.stack(vals) for name, vals in outs.items()}
    return (xp, xs, stk["pk"], stk["pv"], stk["pC"], stk["pn"], stk["pm"], stk["pmk"],
            stk["pmv"], stk["sk"], stk["sv"], stk["sC"], stk["sn"], stk["sm"])
```

```python
import functools
import math

import jax
import jax.numpy as jnp
import numpy as np
from jax import lax
from jax.experimental import pallas as pl
from jax.experimental.pallas import tpu as pltpu

F32 = jnp.float32
BF16 = jnp.bfloat16

D_MODEL = 1024
CHUNK = 64
N_MEM = 256
H_A, DK_A = 4, 64
DV_A = 2 * DK_A
W_A = H_A * DV_A
H_B, DH_B = 4, 64
W_B = H_B * DH_B
H_M, DH_M = 4, 64
W_M = H_M * DH_M
D_MIX = W_A + W_B + W_M
EPS = 1e-6
ALIBI_SLOPES = tuple(2.0 ** (-8.0 * (h + 1) / H_A) for h in range(H_A))
N_IN = 4 * W_A + 5 * W_B + 2 * H_B + 2 * W_M

LANES = 128
MXU_DIM = 256
BF16_SUBLANES = 16
VT_ROWS = DV_A + BF16_SUBLANES
NEG_BIG = -1e30
LOG2E = math.log2(math.e)


def _bf16_pieces(value, n=3):
    pieces, rest = [], np.float32(value)
    for _ in range(n):
        piece = np.float32(rest.astype(jnp.bfloat16))
        pieces.append(float(piece))
        rest = np.float32(rest - piece)
    return tuple(pieces)


LOG2E_BF16_PIECES = _bf16_pieces(LOG2E)

C_QA, C_KA, C_VA, C_GA = 0, W_A, 2 * W_A, 3 * W_A
C_QB = 4 * W_A
C_KB, C_VB, C_OB, C_GB = C_QB + W_B, C_QB + 2 * W_B, C_QB + 3 * W_B, C_QB + 4 * W_B
C_QM = C_GB + W_B
C_GM = C_QM + W_M
C_IF = C_GM + W_M
N_IN_PAD = C_IF + LANES

VMEM_LIMIT = 56 * 1024 * 1024


def _dot(a, b):
    return jnp.dot(a, b, preferred_element_type=F32)


def _dot_nt(a, b):
    return lax.dot_general(a, b, (((1,), (1,)), ((), ())), preferred_element_type=F32)


def _split3(x):
    hi = x.astype(BF16)
    r1 = x - hi.astype(F32)
    mid = r1.astype(BF16)
    lo = (r1 - mid.astype(F32)).astype(BF16)
    return hi, mid, lo


def _group_mean_sq(z, g_mat):
    zz = (z * z).astype(BF16)
    parts = [_dot(zz[:, c:c + MXU_DIM], g_mat) for c in range(0, z.shape[1], MXU_DIM)]
    return parts[0] if len(parts) == 1 else jnp.concatenate(parts, axis=1)


def _group_rms_norm(z, gain_row, g_mat):
    return z * lax.rsqrt(_group_mean_sq(z, g_mat) + EPS) * gain_row


def _log_sigmoid(u):
    return -(jnp.maximum(-u, 0.0) + jnp.log1p(jnp.exp(-jnp.abs(u))))


def _silu(u):
    return u * jax.nn.sigmoid(u)


def _memory_kv_kernel(mem_ref, gmem_ref, wk_ref, wv_ref, gkm_ref, gmat_ref,
                      mk_ref, mkt_ref, mvt_ref, mv_scr):
    x = mem_ref[...]
    ms = jnp.mean(x * x, axis=-1, keepdims=True)
    hm = (x * lax.rsqrt(ms + EPS) * gmem_ref[...]).astype(BF16)
    mk_ref[...] = _group_rms_norm(_dot(hm, wk_ref[...]), gkm_ref[...], gmat_ref[...])
    mv_scr[...] = _dot(hm, wv_ref[...])
    mkt_ref[...] = mk_ref[...].T
    mvt_ref[...] = mv_scr[...].T


def _memory_kv(mem, g_mem, w_mk, w_mv, g_km, g_mat):
    b, n, d = mem.shape
    row = lambda i: (i, 0, 0)
    const2 = lambda i: (0, 0)
    return pl.pallas_call(
        _memory_kv_kernel,
        grid=(b,),
        in_specs=[
            pl.BlockSpec((None, n, d), row),
            pl.BlockSpec((1, d), const2),
            pl.BlockSpec((d, W_M), const2),
            pl.BlockSpec((d, W_M), const2),
            pl.BlockSpec((1, W_M), const2),
            pl.BlockSpec((MXU_DIM, MXU_DIM), const2),
        ],
        out_specs=[pl.BlockSpec((None, n, W_M), row), pl.BlockSpec((None, W_M, n), row),
                   pl.BlockSpec((None, W_M, n), row)],
        out_shape=[jax.ShapeDtypeStruct((b, n, W_M), F32),
                   jax.ShapeDtypeStruct((b, W_M, n), F32),
                   jax.ShapeDtypeStruct((b, W_M, n), F32)],
        scratch_shapes=[pltpu.VMEM((n, W_M), F32)],
        compiler_params=pltpu.CompilerParams(
            dimension_semantics=("arbitrary",), vmem_limit_bytes=VMEM_LIMIT),
        name="memory_kv",
    )(mem, g_mem.reshape(1, d), w_mk.astype(BF16), w_mv.astype(BF16),
      jnp.tile(g_km, H_M).reshape(1, W_M), g_mat)


def _in_proj_kernel(x_ref, gn_ref, w_ref, gmat_ref, gqa_ref, gka_ref, gqm_ref, bif_ref,
                    qaug_ref, k_ref, v_ref, ga_ref, mb_ref, gl_ref, og_ref, qm_ref, gm_ref,
                    k4_ref, v4_ref):
    x = x_ref[...]
    h = (x * gn_ref[...]).astype(BF16)
    inv_rms = lax.rsqrt(jnp.mean(x * x, axis=-1, keepdims=True) + EPS)
    g_mat = gmat_ref[...]

    def proj(start, width):
        return _dot(h, w_ref[:, start:start + width]) * inv_rms

    qn = _group_rms_norm(proj(C_QA, W_A), gqa_ref[...], g_mat) * (DK_A ** -0.5 * LOG2E)
    lane = lax.broadcasted_iota(jnp.int32, (x.shape[0], LANES), 1)
    aug = jnp.zeros((x.shape[0], LANES), F32)
    for rep in range(2):
        for piece, val in enumerate(LOG2E_BF16_PIECES):
            aug = jnp.where(lane == DK_A + 3 * rep + piece, val, aug)
    for hd in range(H_A):
        slab = qn[:, hd * DV_A:(hd + 1) * DV_A]
        q1 = jnp.where(lane < DK_A, slab, aug)
        q2 = jnp.where(lane < DK_A, pltpu.roll(slab, DK_A, 1), aug)
        qaug_ref[:, (2 * hd) * LANES:(2 * hd + 1) * LANES] = q1.astype(BF16)
        qaug_ref[:, (2 * hd + 1) * LANES:(2 * hd + 2) * LANES] = q2.astype(BF16)

    kn = _group_rms_norm(proj(C_KA, W_A), gka_ref[...], g_mat)
    k_ref[...] = kn.astype(BF16)

    def cache_copy(dst_ref, src, hd, later):
        zero = pltpu.bitcast(
            lax.shift_right_logical(pltpu.bitcast(later[:, 0:DV_A], jnp.uint32), jnp.uint32(32)),
            F32)
        dst_ref[:, hd, :] = src[:, hd * DV_A:(hd + 1) * DV_A] + zero

    vn = proj(C_VA, W_A)
    v_ref[...] = vn.astype(BF16)
    cache_copy(k4_ref, kn, 0, vn)
    z = proj(C_GA, W_A)
    ga_ref[...] = _silu(z)
    cache_copy(k4_ref, kn, 1, z)
    z = proj(C_QB, W_B)
    mb_ref[:, 0:W_B] = z
    cache_copy(k4_ref, kn, 2, z)
    z = proj(C_KB, W_B)
    mb_ref[:, W_B:2 * W_B] = z * (DH_B ** -0.5)
    cache_copy(k4_ref, kn, 3, z)
    z = proj(C_VB, W_B)
    mb_ref[:, 2 * W_B:3 * W_B] = z
    cache_copy(v4_ref, vn, 0, z)
    z = proj(C_OB, W_B)
    og_ref[:, 0:W_B] = jax.nn.sigmoid(z)
    cache_copy(v4_ref, vn, 1, z)
    z = proj(C_GB, W_B)
    og_ref[:, W_B:2 * W_B] = _silu(z)
    cache_copy(v4_ref, vn, 2, z)
    z = proj(C_QM, W_M)
    qm_ref[...] = (_group_rms_norm(z, gqm_ref[...], g_mat)
                   * (DH_M ** -0.5 * LOG2E)).astype(BF16)
    cache_copy(v4_ref, vn, 3, z)
    gm_ref[...] = _silu(proj(C_GM, W_M))

    u = proj(C_IF, LANES) + bif_ref[...]
    gl_ref[...] = jnp.where(lane < H_B, u, _log_sigmoid(u))


def _in_proj(x2d, tm, g_norm, w_perm, g_mat, g_qa, g_ka, g_qm, b_if):
    n = x2d.shape[0]
    row = lambda i: (i, 0)
    const = lambda i: (0, 0)
    widths = (2 * W_A, W_A, W_A, W_A, 3 * W_B, LANES, 2 * W_B, W_M, W_M)
    dtypes = (BF16, BF16, BF16) + (F32,) * 4 + (BF16, F32)
    return pl.pallas_call(
        _in_proj_kernel,
        grid=(n // tm,),
        in_specs=[
            pl.BlockSpec((tm, D_MODEL), row),
            pl.BlockSpec((1, D_MODEL), const),
            pl.BlockSpec((D_MODEL, N_IN_PAD), const),
            pl.BlockSpec((MXU_DIM, MXU_DIM), const),
            pl.BlockSpec((1, W_A), const),
            pl.BlockSpec((1, W_A), const),
            pl.BlockSpec((1, W_M), const),
            pl.BlockSpec((1, LANES), const),
        ],
        out_specs=[pl.BlockSpec((tm, w), row) for w in widths]
        + [pl.BlockSpec((tm, H_A, DV_A), lambda i: (i, 0, 0))] * 2,
        out_shape=[jax.ShapeDtypeStruct((n, w), dt) for w, dt in zip(widths, dtypes)]
        + [jax.ShapeDtypeStruct((n, H_A, DV_A), F32)] * 2,
        compiler_params=pltpu.CompilerParams(
            dimension_semantics=("arbitrary",), vmem_limit_bytes=VMEM_LIMIT),
        name="in_proj",
    )(x2d, g_norm, w_perm, g_mat, g_qa, g_ka, g_qm, b_if)


def _lambda_value(lq1_ref, lk1_ref, lq2_ref, lk2_ref, lam_init):
    s1 = jnp.sum(lq1_ref[...] * lk1_ref[...], axis=-1, keepdims=True)
    s2 = jnp.sum(lq2_ref[...] * lk2_ref[...], axis=-1, keepdims=True)
    return jnp.exp(s1) - jnp.exp(s2) + lam_init


def _subln(o, gsub_row, lam_init):
    ms = jnp.mean(o * o, axis=-1, keepdims=True)
    return o * lax.rsqrt(ms + EPS) * gsub_row * (1.0 - lam_init)


def _prompt_attn_kernel(lq1_ref, lk1_ref, lq2_ref, lk2_ref, gsub_ref, q_ref, k_ref, v_ref,
                        ga_ref, o_ref, kaug_ref, vt_ref, m_ref, acc_ref, s_ref, qt_ref,
                        *, tile, lam_init):
    seq = k_ref.shape[0]
    n_tiles = seq // tile
    n_chain = 2 * H_A

    lane = lax.broadcasted_iota(jnp.int32, (seq, LANES), 1)
    pos = lax.broadcasted_iota(jnp.int32, (seq, LANES), 0)
    in_tile = (pos % tile).astype(F32)
    tile_base = (pos - pos % tile).astype(F32)
    for hd in range(H_A):
        slab = k_ref[:, hd * DV_A:(hd + 1) * DV_A].astype(F32)
        aug = jnp.where((lane >= DK_A) & (lane < DK_A + 3), ALIBI_SLOPES[hd] * in_tile,
                        jnp.where((lane >= DK_A + 3) & (lane < DK_A + 6),
                                  ALIBI_SLOPES[hd] * tile_base, 0.0))
        kaug_ref[2 * hd] = jnp.where(lane < DK_A, slab, aug).astype(BF16)
        kaug_ref[2 * hd + 1] = jnp.where(
            lane < DK_A, pltpu.roll(slab, DK_A, 1), aug).astype(BF16)
    for t in range(n_tiles):
        v_t = v_ref[t * tile:(t + 1) * tile, :].astype(F32).T.astype(BF16)
        for hd in range(H_A):
            vt_ref[t, hd, 0:DV_A, :] = v_t[hd * DV_A:(hd + 1) * DV_A, :]
            vt_ref[t, hd, DV_A:VT_ROWS, :] = jnp.ones((VT_ROWS - DV_A, tile), BF16)

    lam = _lambda_value(lq1_ref, lk1_ref, lq2_ref, lk2_ref, lam_init)
    key = lax.broadcasted_iota(jnp.int32, (tile, tile), 0)
    qry = lax.broadcasted_iota(jnp.int32, (tile, tile), 1)
    ahead = (key - qry).astype(F32)
    diag_bias = [jnp.where(key // CHUNK > qry // CHUNK, NEG_BIG,
                           jnp.where(key > qry, (-2.0 * LOG2E * ALIBI_SLOPES[hd]) * ahead, 0.0))
                 for hd in range(H_A)]

    def rows_of(i):
        if isinstance(i, int):
            return pl.ds(i * tile, tile)
        return pl.ds(pl.multiple_of(i * tile, tile), tile)

    def scores(par, c, j):
        s_ref[par, c] = _dot(kaug_ref[c, rows_of(j), :], qt_ref[par, c])

    def softmax_values(par, c, j, bias_of_head):
        s = s_ref[par, c]
        if bias_of_head is not None:
            s = s + bias_of_head[c // 2]
        mx = m_ref[par, c]
        mx_new = jnp.maximum(mx, jnp.max(s, axis=0, keepdims=True))
        p = jnp.exp2(s - mx_new)
        alpha = jnp.exp2(mx - mx_new)
        m_ref[par, c] = mx_new
        acc_ref[par, c] = alpha * acc_ref[par, c] + _dot(vt_ref[j, c // 2], p.astype(BF16))

    def begin_chain(par, c, i):
        q = q_ref[rows_of(i), c * LANES:(c + 1) * LANES]
        qt_ref[par, c] = q.astype(F32).T.astype(BF16)
        m_ref[par, c] = jnp.full((1, tile), NEG_BIG, F32)
        acc_ref[par, c] = jnp.zeros((VT_ROWS, tile), F32)
        scores(par, c, 0)

    def finish_head(par, hd, i):
        a1 = acc_ref[par, 2 * hd]
        a2 = acc_ref[par, 2 * hd + 1]
        o1 = a1[0:DV_A] * (1.0 / a1[DV_A:DV_A + 1])
        o2 = a2[0:DV_A] * (1.0 / a2[DV_A:DV_A + 1])
        o = (o1 - lam * o2).T
        gate = ga_ref[rows_of(i), hd * DV_A:(hd + 1) * DV_A]
        o_ref[rows_of(i), hd * DV_A:(hd + 1) * DV_A] = (
            _subln(o, gsub_ref[...], lam_init) * gate).astype(BF16)

    for c in range(n_chain):
        begin_chain(0, c, 0)

    def tile_pair(pair, carry):
        for par in range(2):
            i = 2 * pair + par

            def past_tile(j, inner, par=par):
                for c in range(n_chain):
                    softmax_values(par, c, j, None)
                    scores(par, c, j + 1)
                return inner

            lax.fori_loop(0, i, past_tile, 0)
            nxt = lax.rem(i + 1, n_tiles)
            for c in range(n_chain):
                softmax_values(par, c, i, diag_bias)
                begin_chain(1 - par, c, nxt)
                if c % 2 == 1:
                    finish_head(par, c // 2, i)
        return carry

    lax.fori_loop(0, n_tiles // 2, tile_pair, 0)


def _prompt_attn(qaug, k, v, ga, lam_vecs, g_subln, lam_init, tile):
    b, s, _ = k.shape
    assert (s // tile) % 2 == 0
    const = lambda bi: (0, 0)
    per_b = lambda bi: (bi, 0, 0)
    kern = functools.partial(_prompt_attn_kernel, tile=tile, lam_init=lam_init)
    return pl.pallas_call(
        kern,
        grid=(b,),
        in_specs=[pl.BlockSpec((1, DK_A), const)] * 4 + [
            pl.BlockSpec((1, DV_A), const),
            pl.BlockSpec((None, s, 2 * W_A), per_b),
            pl.BlockSpec((None, s, W_A), per_b),
            pl.BlockSpec((None, s, W_A), per_b),
            pl.BlockSpec((None, s, W_A), per_b),
        ],
        out_specs=pl.BlockSpec((None, s, W_A), per_b),
        out_shape=jax.ShapeDtypeStruct((b, s, W_A), BF16),
        scratch_shapes=[
            pltpu.VMEM((2 * H_A, s, LANES), BF16),
            pltpu.VMEM((s // tile, H_A, VT_ROWS, tile), BF16),
            pltpu.VMEM((2, 2 * H_A, 1, tile), F32),
            pltpu.VMEM((2, 2 * H_A, VT_ROWS, tile), F32),
            pltpu.VMEM((2, 2 * H_A, tile, tile), F32),
            pltpu.VMEM((2, 2 * H_A, LANES, tile), BF16),
        ],
        compiler_params=pltpu.CompilerParams(
            dimension_semantics=("arbitrary",), vmem_limit_bytes=VMEM_LIMIT),
        name="diff_attn_prompt",
    )(*lam_vecs, g_subln, qaug, k, v, ga)


def _sample_attn_kernel(lq1_ref, lk1_ref, lq2_ref, lk2_ref, gsub_ref, q_ref, kc_ref, vc_ref,
                        kn_ref, vn_ref, ga_ref, o_ref, *, lam_init):
    sq = q_ref.shape[0]
    past = kc_ref.shape[0]
    lam = _lambda_value(lq1_ref, lk1_ref, lq2_ref, lk2_ref, lam_init)
    lane = lax.broadcasted_iota(jnp.int32, (sq, LANES), 1)
    qpos_c = past + lax.broadcasted_iota(jnp.int32, (sq, past), 0)
    dist_c = jnp.abs(qpos_c - lax.broadcasted_iota(jnp.int32, (sq, past), 1)).astype(F32)
    dist_n = jnp.abs(lax.broadcasted_iota(jnp.int32, (sq, sq), 0)
                     - lax.broadcasted_iota(jnp.int32, (sq, sq), 1)).astype(F32)
    for hd in range(H_A):
        slope = ALIBI_SLOPES[hd]
        kc = kc_ref[:, hd, :].astype(BF16)
        kn = kn_ref[:, hd * DV_A:(hd + 1) * DV_A].astype(BF16)
        vc = vc_ref[:, hd, :].astype(BF16)
        vn = vn_ref[:, hd * DV_A:(hd + 1) * DV_A].astype(BF16)
        outs = []
        for m in range(2):
            q = q_ref[:, (2 * hd + m) * LANES:(2 * hd + m + 1) * LANES].astype(F32)
            if m == 0:
                q = jnp.where(lane < DK_A, q, 0.0)
            else:
                q = jnp.where(lane >= DK_A, pltpu.roll(q, DK_A, 1), 0.0)
            q = q.astype(BF16)
            s_c = _dot_nt(q, kc) - (slope * LOG2E) * dist_c
            s_n = _dot_nt(q, kn) - (slope * LOG2E) * dist_n
            mx = jnp.maximum(jnp.max(s_c, axis=-1, keepdims=True),
                             jnp.max(s_n, axis=-1, keepdims=True))
            p_c = jnp.exp2(s_c - mx)
            p_n = jnp.exp2(s_n - mx)
            l = jnp.sum(p_c, axis=-1, keepdims=True) + jnp.sum(p_n, axis=-1, keepdims=True)
            acc = _dot(p_c.astype(BF16), vc) + _dot(p_n.astype(BF16), vn)
            outs.append(acc * (1.0 / l))
        o = outs[0] - lam * outs[1]
        gate = ga_ref[:, hd * DV_A:(hd + 1) * DV_A]
        o_ref[:, hd * DV_A:(hd + 1) * DV_A] = (
            _subln(o, gsub_ref[...], lam_init) * gate).astype(BF16)


def _sample_attn(qaug, k_cache, v_cache, k_new, v_new, ga, lam_vecs, g_subln, lam_init):
    b, past = k_cache.shape[:2]
    sq = k_new.shape[1]
    per_b4 = lambda bi: (bi, 0, 0, 0)
    assert (past + sq - 1) // CHUNK <= past // CHUNK
    const = lambda bi: (0, 0)
    per_b = lambda bi: (bi, 0, 0)
    kern = functools.partial(_sample_attn_kernel, lam_init=lam_init)
    return pl.pallas_call(
        kern,
        grid=(b,),
        in_specs=[pl.BlockSpec((1, DK_A), const)] * 4 + [
            pl.BlockSpec((1, DV_A), const),
            pl.BlockSpec((None, sq, 2 * W_A), per_b),
            pl.BlockSpec((None, past, H_A, DV_A), per_b4),
            pl.BlockSpec((None, past, H_A, DV_A), per_b4),
            pl.BlockSpec((None, sq, W_A), per_b),
            pl.BlockSpec((None, sq, W_A), per_b),
            pl.BlockSpec((None, sq, W_A), per_b),
        ],
        out_specs=pl.BlockSpec((None, sq, W_A), per_b),
        out_shape=jax.ShapeDtypeStruct((b, sq, W_A), BF16),
        compiler_params=pltpu.CompilerParams(
            dimension_semantics=("arbitrary",), vmem_limit_bytes=VMEM_LIMIT),
        name="diff_attn_sample",
    )(*lam_vecs, g_subln, qaug, k_cache, v_cache, k_new, v_new, ga)


def _mlstm_kernel(mb_ref, gl_ref, og_ref, gmh_ref, c0_ref, n0_ref, m0_ref,
                  h_ref, c_ref, n_ref, m_ref, *, valid):
    c_idx = pl.program_id(1)
    L = CHUNK
    W = W_B
    G = mb_ref.shape[0]
    R = G * L

    @pl.when(c_idx == 0)
    def _load_state():
        c_ref[...] = c0_ref[...]
        n_ref[...] = n0_ref[...]
        m_ref[...] = m0_ref[...]

    lane_head = lax.broadcasted_iota(jnp.int32, (R, W), 1) // DH_B
    row_t = lax.broadcasted_iota(jnp.int32, (R, W), 0) % L
    lane_s = lax.broadcasted_iota(jnp.int32, (R, W), 1) % DH_B
    wi = lax.broadcasted_iota(jnp.int32, (W, W), 0)
    wj = lax.broadcasted_iota(jnp.int32, (W, W), 1)
    blk = wi // DH_B == wj // DH_B
    ones_bd = blk.astype(BF16)
    ei = lax.broadcasted_iota(jnp.int32, (LANES, 2 * W), 0)
    ej = lax.broadcasted_iota(jnp.int32, (LANES, 2 * W), 1)
    expand = (ei == ej // DH_B).astype(BF16)
    tri = (lax.broadcasted_iota(jnp.int32, (L, L), 1)
           <= lax.broadcasted_iota(jnp.int32, (L, L), 0)).astype(BF16)
    nar_lane = lax.broadcasted_iota(jnp.int32, (R, LANES), 1)
    nar_t = lax.broadcasted_iota(jnp.int32, (R, LANES), 0) % L

    def per_seq(x):
        return x.reshape(G, L, x.shape[-1])

    def stacked(x):
        return x.reshape(R, x.shape[-1])

    def seq_rows(x, g):
        return x[g * L:(g + 1) * L]

    def exact_matmul(xs, mat):
        n = xs[0].shape[0]
        pieces = [part for x in xs for part in _split3(x)]
        out = _dot(jnp.concatenate(pieces, axis=0), mat)
        return [out[(3 * i) * n:(3 * i + 1) * n] + out[(3 * i + 1) * n:(3 * i + 2) * n]
                + out[(3 * i + 2) * n:(3 * i + 3) * n] for i in range(len(xs))]

    def head_max(x):
        out = jnp.zeros((R, W), F32)
        for hd in range(H_B):
            mx = jnp.max(jnp.where(lane_head == hd, x, NEG_BIG), axis=-1, keepdims=True)
            out = jnp.where(lane_head == hd, mx, out)
        return out

    q = stacked(mb_ref[:, :, 0:W])
    k = stacked(mb_ref[:, :, W:2 * W])
    v = stacked(mb_ref[:, :, 2 * W:3 * W])
    gl = jnp.where(nar_lane < 2 * H_B, stacked(gl_ref[...]), 0.0)
    gl = jnp.where(nar_t < valid, gl, jnp.where(nar_lane < H_B, NEG_BIG, 0.0))
    cums = []
    for g in range(G):
        out = _dot(tri, jnp.concatenate(_split3(seq_rows(gl, g)), axis=1))
        cums.append(out[:, 0:LANES] + out[:, LANES:2 * LANES] + out[:, 2 * LANES:3 * LANES])
    narrow = jnp.where(nar_lane < H_B, gl, jnp.concatenate(cums, axis=0))
    (wide,) = exact_matmul([narrow], expand)
    ig_all = wide[:, 0:W]
    b_all = wide[:, W:2 * W]

    r_all = b_all - ig_all
    r_row = jnp.sum(per_seq(jnp.where(lane_s == row_t, r_all, 0.0)), axis=1, keepdims=True)
    m0_row = m_ref[...]
    n0_row = n_ref[...]
    b_seq = per_seq(b_all)
    d_mat = jnp.where(lane_s <= row_t, stacked(b_seq - r_row), NEG_BIG)
    g_all = stacked(b_seq + m0_row)
    m_all = jnp.maximum(g_all, head_max(d_mat))
    d_w = jnp.exp(d_mat - m_all)
    g_w = jnp.exp(g_all - m_all)

    q_bf = q.astype(BF16)
    zero_bf = jnp.zeros((W, W), BF16)
    scores, inter, v_bds = [], [], []
    for g in range(G):
        k_t = seq_rows(k, g).T.astype(BF16)
        k_bd = jnp.where(blk, jnp.concatenate([k_t] * H_B, axis=1), zero_bf)
        v_bds.append(jnp.where(
            blk, jnp.concatenate([seq_rows(v, g).astype(BF16)] * H_B, axis=0), zero_bf))
        c0_bd = jnp.where(blk, c_ref[g].astype(BF16), zero_bf)
        scores.append(_dot(seq_rows(q_bf, g), k_bd))
        inter.append(_dot(seq_rows(q_bf, g), c0_bd))
    a_mat = jnp.concatenate(scores, axis=0) * d_w
    a_bf = a_mat.astype(BF16)
    intra = [_dot(seq_rows(a_bf, g), v_bds[g]) for g in range(G)]
    num = g_w * jnp.concatenate(inter, axis=0) + jnp.concatenate(intra, axis=0)

    qn, a_sum = exact_matmul([stacked(per_seq(q) * n0_row), a_mat], ones_bd)
    den = g_w * qn + a_sum
    h = num / jnp.maximum(jnp.abs(den), jnp.exp(-m_all))
    h_ms = _dot((h * h).astype(BF16), ones_bd) * (1.0 / DH_B)
    h = h * lax.rsqrt(h_ms + EPS) * gmh_ref[...]
    og = stacked(og_ref[...])
    h_ref[...] = per_seq((h * og[:, 0:W] * og[:, W:2 * W]).astype(BF16))

    m_last = per_seq(m_all)[:, L - 1:L, :]
    b_last = b_seq[:, L - 1:L, :]
    w_s = jnp.exp(stacked(b_last - b_seq + per_seq(ig_all) - m_last))
    decay = jnp.exp(b_last + m0_row - m_last)
    kw = k * w_s
    v_bf = v.astype(BF16)
    for g in range(G):
        c_ref[g] = decay[g] * c_ref[g] + _dot(seq_rows(kw, g).T.astype(BF16), seq_rows(v_bf, g))
    n_ref[...] = decay * n0_row + jnp.sum(per_seq(kw), axis=1, keepdims=True)
    m_ref[...] = m_last


def _mlstm(mb, gl, og, g_mh_row, c0_bd, n0_row, m0_row, valid, n_seq):
    b, s, _ = mb.shape
    nc = s // CHUNK
    tok = lambda bi, ci: (bi, ci, 0)
    st = lambda bi, ci: (bi, 0, 0)
    kern = functools.partial(_mlstm_kernel, valid=valid)
    return pl.pallas_call(
        kern,
        grid=(b // n_seq, nc),
        in_specs=[
            pl.BlockSpec((n_seq, CHUNK, 3 * W_B), tok),
            pl.BlockSpec((n_seq, CHUNK, LANES), tok),
            pl.BlockSpec((n_seq, CHUNK, 2 * W_B), tok),
            pl.BlockSpec((1, W_B), lambda bi, ci: (0, 0)),
            pl.BlockSpec((n_seq, W_B, W_B), st),
            pl.BlockSpec((n_seq, 1, W_B), st),
            pl.BlockSpec((n_seq, 1, W_B), st),
        ],
        out_specs=[
            pl.BlockSpec((n_seq, CHUNK, W_B), tok),
            pl.BlockSpec((n_seq, W_B, W_B), st),
            pl.BlockSpec((n_seq, 1, W_B), st),
            pl.BlockSpec((n_seq, 1, W_B), st),
        ],
        out_shape=[
            jax.ShapeDtypeStruct((b, s, W_B), BF16),
            jax.ShapeDtypeStruct((b, W_B, W_B), F32),
            jax.ShapeDtypeStruct((b, 1, W_B), F32),
            jax.ShapeDtypeStruct((b, 1, W_B), F32),
        ],
        compiler_params=pltpu.CompilerParams(
            dimension_semantics=("arbitrary", "arbitrary"), vmem_limit_bytes=VMEM_LIMIT),
        name="mlstm",
    )(mb, gl, og, g_mh_row, c0_bd, n0_row, m0_row)


def _mix_out_kernel(x_ref, oa_ref, hb_ref, qm_ref, gm_ref, mk_ref, mvt_ref, wo_ref, y_ref,
                    *, sub_rows):
    tm = x_ref.shape[0]
    sub = min(tm, sub_rows)
    mk = mk_ref[...]
    mvt = mvt_ref[...]
    k_feat_head = lax.broadcasted_iota(jnp.int32, mk.shape, 1) // DH_M
    v_feat_head = lax.broadcasted_iota(jnp.int32, mvt.shape, 0) // DH_M
    k_of_head = [jnp.where(k_feat_head == hd, mk, 0.0).astype(BF16) for hd in range(H_M)]
    vt_of_head = [jnp.where(v_feat_head == hd, mvt, 0.0).astype(BF16) for hd in range(H_M)]

    def probs(s):
        p = jnp.exp2(s - jnp.max(s, axis=0, keepdims=True))
        return (p * (1.0 / jnp.sum(p, axis=0, keepdims=True))).astype(BF16)

    def sub_tile_stages(r0):
        rows = slice(r0, r0 + sub)
        qm = qm_ref[rows, :]
        if sub < LANES:
            qm = jnp.concatenate([qm, jnp.zeros((LANES - sub, W_M), BF16)], axis=0)
        s0 = _dot_nt(k_of_head[0], qm)
        y = x_ref[rows, :] + _dot(oa_ref[rows, :], wo_ref[0:W_A, :])
        yield
        s1 = _dot_nt(k_of_head[1], qm)
        p0 = probs(s0)
        y = y + _dot(hb_ref[rows, :], wo_ref[W_A:W_A + W_B, :])
        yield
        s2 = _dot_nt(k_of_head[2], qm)
        p1 = probs(s1)
        omt = _dot(vt_of_head[0], p0)
        yield
        s3 = _dot_nt(k_of_head[3], qm)
        p2 = probs(s2)
        omt = omt + _dot(vt_of_head[1], p1)
        yield
        p3 = probs(s3)
        omt = omt + _dot(vt_of_head[2], p2)
        omt = omt + _dot(vt_of_head[3], p3)
        yield
        om = omt.T[0:sub] * gm_ref[rows, :]
        y_ref[rows, :] = y + _dot(om.astype(BF16), wo_ref[W_A + W_B:D_MIX, :])
        yield

    n_stage = 6
    gens = [sub_tile_stages(r0) for r0 in range(0, tm, sub)]
    for step in range(n_stage + len(gens) - 1):
        for idx, gen in enumerate(gens):
            if 0 <= step - idx < n_stage:
                next(gen)


def _mix_out(x, oa, hb, qm, gm, mem_k, mem_vt, w_out_bf, tm):
    b, s, _ = x.shape
    tok = lambda bi, ti: (bi, ti, 0)
    per_b = lambda bi, ti: (bi, 0, 0)
    const = lambda bi, ti: (0, 0)
    return pl.pallas_call(
        functools.partial(_mix_out_kernel, sub_rows=MXU_DIM),
        grid=(b, s // tm),
        in_specs=[
            pl.BlockSpec((None, tm, D_MODEL), tok),
            pl.BlockSpec((None, tm, W_A), tok),
            pl.BlockSpec((None, tm, W_B), tok),
            pl.BlockSpec((None, tm, W_M), tok),
            pl.BlockSpec((None, tm, W_M), tok),
            pl.BlockSpec((None, N_MEM, W_M), per_b),
            pl.BlockSpec((None, N_MEM, W_M), per_b),
            pl.BlockSpec((D_MIX, D_MODEL), const),
        ],
        out_specs=pl.BlockSpec((None, tm, D_MODEL), tok),
        out_shape=jax.ShapeDtypeStruct((b, s, D_MODEL), F32),
        compiler_params=pltpu.CompilerParams(
            dimension_semantics=("arbitrary", "arbitrary"), vmem_limit_bytes=VMEM_LIMIT),
        name="mix_out",
    )(x, oa, hb, qm, gm, mem_k, mem_vt, w_out_bf)


def _block_diag_state(c):
    b = c.shape[0]
    eye = jnp.eye(H_B, dtype=c.dtype)
    return jnp.einsum('bhde,hg->bhdge', c, eye).reshape(b, W_B, W_B)


def _diag_blocks(c_bd):
    b = c_bd.shape[0]
    c5 = c_bd.reshape(b, H_B, DH_B, H_B, DH_B)
    return jnp.stack([c5[:, h, :, h, :] for h in range(H_B)], axis=1)


def _mixer_layer(x, past_kv, mlstm_state, mem_k, mem_vt, lam_init, p):
    b, s, _ = x.shape
    n = b * s
    tm = min(512, n)
    qaug, k_new, v_new, ga, mb, gl, og, qm, gm, k4, v4 = _in_proj(
        x.reshape(n, D_MODEL), tm, p["g_norm"], p["w_perm"], p["g_mat"],
        p["g_qa"], p["g_ka"], p["g_qm"], p["b_if"])
    k_new = k_new.reshape(b, s, W_A)
    v_new = v_new.reshape(b, s, W_A)
    qaug = qaug.reshape(b, s, 2 * W_A)
    ga = ga.reshape(b, s, W_A)

    if past_kv is None:
        oa = _prompt_attn(qaug, k_new, v_new, ga, p["lam_vecs"], p["g_subln"], lam_init,
                          tile=256)
    else:
        oa = _sample_attn(qaug, past_kv[0], past_kv[1], k_new, v_new, ga,
                          p["lam_vecs"], p["g_subln"], lam_init)

    c0, n0, m0 = mlstm_state
    s_pad = -(-s // CHUNK) * CHUNK
    mb3 = mb.reshape(b, s, 3 * W_B)
    gl3 = gl.reshape(b, s, LANES)
    og3 = og.reshape(b, s, 2 * W_B)
    if s_pad != s:
        pad = ((0, 0), (0, s_pad - s), (0, 0))
        mb3, gl3, og3 = jnp.pad(mb3, pad), jnp.pad(gl3, pad), jnp.pad(og3, pad)
    valid = CHUNK if s_pad == s else s
    hb, c_bd, n_row, m_row = _mlstm(
        mb3, gl3, og3, p["g_mh"], _block_diag_state(c0), n0.reshape(b, 1, W_B),
        jnp.repeat(m0, DH_B, axis=-1).reshape(b, 1, W_B), valid, n_seq=8)
    hb = hb[:, :s]
    new_state = (_diag_blocks(c_bd), n_row.reshape(b, H_B, DH_B),
                 m_row.reshape(b, H_B, DH_B)[:, :, 0])

    y = _mix_out(x, oa, hb, qm.reshape(b, s, W_M), gm.reshape(b, s, W_M), mem_k, mem_vt,
                 p["w_out"], tm=min(1024, s))
    return (y, k4.reshape(b, s, H_A, DV_A), v4.reshape(b, s, H_A, DV_A), new_state)


def kernel(x_prompt, x_sample, cache_attn_k, cache_attn_v, state_mlstm_C, state_mlstm_n,
           state_mlstm_m, cache_mem_k, cache_mem_v, mem_prompt, g_norm, w_in, w_out, g_qa,
           g_ka, lam_q1, lam_k1, lam_q2, lam_k2, g_subln, b_i, b_f, g_mh, g_qm, g_km, g_mem,
           w_mk, w_mv):
    depth = w_in.shape[0]
    bp = x_prompt.shape[0]
    bs = x_sample.shape[0]
    past = cache_attn_k.shape[2]
    gi = lax.broadcasted_iota(jnp.int32, (MXU_DIM, MXU_DIM), 0) // DH_B
    gj = lax.broadcasted_iota(jnp.int32, (MXU_DIM, MXU_DIM), 1) // DH_B
    g_mat = jnp.where(gi == gj, 1.0 / DH_B, 0.0).astype(BF16)

    xp, xs = x_prompt, x_sample
    outs = {name: [] for name in ("pk", "pv", "pC", "pn", "pm", "pmk", "pmv",
                                  "sk", "sv", "sC", "sn", "sm")}
    for l in range(depth):
        lam_init = 0.8 - 0.6 * math.exp(-0.3 * l)
        w = w_in[l]
        w_perm = jnp.concatenate(
            [w[:, :4 * W_A + 4 * W_B], w[:, 4 * W_A + 4 * W_B + 2 * H_B:],
             w[:, 4 * W_A + 4 * W_B:4 * W_A + 4 * W_B + 2 * H_B],
             jnp.zeros((D_MODEL, LANES - 2 * H_B), F32)], axis=1).astype(BF16)
        p = {
            "g_norm": g_norm[l].reshape(1, D_MODEL),
            "w_perm": w_perm,
            "g_mat": g_mat,
            "g_qa": jnp.tile(g_qa[l], 2 * H_A).reshape(1, W_A),
            "g_ka": jnp.tile(g_ka[l], 2 * H_A).reshape(1, W_A),
            "g_qm": jnp.tile(g_qm[l], H_M).reshape(1, W_M),
            "b_if": jnp.concatenate(
                [b_i[l], b_f[l], jnp.zeros((LANES - 2 * H_B,), F32)]).reshape(1, LANES),
            "lam_vecs": tuple(v[l].reshape(1, DK_A) for v in (lam_q1, lam_k1, lam_q2, lam_k2)),
            "g_subln": g_subln[l].reshape(1, DV_A),
            "g_mh": jnp.tile(g_mh[l], H_B).reshape(1, W_B),
            "w_out": w_out[l].astype(BF16),
        }
        mk, mkt, mvt = _memory_kv(mem_prompt, g_mem[l], w_mk[l], w_mv[l], g_km[l], g_mat)
        zero_state = (jnp.zeros((bp, H_B, DH_B, DH_B), F32), jnp.zeros((bp, H_B, DH_B), F32),
                      jnp.zeros((bp, H_B), F32))
        xp, k_p, v_p, st_p = _mixer_layer(xp, None, zero_state, mk, mvt, lam_init, p)
        xs, k_s, v_s, st_s = _mixer_layer(
            xs, (cache_attn_k[l], cache_attn_v[l]),
            (state_mlstm_C[l], state_mlstm_n[l], state_mlstm_m[l]),
            cache_mem_k[l].reshape(bs, N_MEM, W_M),
            cache_mem_v[l].transpose(0, 2, 3, 1).reshape(bs, W_M, N_MEM),
            lam_init, p)

        def tokens_first(t):
            return t.reshape(bp, H_M, DH_M, N_MEM).transpose(0, 3, 1, 2)

        outs["pk"].append(k_p); outs["pv"].append(v_p)
        outs["pC"].append(st_p[0]); outs["pn"].append(st_p[1]); outs["pm"].append(st_p[2])
        outs["pmk"].append(tokens_first(mkt))
        outs["pmv"].append(tokens_first(mvt))
        outs["sk"].append(k_s); outs["sv"].append(v_s)
        outs["sC"].append(st_s[0]); outs["sn"].append(st_s[1]); outs["sm"].append(st_s[2])
    stk = {name: jnp.stack(vals) for name, vals in outs.items()}
    return (xp, xs, stk["pk"], stk["pv"], stk["pC"], stk["pn"], stk["pm"], stk["pmk"],
            stk["pmv"], stk["sk"], stk["sv"], stk["sC"], stk["sn"], stk["sm"])
```

```python
import functools
import math

import jax
import jax.numpy as jnp
import numpy as np
from jax import lax
from jax.experimental import pallas as pl
from jax.experimental.pallas import tpu as pltpu

F32 = jnp.float32
BF16 = jnp.bfloat16

D_MODEL = 1024
CHUNK = 64
N_MEM = 256
H_A, DK_A = 4, 64
DV_A = 2 * DK_A
W_A = H_A * DV_A
H_B, DH_B = 4, 64
W_B = H_B * DH_B
H_M, DH_M = 4, 64
W_M = H_M * DH_M
D_MIX = W_A + W_B + W_M
EPS = 1e-6
ALIBI_SLOPES = tuple(2.0 ** (-8.0 * (h + 1) / H_A) for h in range(H_A))
N_IN = 4 * W_A + 5 * W_B + 2 * H_B + 2 * W_M

LANES = 128
MXU_DIM = 256
BF16_SUBLANES = 16
VT_ROWS = DV_A + BF16_SUBLANES
NEG_BIG = -1e30
LOG2E = math.log2(math.e)


def _bf16_pieces(value, n=3):
    pieces, rest = [], np.float32(value)
    for _ in range(n):
        piece = np.float32(rest.astype(jnp.bfloat16))
        pieces.append(float(piece))
        rest = np.float32(rest - piece)
    return tuple(pieces)


LOG2E_BF16_PIECES = _bf16_pieces(LOG2E)

C_QA, C_KA, C_VA, C_GA = 0, W_A, 2 * W_A, 3 * W_A
C_QB = 4 * W_A
C_KB, C_VB, C_OB, C_GB = C_QB + W_B, C_QB + 2 * W_B, C_QB + 3 * W_B, C_QB + 4 * W_B
C_QM = C_GB + W_B
C_GM = C_QM + W_M
C_IF = C_GM + W_M
N_IN_PAD = C_IF + LANES

VMEM_LIMIT = 56 * 1024 * 1024


def _dot(a, b):
    return jnp.dot(a, b, preferred_element_type=F32)


def _dot_nt(a, b):
    return lax.dot_general(a, b, (((1,), (1,)), ((), ())), preferred_element_type=F32)


def _split3(x):
    hi = x.astype(BF16)
    r1 = x - hi.astype(F32)
    mid = r1.astype(BF16)
    lo = (r1 - mid.astype(F32)).astype(BF16)
    return hi, mid, lo


def _group_mean_sq(z, g_mat):
    zz = (z * z).astype(BF16)
    parts = [_dot(zz[:, c:c + MXU_DIM], g_mat) for c in range(0, z.shape[1], MXU_DIM)]
    return parts[0] if len(parts) == 1 else jnp.concatenate(parts, axis=1)


def _group_rms_norm(z, gain_row, g_mat):
    return z * lax.rsqrt(_group_mean_sq(z, g_mat) + EPS) * gain_row


def _log_sigmoid(u):
    return -(jnp.maximum(-u, 0.0) + jnp.log1p(jnp.exp(-jnp.abs(u))))


def _silu(u):
    return u * jax.nn.sigmoid(u)


def _memory_kv_kernel(mem_ref, gmem_ref, wk_ref, wv_ref, gkm_ref, gmat_ref,
                      mk_ref, mkt_ref, mvt_ref, mv_scr):
    x = mem_ref[...]
    ms = jnp.mean(x * x, axis=-1, keepdims=True)
    hm = (x * lax.rsqrt(ms + EPS) * gmem_ref[...]).astype(BF16)
    mk_ref[...] = _group_rms_norm(_dot(hm, wk_ref[...]), gkm_ref[...], gmat_ref[...])
    mv_scr[...] = _dot(hm, wv_ref[...])
    mkt_ref[...] = mk_ref[...].T
    mvt_ref[...] = mv_scr[...].T


def _memory_kv(mem, g_mem, w_mk, w_mv, g_km, g_mat):
    b, n, d = mem.shape
    row = lambda i: (i, 0, 0)
    const2 = lambda i: (0, 0)
    return pl.pallas_call(
        _memory_kv_kernel,
        grid=(b,),
        in_specs=[
            pl.BlockSpec((None, n, d), row),
            pl.BlockSpec((1, d), const2),
            pl.BlockSpec((d, W_M), const2),
            pl.BlockSpec((d, W_M), const2),
            pl.BlockSpec((1, W_M), const2),
            pl.BlockSpec((MXU_DIM, MXU_DIM), const2),
        ],
        out_specs=[pl.BlockSpec((None, n, W_M), row), pl.BlockSpec((None, W_M, n), row),
                   pl.BlockSpec((None, W_M, n), row)],
        out_shape=[jax.ShapeDtypeStruct((b, n, W_M), F32),
                   jax.ShapeDtypeStruct((b, W_M, n), F32),
                   jax.ShapeDtypeStruct((b, W_M, n), F32)],
        scratch_shapes=[pltpu.VMEM((n, W_M), F32)],
        compiler_params=pltpu.CompilerParams(
            dimension_semantics=("arbitrary",), vmem_limit_bytes=VMEM_LIMIT),
        name="memory_kv",
    )(mem, g_mem.reshape(1, d), w_mk.astype(BF16), w_mv.astype(BF16),
      jnp.tile(g_km, H_M).reshape(1, W_M), g_mat)


def _in_proj_kernel(x_ref, gn_ref, w_ref, gmat_ref, gqa_ref, gka_ref, gqm_ref, bif_ref,
                    qaug_ref, k_ref, v_ref, ga_ref, mb_ref, gl_ref, og_ref, qm_ref, gm_ref,
                    k4_ref, v4_ref):
    x = x_ref[...]
    h = (x * gn_ref[...]).astype(BF16)
    inv_rms = lax.rsqrt(jnp.mean(x * x, axis=-1, keepdims=True) + EPS)
    g_mat = gmat_ref[...]

    def proj(start, width):
        return _dot(h, w_ref[:, start:start + width]) * inv_rms

    qn = _group_rms_norm(proj(C_QA, W_A), gqa_ref[...], g_mat) * (DK_A ** -0.5 * LOG2E)
    lane = lax.broadcasted_iota(jnp.int32, (x.shape[0], LANES), 1)
    aug = jnp.zeros((x.shape[0], LANES), F32)
    for rep in range(2):
        for piece, val in enumerate(LOG2E_BF16_PIECES):
            aug = jnp.where(lane == DK_A + 3 * rep + piece, val, aug)
    for hd in range(H_A):
        slab = qn[:, hd * DV_A:(hd + 1) * DV_A]
        q1 = jnp.where(lane < DK_A, slab, aug)
        q2 = jnp.where(lane < DK_A, pltpu.roll(slab, DK_A, 1), aug)
        qaug_ref[:, (2 * hd) * LANES:(2 * hd + 1) * LANES] = q1.astype(BF16)
        qaug_ref[:, (2 * hd + 1) * LANES:(2 * hd + 2) * LANES] = q2.astype(BF16)

    kn = _group_rms_norm(proj(C_KA, W_A), gka_ref[...], g_mat)
    k_ref[...] = kn.astype(BF16)

    def cache_copy(dst_ref, src, hd, later):
        zero = pltpu.bitcast(
            lax.shift_right_logical(pltpu.bitcast(later[:, 0:DV_A], jnp.uint32), jnp.uint32(32)),
            F32)
        dst_ref[:, hd, :] = src[:, hd * DV_A:(hd + 1) * DV_A] + zero

    vn = proj(C_VA, W_A)
    v_ref[...] = vn.astype(BF16)
    cache_copy(k4_ref, kn, 0, vn)
    z = proj(C_GA, W_A)
    ga_ref[...] = _silu(z)
    cache_copy(k4_ref, kn, 1, z)
    z = proj(C_QB, W_B)
    mb_ref[:, 0:W_B] = z
    cache_copy(k4_ref, kn, 2, z)
    z = proj(C_KB, W_B)
    mb_ref[:, W_B:2 * W_B] = z * (DH_B ** -0.5)
    cache_copy(k4_ref, kn, 3, z)
    z = proj(C_VB, W_B)
    mb_ref[:, 2 * W_B:3 * W_B] = z
    cache_copy(v4_ref, vn, 0, z)
    z = proj(C_OB, W_B)
    og_ref[:, 0:W_B] = jax.nn.sigmoid(z)
    cache_copy(v4_ref, vn, 1, z)
    z = proj(C_GB, W_B)
    og_ref[:, W_B:2 * W_B] = _silu(z)
    cache_copy(v4_ref, vn, 2, z)
    z = proj(C_QM, W_M)
    qm_ref[...] = (_group_rms_norm(z, gqm_ref[...], g_mat)
                   * (DH_M ** -0.5 * LOG2E)).astype(BF16)
    cache_copy(v4_ref, vn, 3, z)
    gm_ref[...] = _silu(proj(C_GM, W_M))

    u = proj(C_IF, LANES) + bif_ref[...]
    gl_ref[...] = jnp.where(lane < H_B, u, _log_sigmoid(u))


def _in_proj(x2d, tm, g_norm, w_perm, g_mat, g_qa, g_ka, g_qm, b_if):
    n = x2d.shape[0]
    row = lambda i: (i, 0)
    const = lambda i: (0, 0)
    widths = (2 * W_A, W_A, W_A, W_A, 3 * W_B, LANES, 2 * W_B, W_M, W_M)
    dtypes = (BF16, BF16, BF16) + (F32,) * 4 + (BF16, F32)
    return pl.pallas_call(
        _in_proj_kernel,
        grid=(n // tm,),
        in_specs=[
            pl.BlockSpec((tm, D_MODEL), row),
            pl.BlockSpec((1, D_MODEL), const),
            pl.BlockSpec((D_MODEL, N_IN_PAD), const),
            pl.BlockSpec((MXU_DIM, MXU_DIM), const),
            pl.BlockSpec((1, W_A), const),
            pl.BlockSpec((1, W_A), const),
            pl.BlockSpec((1, W_M), const),
            pl.BlockSpec((1, LANES), const),
        ],
        out_specs=[pl.BlockSpec((tm, w), row) for w in widths]
        + [pl.BlockSpec((tm, H_A, DV_A), lambda i: (i, 0, 0))] * 2,
        out_shape=[jax.ShapeDtypeStruct((n, w), dt) for w, dt in zip(widths, dtypes)]
        + [jax.ShapeDtypeStruct((n, H_A, DV_A), F32)] * 2,
        compiler_params=pltpu.CompilerParams(
            dimension_semantics=("arbitrary",), vmem_limit_bytes=VMEM_LIMIT),
        name="in_proj",
    )(x2d, g_norm, w_perm, g_mat, g_qa, g_ka, g_qm, b_if)


def _lambda_value(lq1_ref, lk1_ref, lq2_ref, lk2_ref, lam_init):
    s1 = jnp.sum(lq1_ref[...] * lk1_ref[...], axis=-1, keepdims=True)
    s2 = jnp.sum(lq2_ref[...] * lk2_ref[...], axis=-1, keepdims=True)
    return jnp.exp(s1) - jnp.exp(s2) + lam_init


def _subln(o, gsub_row, lam_init):
    ms = jnp.mean(o * o, axis=-1, keepdims=True)
    return o * lax.rsqrt(ms + EPS) * gsub_row * (1.0 - lam_init)


def _prompt_attn_kernel(lq1_ref, lk1_ref, lq2_ref, lk2_ref, gsub_ref, q_ref, k_ref, v_ref,
                        ga_ref, o_ref, kaug_ref, vt_ref, m_ref, acc_ref, s_ref, qt_ref,
                        *, tile, lam_init):
    seq = k_ref.shape[0]
    n_tiles = seq // tile
    n_chain = 2 * H_A

    lane = lax.broadcasted_iota(jnp.int32, (seq, LANES), 1)
    pos = lax.broadcasted_iota(jnp.int32, (seq, LANES), 0)
    in_tile = (pos % tile).astype(F32)
    tile_base = (pos - pos % tile).astype(F32)
    for hd in range(H_A):
        slab = k_ref[:, hd * DV_A:(hd + 1) * DV_A].astype(F32)
        aug = jnp.where((lane >= DK_A) & (lane < DK_A + 3), ALIBI_SLOPES[hd] * in_tile,
                        jnp.where((lane >= DK_A + 3) & (lane < DK_A + 6),
                                  ALIBI_SLOPES[hd] * tile_base, 0.0))
        kaug_ref[2 * hd] = jnp.where(lane < DK_A, slab, aug).astype(BF16)
        kaug_ref[2 * hd + 1] = jnp.where(
            lane < DK_A, pltpu.roll(slab, DK_A, 1), aug).astype(BF16)
    for t in range(n_tiles):
        v_t = v_ref[t * tile:(t + 1) * tile, :].astype(F32).T.astype(BF16)
        for hd in range(H_A):
            vt_ref[t, hd, 0:DV_A, :] = v_t[hd * DV_A:(hd + 1) * DV_A, :]
            vt_ref[t, hd, DV_A:VT_ROWS, :] = jnp.ones((VT_ROWS - DV_A, tile), BF16)

    lam = _lambda_value(lq1_ref, lk1_ref, lq2_ref, lk2_ref, lam_init)

    key = lax.broadcasted_iota(jnp.int32, (tile, tile), 0)
    qry = lax.broadcasted_iota(jnp.int32, (tile, tile), 1)
    ahead = (key - qry).astype(F32)
    diag_bias = [jnp.where(key // CHUNK > qry // CHUNK, NEG_BIG,
                           jnp.where(key > qry, (-2.0 * LOG2E * ALIBI_SLOPES[hd]) * ahead, 0.0))
                 for hd in range(H_A)]

    def rows(i):
        if isinstance(i, int):
            return pl.ds(i * tile, tile)
        return pl.ds(pl.multiple_of(i * tile, tile), tile)

    def scores(par, c, h, j):
        s_ref[par, c, h] = _dot(kaug_ref[c, rows(j), :], qt_ref[par, c, h])

    def softmax_values(par, c, h, j, diagonal):
        s = s_ref[par, c, h]
        if diagonal:
            s = s + diag_bias[c // 2]
        mx = m_ref[par, c, h]
        mx_new = jnp.maximum(mx, jnp.max(s, axis=0, keepdims=True))
        p = jnp.exp2(s - mx_new)
        alpha = jnp.exp2(mx - mx_new)
        m_ref[par, c, h] = mx_new
        acc_ref[par, c, h] = (alpha * acc_ref[par, c, h]
                              + _dot(vt_ref[j, c // 2], p.astype(BF16)))

    def begin_chain(par, c, duo):
        for h in range(2):
            q = q_ref[rows(2 * duo + h), c * LANES:(c + 1) * LANES]
            qt_ref[par, c, h] = q.astype(F32).T.astype(BF16)
            m_ref[par, c, h] = jnp.full((1, tile), NEG_BIG, F32)
            acc_ref[par, c, h] = jnp.zeros((VT_ROWS, tile), F32)
            scores(par, c, h, 0)

    def finish_head(par, hd, duo):
        for h in range(2):
            a1 = acc_ref[par, 2 * hd, h]
            a2 = acc_ref[par, 2 * hd + 1, h]
            o1 = a1[0:DV_A] * (1.0 / a1[DV_A:DV_A + 1])
            o2 = a2[0:DV_A] * (1.0 / a2[DV_A:DV_A + 1])
            o = (o1 - lam * o2).T
            gate = ga_ref[rows(2 * duo + h), hd * DV_A:(hd + 1) * DV_A]
            o_ref[rows(2 * duo + h), hd * DV_A:(hd + 1) * DV_A] = (
                _subln(o, gsub_ref[...], lam_init) * gate).astype(BF16)

    for c in range(n_chain):
        begin_chain(0, c, 0)

    def duo_pair(pair, carry):
        for par in range(2):
            duo = 2 * pair + par
            first_diag = 2 * duo

            def past_tile(j, inner, par=par):
                for c in range(n_chain):
                    for h in range(2):
                        softmax_values(par, c, h, j, False)
                        scores(par, c, h, j + 1)
                return inner

            lax.fori_loop(0, first_diag, past_tile, 0)
            for c in range(n_chain):
                softmax_values(par, c, 0, first_diag, True)
                softmax_values(par, c, 1, first_diag, False)
                scores(par, c, 1, first_diag + 1)
            nxt = lax.rem(duo + 1, n_tiles // 2)
            for c in range(n_chain):
                softmax_values(par, c, 1, first_diag + 1, True)
                begin_chain(1 - par, c, nxt)
                if c % 2 == 1:
                    finish_head(par, c // 2, duo)
        return carry

    lax.fori_loop(0, n_tiles // 4, duo_pair, 0)


def _prompt_attn(qaug, k, v, ga, lam_vecs, g_subln, lam_init, tile):
    b, s, _ = k.shape
    assert (s // tile) % 4 == 0
    const = lambda bi: (0, 0)
    per_b = lambda bi: (bi, 0, 0)
    kern = functools.partial(_prompt_attn_kernel, tile=tile, lam_init=lam_init)
    return pl.pallas_call(
        kern,
        grid=(b,),
        in_specs=[pl.BlockSpec((1, DK_A), const)] * 4 + [
            pl.BlockSpec((1, DV_A), const),
            pl.BlockSpec((None, s, 2 * W_A), per_b),
            pl.BlockSpec((None, s, W_A), per_b),
            pl.BlockSpec((None, s, W_A), per_b),
            pl.BlockSpec((None, s, W_A), per_b),
        ],
        out_specs=pl.BlockSpec((None, s, W_A), per_b),
        out_shape=jax.ShapeDtypeStruct((b, s, W_A), BF16),
        scratch_shapes=[
            pltpu.VMEM((2 * H_A, s, LANES), BF16),
            pltpu.VMEM((s // tile, H_A, VT_ROWS, tile), BF16),
            pltpu.VMEM((2, 2 * H_A, 2, 1, tile), F32),
            pltpu.VMEM((2, 2 * H_A, 2, VT_ROWS, tile), F32),
            pltpu.VMEM((2, 2 * H_A, 2, tile, tile), F32),
            pltpu.VMEM((2, 2 * H_A, 2, LANES, tile), BF16),
        ],
        compiler_params=pltpu.CompilerParams(
            dimension_semantics=("arbitrary",), vmem_limit_bytes=VMEM_LIMIT),
        name="diff_attn_prompt",
    )(*lam_vecs, g_subln, qaug, k, v, ga)


def _sample_attn_kernel(lq1_ref, lk1_ref, lq2_ref, lk2_ref, gsub_ref, q_ref, kc_ref, vc_ref,
                        kn_ref, vn_ref, ga_ref, o_ref, *, lam_init):
    hd = pl.program_id(1)
    sq = q_ref.shape[0]
    past = kc_ref.shape[0] // H_A
    lam = _lambda_value(lq1_ref, lk1_ref, lq2_ref, lk2_ref, lam_init)
    lane = lax.broadcasted_iota(jnp.int32, (sq, LANES), 1)
    qpos_c = past + lax.broadcasted_iota(jnp.int32, (sq, past), 0)
    dist_c = jnp.abs(qpos_c - lax.broadcasted_iota(jnp.int32, (sq, past), 1)).astype(F32)
    dist_n = jnp.abs(lax.broadcasted_iota(jnp.int32, (sq, sq), 0)
                     - lax.broadcasted_iota(jnp.int32, (sq, sq), 1)).astype(F32)
    slope = jnp.float32(ALIBI_SLOPES[H_A - 1])
    for h in range(H_A - 1):
        slope = jnp.where(hd == h, jnp.float32(ALIBI_SLOPES[h]), slope)
    kc = kc_ref[pl.ds(hd, past, stride=H_A), :].astype(BF16)
    kn = kn_ref[...]
    vc = vc_ref[pl.ds(hd, past, stride=H_A), :].astype(BF16)
    vn = vn_ref[...]
    outs = []
    for m in range(2):
        q = q_ref[:, m * LANES:(m + 1) * LANES].astype(F32)
        if m == 0:
            q = jnp.where(lane < DK_A, q, 0.0)
        else:
            q = jnp.where(lane >= DK_A, pltpu.roll(q, DK_A, 1), 0.0)
        q = q.astype(BF16)
        s_c = _dot_nt(q, kc) - (slope * LOG2E) * dist_c
        s_n = _dot_nt(q, kn) - (slope * LOG2E) * dist_n
        mx = jnp.maximum(jnp.max(s_c, axis=-1, keepdims=True),
                         jnp.max(s_n, axis=-1, keepdims=True))
        p_c = jnp.exp2(s_c - mx)
        p_n = jnp.exp2(s_n - mx)
        l = jnp.sum(p_c, axis=-1, keepdims=True) + jnp.sum(p_n, axis=-1, keepdims=True)
        acc = _dot(p_c.astype(BF16), vc) + _dot(p_n.astype(BF16), vn)
        outs.append(acc * (1.0 / l))
    o = outs[0] - lam * outs[1]
    o_ref[...] = (_subln(o, gsub_ref[...], lam_init) * ga_ref[...]).astype(BF16)


def _sample_attn(qaug, k_cache, v_cache, k_new, v_new, ga, lam_vecs, g_subln, lam_init):
    b, past = k_cache.shape[:2]
    sq = k_new.shape[1]
    assert (past + sq - 1) // CHUNK <= past // CHUNK
    k_cache = k_cache.reshape(b, past * H_A, DV_A)
    v_cache = v_cache.reshape(b, past * H_A, DV_A)
    const = lambda bi, hi: (0, 0)
    head = lambda bi, hi: (bi, 0, hi)
    cache = lambda bi, hi: (bi, 0, 0)
    kern = functools.partial(_sample_attn_kernel, lam_init=lam_init)
    return pl.pallas_call(
        kern,
        grid=(b, H_A),
        in_specs=[pl.BlockSpec((1, DK_A), const)] * 4 + [
            pl.BlockSpec((1, DV_A), const),
            pl.BlockSpec((None, sq, 2 * LANES), head),
            pl.BlockSpec((None, past * H_A, DV_A), cache),
            pl.BlockSpec((None, past * H_A, DV_A), cache),
            pl.BlockSpec((None, sq, DV_A), head),
            pl.BlockSpec((None, sq, DV_A), head),
            pl.BlockSpec((None, sq, DV_A), head),
        ],
        out_specs=pl.BlockSpec((None, sq, DV_A), head),
        out_shape=jax.ShapeDtypeStruct((b, sq, W_A), BF16),
        compiler_params=pltpu.CompilerParams(
            dimension_semantics=("arbitrary", "arbitrary"), vmem_limit_bytes=VMEM_LIMIT),
        name="diff_attn_sample",
    )(*lam_vecs, g_subln, qaug, k_cache, v_cache, k_new, v_new, ga)


def _mlstm_kernel(mb_ref, gl_ref, og_ref, gmh_ref, c0_ref, n0_ref, m0_ref,
                  h_ref, c_ref, n_ref, m_ref, *, valid):
    c_idx = pl.program_id(1)
    L = CHUNK
    W = W_B
    G = mb_ref.shape[0]
    R = G * L

    @pl.when(c_idx == 0)
    def _load_state():
        c_ref[...] = c0_ref[...]
        n_ref[...] = n0_ref[...]
        m_ref[...] = m0_ref[...]

    lane_head = lax.broadcasted_iota(jnp.int32, (R, W), 1) // DH_B
    row_t = lax.broadcasted_iota(jnp.int32, (R, W), 0) % L
    lane_s = lax.broadcasted_iota(jnp.int32, (R, W), 1) % DH_B
    wi = lax.broadcasted_iota(jnp.int32, (W, W), 0)
    wj = lax.broadcasted_iota(jnp.int32, (W, W), 1)
    blk = wi // DH_B == wj // DH_B
    ones_bd = blk.astype(BF16)
    ei = lax.broadcasted_iota(jnp.int32, (LANES, 2 * W), 0)
    ej = lax.broadcasted_iota(jnp.int32, (LANES, 2 * W), 1)
    expand = (ei == ej // DH_B).astype(BF16)
    tri = (lax.broadcasted_iota(jnp.int32, (L, L), 1)
           <= lax.broadcasted_iota(jnp.int32, (L, L), 0)).astype(BF16)
    nar_lane = lax.broadcasted_iota(jnp.int32, (R, LANES), 1)
    nar_t = lax.broadcasted_iota(jnp.int32, (R, LANES), 0) % L

    def per_seq(x):
        return x.reshape(G, L, x.shape[-1])

    def stacked(x):
        return x.reshape(R, x.shape[-1])

    def seq_rows(x, g):
        return x[g * L:(g + 1) * L]

    def exact_matmul(xs, mat):
        n = xs[0].shape[0]
        pieces = [part for x in xs for part in _split3(x)]
        out = _dot(jnp.concatenate(pieces, axis=0), mat)
        return [out[(3 * i) * n:(3 * i + 1) * n] + out[(3 * i + 1) * n:(3 * i + 2) * n]
                + out[(3 * i + 2) * n:(3 * i + 3) * n] for i in range(len(xs))]

    def head_max(x):
        out = jnp.zeros((R, W), F32)
        for hd in range(H_B):
            mx = jnp.max(jnp.where(lane_head == hd, x, NEG_BIG), axis=-1, keepdims=True)
            out = jnp.where(lane_head == hd, mx, out)
        return out

    q = stacked(mb_ref[:, :, 0:W])
    k = stacked(mb_ref[:, :, W:2 * W])
    v = stacked(mb_ref[:, :, 2 * W:3 * W])
    gl = jnp.where(nar_lane < 2 * H_B, stacked(gl_ref[...]), 0.0)
    gl = jnp.where(nar_t < valid, gl, jnp.where(nar_lane < H_B, NEG_BIG, 0.0))
    cums = []
    for g in range(G):
        out = _dot(tri, jnp.concatenate(_split3(seq_rows(gl, g)), axis=1))
        cums.append(out[:, 0:LANES] + out[:, LANES:2 * LANES] + out[:, 2 * LANES:3 * LANES])
    narrow = jnp.where(nar_lane < H_B, gl, jnp.concatenate(cums, axis=0))
    (wide,) = exact_matmul([narrow], expand)
    ig_all = wide[:, 0:W]
    b_all = wide[:, W:2 * W]

    r_all = b_all - ig_all
    r_row = jnp.sum(per_seq(jnp.where(lane_s == row_t, r_all, 0.0)), axis=1, keepdims=True)
    m0_row = m_ref[...]
    n0_row = n_ref[...]
    b_seq = per_seq(b_all)
    d_mat = jnp.where(lane_s <= row_t, stacked(b_seq - r_row), NEG_BIG)
    g_all = stacked(b_seq + m0_row)
    m_all = jnp.maximum(g_all, head_max(d_mat))
    d_w = jnp.exp(d_mat - m_all)
    g_w = jnp.exp(g_all - m_all)

    q_bf = q.astype(BF16)
    zero_bf = jnp.zeros((W, W), BF16)
    scores, inter, v_bds = [], [], []
    for g in range(G):
        k_t = seq_rows(k, g).T.astype(BF16)
        k_bd = jnp.where(blk, jnp.concatenate([k_t] * H_B, axis=1), zero_bf)
        v_bds.append(jnp.where(
            blk, jnp.concatenate([seq_rows(v, g).astype(BF16)] * H_B, axis=0), zero_bf))
        c0_bd = jnp.where(blk, c_ref[g].astype(BF16), zero_bf)
        scores.append(_dot(seq_rows(q_bf, g), k_bd))
        inter.append(_dot(seq_rows(q_bf, g), c0_bd))
    a_mat = jnp.concatenate(scores, axis=0) * d_w
    a_bf = a_mat.astype(BF16)
    intra = [_dot(seq_rows(a_bf, g), v_bds[g]) for g in range(G)]
    num = g_w * jnp.concatenate(inter, axis=0) + jnp.concatenate(intra, axis=0)

    qn, a_sum = exact_matmul([stacked(per_seq(q) * n0_row), a_mat], ones_bd)
    den = g_w * qn + a_sum
    h = num / jnp.maximum(jnp.abs(den), jnp.exp(-m_all))
    h_ms = _dot((h * h).astype(BF16), ones_bd) * (1.0 / DH_B)
    h = h * lax.rsqrt(h_ms + EPS) * gmh_ref[...]
    og = stacked(og_ref[...])
    h_ref[...] = per_seq((h * og[:, 0:W] * og[:, W:2 * W]).astype(BF16))

    m_last = per_seq(m_all)[:, L - 1:L, :]
    b_last = b_seq[:, L - 1:L, :]
    w_s = jnp.exp(stacked(b_last - b_seq + per_seq(ig_all) - m_last))
    decay = jnp.exp(b_last + m0_row - m_last)
    kw = k * w_s
    v_bf = v.astype(BF16)
    for g in range(G):
        c_ref[g] = decay[g] * c_ref[g] + _dot(seq_rows(kw, g).T.astype(BF16), seq_rows(v_bf, g))
    n_ref[...] = decay * n0_row + jnp.sum(per_seq(kw), axis=1, keepdims=True)
    m_ref[...] = m_last


def _mlstm(mb, gl, og, g_mh_row, c0_bd, n0_row, m0_row, valid, n_seq):
    b, s, _ = mb.shape
    nc = s // CHUNK
    tok = lambda bi, ci: (bi, ci, 0)
    st = lambda bi, ci: (bi, 0, 0)
    kern = functools.partial(_mlstm_kernel, valid=valid)
    return pl.pallas_call(
        kern,
        grid=(b // n_seq, nc),
        in_specs=[
            pl.BlockSpec((n_seq, CHUNK, 3 * W_B), tok),
            pl.BlockSpec((n_seq, CHUNK, LANES), tok),
            pl.BlockSpec((n_seq, CHUNK, 2 * W_B), tok),
            pl.BlockSpec((1, W_B), lambda bi, ci: (0, 0)),
            pl.BlockSpec((n_seq, W_B, W_B), st),
            pl.BlockSpec((n_seq, 1, W_B), st),
            pl.BlockSpec((n_seq, 1, W_B), st),
        ],
        out_specs=[
            pl.BlockSpec((n_seq, CHUNK, W_B), tok),
            pl.BlockSpec((n_seq, W_B, W_B), st),
            pl.BlockSpec((n_seq, 1, W_B), st),
            pl.BlockSpec((n_seq, 1, W_B), st),
        ],
        out_shape=[
            jax.ShapeDtypeStruct((b, s, W_B), BF16),
            jax.ShapeDtypeStruct((b, W_B, W_B), F32),
            jax.ShapeDtypeStruct((b, 1, W_B), F32),
            jax.ShapeDtypeStruct((b, 1, W_B), F32),
        ],
        compiler_params=pltpu.CompilerParams(
            dimension_semantics=("arbitrary", "arbitrary"), vmem_limit_bytes=VMEM_LIMIT),
        name="mlstm",
    )(mb, gl, og, g_mh_row, c0_bd, n0_row, m0_row)


def _mix_out_kernel(x_ref, oa_ref, hb_ref, qm_ref, gm_ref, mk_ref, mvt_ref, wo_ref, y_ref,
                    *, sub_rows):
    tm = x_ref.shape[0]
    sub = min(tm, sub_rows)
    mk = mk_ref[...]
    mvt = mvt_ref[...]
    k_feat_head = lax.broadcasted_iota(jnp.int32, mk.shape, 1) // DH_M
    v_feat_head = lax.broadcasted_iota(jnp.int32, mvt.shape, 0) // DH_M
    k_of_head = [jnp.where(k_feat_head == hd, mk, 0.0).astype(BF16) for hd in range(H_M)]
    vt_of_head = [jnp.where(v_feat_head == hd, mvt, 0.0).astype(BF16) for hd in range(H_M)]

    def probs(s):
        p = jnp.exp2(s - jnp.max(s, axis=0, keepdims=True))
        return (p * (1.0 / jnp.sum(p, axis=0, keepdims=True))).astype(BF16)

    def sub_tile_stages(r0):
        rows = slice(r0, r0 + sub)
        qm = qm_ref[rows, :]
        if sub < LANES:
            qm = jnp.concatenate([qm, jnp.zeros((LANES - sub, W_M), BF16)], axis=0)
        s0 = _dot_nt(k_of_head[0], qm)
        y = x_ref[rows, :] + _dot(oa_ref[rows, :], wo_ref[0:W_A, :])
        yield
        s1 = _dot_nt(k_of_head[1], qm)
        p0 = probs(s0)
        y = y + _dot(hb_ref[rows, :], wo_ref[W_A:W_A + W_B, :])
        yield
        s2 = _dot_nt(k_of_head[2], qm)
        p1 = probs(s1)
        omt = _dot(vt_of_head[0], p0)
        yield
        s3 = _dot_nt(k_of_head[3], qm)
        p2 = probs(s2)
        omt = omt + _dot(vt_of_head[1], p1)
        yield
        p3 = probs(s3)
        omt = omt + _dot(vt_of_head[2], p2)
        omt = omt + _dot(vt_of_head[3], p3)
        yield
        om = omt.T[0:sub] * gm_ref[rows, :]
        y_ref[rows, :] = y + _dot(om.astype(BF16), wo_ref[W_A + W_B:D_MIX, :])
        yield

    n_stage = 6
    gens = [sub_tile_stages(r0) for r0 in range(0, tm, sub)]
    for step in range(n_stage + len(gens) - 1):
        for idx, gen in enumerate(gens):
            if 0 <= step - idx < n_stage:
                next(gen)


def _mix_out(x, oa, hb, qm, gm, mem_k, mem_vt, w_out_bf, tm):
    b, s, _ = x.shape
    tok = lambda bi, ti: (bi, ti, 0)
    per_b = lambda bi, ti: (bi, 0, 0)
    const = lambda bi, ti: (0, 0)
    return pl.pallas_call(
        functools.partial(_mix_out_kernel, sub_rows=MXU_DIM),
        grid=(b, s // tm),
        in_specs=[
            pl.BlockSpec((None, tm, D_MODEL), tok),
            pl.BlockSpec((None, tm, W_A), tok),
            pl.BlockSpec((None, tm, W_B), tok),
            pl.BlockSpec((None, tm, W_M), tok),
            pl.BlockSpec((None, tm, W_M), tok),
            pl.BlockSpec((None, N_MEM, W_M), per_b),
            pl.BlockSpec((None, N_MEM, W_M), per_b),
            pl.BlockSpec((D_MIX, D_MODEL), const),
        ],
        out_specs=pl.BlockSpec((None, tm, D_MODEL), tok),
        out_shape=jax.ShapeDtypeStruct((b, s, D_MODEL), F32),
        compiler_params=pltpu.CompilerParams(
            dimension_semantics=("arbitrary", "arbitrary"), vmem_limit_bytes=VMEM_LIMIT),
        name="mix_out",
    )(x, oa, hb, qm, gm, mem_k, mem_vt, w_out_bf)


def _block_diag_state(c):
    b = c.shape[0]
    eye = jnp.eye(H_B, dtype=c.dtype)
    return jnp.einsum('bhde,hg->bhdge', c, eye).reshape(b, W_B, W_B)


def _diag_blocks(c_bd):
    b = c_bd.shape[0]
    c5 = c_bd.reshape(b, H_B, DH_B, H_B, DH_B)
    return jnp.stack([c5[:, h, :, h, :] for h in range(H_B)], axis=1)


def _mixer_layer(x, past_kv, mlstm_state, mem_k, mem_vt, lam_init, p):
    b, s, _ = x.shape
    n = b * s
    tm = min(512, n)
    qaug, k_new, v_new, ga, mb, gl, og, qm, gm, k4, v4 = _in_proj(
        x.reshape(n, D_MODEL), tm, p["g_norm"], p["w_perm"], p["g_mat"],
        p["g_qa"], p["g_ka"], p["g_qm"], p["b_if"])
    k_new = k_new.reshape(b, s, W_A)
    v_new = v_new.reshape(b, s, W_A)
    qaug = qaug.reshape(b, s, 2 * W_A)
    ga = ga.reshape(b, s, W_A)

    if past_kv is None:
        oa = _prompt_attn(qaug, k_new, v_new, ga, p["lam_vecs"], p["g_subln"], lam_init,
                          tile=256)
    else:
        oa = _sample_attn(qaug, past_kv[0], past_kv[1], k_new, v_new, ga,
                          p["lam_vecs"], p["g_subln"], lam_init)

    c0, n0, m0 = mlstm_state
    s_pad = -(-s // CHUNK) * CHUNK
    mb3 = mb.reshape(b, s, 3 * W_B)
    gl3 = gl.reshape(b, s, LANES)
    og3 = og.reshape(b, s, 2 * W_B)
    if s_pad != s:
        pad = ((0, 0), (0, s_pad - s), (0, 0))
        mb3, gl3, og3 = jnp.pad(mb3, pad), jnp.pad(gl3, pad), jnp.pad(og3, pad)
    valid = CHUNK if s_pad == s else s
    hb, c_bd, n_row, m_row = _mlstm(
        mb3, gl3, og3, p["g_mh"], _block_diag_state(c0), n0.reshape(b, 1, W_B),
        jnp.repeat(m0, DH_B, axis=-1).reshape(b, 1, W_B), valid, n_seq=8)
    hb = hb[:, :s]
    new_state = (_diag_blocks(c_bd), n_row.reshape(b, H_B, DH_B),
                 m_row.reshape(b, H_B, DH_B)[:, :, 0])

    y = _mix_out(x, oa, hb, qm.reshape(b, s, W_M), gm.reshape(b, s, W_M), mem_k, mem_vt,
                 p["w_out"], tm=min(1024, s))
    return (y, k4.reshape(b, s, H_A, DV_A), v4.reshape(b, s, H_A, DV_A), new_state)


def kernel(x_prompt, x_sample, cache_attn_k, cache_attn_v, state_mlstm_C, state_mlstm_n,
           state_mlstm_m, cache_mem_k, cache_mem_v, mem_prompt, g_norm, w_in, w_out, g_qa,
           g_ka, lam_q1, lam_k1, lam_q2, lam_k2, g_subln, b_i, b_f, g_mh, g_qm, g_km, g_mem,
           w_mk, w_mv):
    depth = w_in.shape[0]
    bp = x_prompt.shape[0]
    bs = x_sample.shape[0]
    past = cache_attn_k.shape[2]
    gi = lax.broadcasted_iota(jnp.int32, (MXU_DIM, MXU_DIM), 0) // DH_B
    gj = lax.broadcasted_iota(jnp.int32, (MXU_DIM, MXU_DIM), 1) // DH_B
    g_mat = jnp.where(gi == gj, 1.0 / DH_B, 0.0).astype(BF16)

    xp, xs = x_prompt, x_sample
    outs = {name: [] for name in ("pk", "pv", "pC", "pn", "pm", "pmk", "pmv",
                                  "sk", "sv", "sC", "sn", "sm")}
    for l in range(depth):
        lam_init = 0.8 - 0.6 * math.exp(-0.3 * l)
        w = w_in[l]
        w_perm = jnp.concatenate(
            [w[:, :4 * W_A + 4 * W_B], w[:, 4 * W_A + 4 * W_B + 2 * H_B:],
             w[:, 4 * W_A + 4 * W_B:4 * W_A + 4 * W_B + 2 * H_B],
             jnp.zeros((D_MODEL, LANES - 2 * H_B), F32)], axis=1).astype(BF16)
        p = {
            "g_norm": g_norm[l].reshape(1, D_MODEL),
            "w_perm": w_perm,
            "g_mat": g_mat,
            "g_qa": jnp.tile(g_qa[l], 2 * H_A).reshape(1, W_A),
            "g_ka": jnp.tile(g_ka[l], 2 * H_A).reshape(1, W_A),
            "g_qm": jnp.tile(g_qm[l], H_M).reshape(1, W_M),
            "b_if": jnp.concatenate(
                [b_i[l], b_f[l], jnp.zeros((LANES - 2 * H_B,), F32)]).reshape(1, LANES),
            "lam_vecs": tuple(v[l].reshape(1, DK_A) for v in (lam_q1, lam_k1, lam_q2, lam_k2)),
            "g_subln": g_subln[l].reshape(1, DV_A),
            "g_mh": jnp.tile(g_mh[l], H_B).reshape(1, W_B),
            "w_out": w_out[l].astype(BF16),
        }
        mk, mkt, mvt = _memory_kv(mem_prompt, g_mem[l], w_mk[l], w_mv[l], g_km[l], g_mat)
        zero_state = (jnp.zeros((bp, H_B, DH_B, DH_B), F32), jnp.zeros((bp, H_B, DH_B), F32),
                      jnp.zeros((bp, H_B), F32))
        xp, k_p, v_p, st_p = _mixer_layer(xp, None, zero_state, mk, mvt, lam_init, p)
        xs, k_s, v_s, st_s = _mixer_layer(
            xs, (cache_attn_k[l], cache_attn_v[l]),
            (state_mlstm_C[l], state_mlstm_n[l], state_mlstm_m[l]),
            cache_mem_k[l].reshape(bs, N_MEM, W_M),
            cache_mem_v[l].transpose(0, 2, 3, 1).reshape(bs, W_M, N_MEM),
            lam_init, p)

        def tokens_first(t):
            return t.reshape(bp, H_M, DH_M, N_MEM).transpose(0, 3, 1, 2)

        outs["pk"].append(k_p); outs["pv"].append(v_p)
        outs["pC"].append(st_p[0]); outs["pn"].append(st_p[1]); outs["pm"].append(st_p[2])
        outs["pmk"].append(tokens_first(mkt))
        outs["pmv"].append(tokens_first(mvt))
        outs["sk"].append(k_s); outs["sv"].append(v_s)
        outs["sC"].append(st_s[0]); outs["sn"].append(st_s[1]); outs["sm"].append(st_s[2])
    stk = {name: jnp.stack(vals) for name, vals in outs.items()}
    return (xp, xs, stk["pk"], stk["pv"], stk["pC"], stk["pn"], stk["pm"], stk["pmk"],
            stk["pmv"], stk["sk"], stk["sv"], stk["sC"], stk["sn"], stk["sm"])
```

```python
import functools
import math

import jax
import jax.numpy as jnp
import numpy as np
from jax import lax
from jax.experimental import pallas as pl
from jax.experimental.pallas import tpu as pltpu

F32 = jnp.float32
BF16 = jnp.bfloat16

D_MODEL = 1024
CHUNK = 64
N_MEM = 256
H_A, DK_A = 4, 64
DV_A = 2 * DK_A
W_A = H_A * DV_A
H_B, DH_B = 4, 64
W_B = H_B * DH_B
H_M, DH_M = 4, 64
W_M = H_M * DH_M
D_MIX = W_A + W_B + W_M
EPS = 1e-6
ALIBI_SLOPES = tuple(2.0 ** (-8.0 * (h + 1) / H_A) for h in range(H_A))
N_IN = 4 * W_A + 5 * W_B + 2 * H_B + 2 * W_M

LANES = 128
MXU_DIM = 256
BF16_SUBLANES = 16
VT_ROWS = DV_A + BF16_SUBLANES
NEG_BIG = -1e30
LOG2E = math.log2(math.e)


def _bf16_pieces(value, n=3):
    pieces, rest = [], np.float32(value)
    for _ in range(n):
        piece = np.float32(rest.astype(jnp.bfloat16))
        pieces.append(float(piece))
        rest = np.float32(rest - piece)
    return tuple(pieces)


LOG2E_BF16_PIECES = _bf16_pieces(LOG2E)

C_QA, C_KA, C_VA, C_GA = 0, W_A, 2 * W_A, 3 * W_A
C_QB = 4 * W_A
C_KB, C_VB, C_OB, C_GB = C_QB + W_B, C_QB + 2 * W_B, C_QB + 3 * W_B, C_QB + 4 * W_B
C_QM = C_GB + W_B
C_GM = C_QM + W_M
C_IF = C_GM + W_M
N_IN_PAD = C_IF + LANES

VMEM_LIMIT = 56 * 1024 * 1024


def _dot(a, b):
    return jnp.dot(a, b, preferred_element_type=F32)


def _dot_nt(a, b):
    return lax.dot_general(a, b, (((1,), (1,)), ((), ())), preferred_element_type=F32)


def _split3(x):
    hi = x.astype(BF16)
    r1 = x - hi.astype(F32)
    mid = r1.astype(BF16)
    lo = (r1 - mid.astype(F32)).astype(BF16)
    return hi, mid, lo


def _group_mean_sq(z, g_mat):
    zz = (z * z).astype(BF16)
    parts = [_dot(zz[:, c:c + MXU_DIM], g_mat) for c in range(0, z.shape[1], MXU_DIM)]
    return parts[0] if len(parts) == 1 else jnp.concatenate(parts, axis=1)


def _group_rms_norm(z, gain_row, g_mat):
    return z * lax.rsqrt(_group_mean_sq(z, g_mat) + EPS) * gain_row


def _log_sigmoid(u):
    return -(jnp.maximum(-u, 0.0) + jnp.log1p(jnp.exp(-jnp.abs(u))))


def _silu(u):
    return u * jax.nn.sigmoid(u)


def _memory_kv_kernel(mem_ref, gmem_ref, wk_ref, wv_ref, gkm_ref, gmat_ref,
                      mk_ref, mkt_ref, mvt_ref, mv_scr):
    x = mem_ref[...]
    ms = jnp.mean(x * x, axis=-1, keepdims=True)
    hm = (x * lax.rsqrt(ms + EPS) * gmem_ref[...]).astype(BF16)
    mk_ref[...] = _group_rms_norm(_dot(hm, wk_ref[...]), gkm_ref[...], gmat_ref[...])
    mv_scr[...] = _dot(hm, wv_ref[...])
    mkt_ref[...] = mk_ref[...].T
    mvt_ref[...] = mv_scr[...].T


def _memory_kv(mem, g_mem, w_mk, w_mv, g_km, g_mat):
    b, n, d = mem.shape
    row = lambda i: (i, 0, 0)
    const2 = lambda i: (0, 0)
    return pl.pallas_call(
        _memory_kv_kernel,
        grid=(b,),
        in_specs=[
            pl.BlockSpec((None, n, d), row),
            pl.BlockSpec((1, d), const2),
            pl.BlockSpec((d, W_M), const2),
            pl.BlockSpec((d, W_M), const2),
            pl.BlockSpec((1, W_M), const2),
            pl.BlockSpec((MXU_DIM, MXU_DIM), const2),
        ],
        out_specs=[pl.BlockSpec((None, n, W_M), row), pl.BlockSpec((None, W_M, n), row),
                   pl.BlockSpec((None, W_M, n), row)],
        out_shape=[jax.ShapeDtypeStruct((b, n, W_M), F32),
                   jax.ShapeDtypeStruct((b, W_M, n), F32),
                   jax.ShapeDtypeStruct((b, W_M, n), F32)],
        scratch_shapes=[pltpu.VMEM((n, W_M), F32)],
        compiler_params=pltpu.CompilerParams(
            dimension_semantics=("arbitrary",), vmem_limit_bytes=VMEM_LIMIT),
        name="memory_kv",
    )(mem, g_mem.reshape(1, d), w_mk.astype(BF16), w_mv.astype(BF16),
      jnp.tile(g_km, H_M).reshape(1, W_M), g_mat)


def _in_proj_kernel(x_ref, gn_ref, w_ref, gmat_ref, gqa_ref, gka_ref, gqm_ref, bif_ref,
                    qaug_ref, k_ref, v_ref, ga_ref, mb_ref, gl_ref, og_ref, qm_ref, gm_ref,
                    k4_ref, v4_ref):
    x = x_ref[...]
    h = (x * gn_ref[...]).astype(BF16)
    inv_rms = lax.rsqrt(jnp.mean(x * x, axis=-1, keepdims=True) + EPS)
    g_mat = gmat_ref[...]

    def proj(start, width):
        return _dot(h, w_ref[:, start:start + width]) * inv_rms

    qn = _group_rms_norm(proj(C_QA, W_A), gqa_ref[...], g_mat) * (DK_A ** -0.5 * LOG2E)
    lane = lax.broadcasted_iota(jnp.int32, (x.shape[0], LANES), 1)
    aug = jnp.zeros((x.shape[0], LANES), F32)
    for rep in range(2):
        for piece, val in enumerate(LOG2E_BF16_PIECES):
            aug = jnp.where(lane == DK_A + 3 * rep + piece, val, aug)
    for hd in range(H_A):
        slab = qn[:, hd * DV_A:(hd + 1) * DV_A]
        q1 = jnp.where(lane < DK_A, slab, aug)
        q2 = jnp.where(lane < DK_A, pltpu.roll(slab, DK_A, 1), aug)
        qaug_ref[:, (2 * hd) * LANES:(2 * hd + 1) * LANES] = q1.astype(BF16)
        qaug_ref[:, (2 * hd + 1) * LANES:(2 * hd + 2) * LANES] = q2.astype(BF16)

    kn = _group_rms_norm(proj(C_KA, W_A), gka_ref[...], g_mat)
    k_ref[...] = kn.astype(BF16)

    def cache_copy(dst_ref, src, hd, later):
        zero = pltpu.bitcast(
            lax.shift_right_logical(pltpu.bitcast(later[:, 0:DV_A], jnp.uint32), jnp.uint32(32)),
            F32)
        dst_ref[:, hd, :] = src[:, hd * DV_A:(hd + 1) * DV_A] + zero

    vn = proj(C_VA, W_A)
    v_ref[...] = vn.astype(BF16)
    cache_copy(k4_ref, kn, 0, vn)
    z = proj(C_GA, W_A)
    ga_ref[...] = _silu(z)
    cache_copy(k4_ref, kn, 1, z)
    z = proj(C_QB, W_B)
    mb_ref[:, 0:W_B] = z
    cache_copy(k4_ref, kn, 2, z)
    z = proj(C_KB, W_B)
    mb_ref[:, W_B:2 * W_B] = z * (DH_B ** -0.5)
    cache_copy(k4_ref, kn, 3, z)
    z = proj(C_VB, W_B)
    mb_ref[:, 2 * W_B:3 * W_B] = z
    cache_copy(v4_ref, vn, 0, z)
    z = proj(C_OB, W_B)
    og_ref[:, 0:W_B] = jax.nn.sigmoid(z)
    cache_copy(v4_ref, vn, 1, z)
    z = proj(C_GB, W_B)
    og_ref[:, W_B:2 * W_B] = _silu(z)
    cache_copy(v4_ref, vn, 2, z)
    z = proj(C_QM, W_M)
    qm_ref[...] = (_group_rms_norm(z, gqm_ref[...], g_mat)
                   * (DH_M ** -0.5 * LOG2E)).astype(BF16)
    cache_copy(v4_ref, vn, 3, z)
    gm_ref[...] = _silu(proj(C_GM, W_M))

    u = proj(C_IF, LANES) + bif_ref[...]
    gl_ref[...] = jnp.where(lane < H_B, u, _log_sigmoid(u))


def _in_proj(x2d, tm, g_norm, w_perm, g_mat, g_qa, g_ka, g_qm, b_if):
    n = x2d.shape[0]
    row = lambda i: (i, 0)
    const = lambda i: (0, 0)
    widths = (2 * W_A, W_A, W_A, W_A, 3 * W_B, LANES, 2 * W_B, W_M, W_M)
    dtypes = (BF16, BF16, BF16) + (F32,) * 4 + (BF16, F32)
    return pl.pallas_call(
        _in_proj_kernel,
        grid=(n // tm,),
        in_specs=[
            pl.BlockSpec((tm, D_MODEL), row),
            pl.BlockSpec((1, D_MODEL), const),
            pl.BlockSpec((D_MODEL, N_IN_PAD), const),
            pl.BlockSpec((MXU_DIM, MXU_DIM), const),
            pl.BlockSpec((1, W_A), const),
            pl.BlockSpec((1, W_A), const),
            pl.BlockSpec((1, W_M), const),
            pl.BlockSpec((1, LANES), const),
        ],
        out_specs=[pl.BlockSpec((tm, w), row) for w in widths]
        + [pl.BlockSpec((tm, H_A, DV_A), lambda i: (i, 0, 0))] * 2,
        out_shape=[jax.ShapeDtypeStruct((n, w), dt) for w, dt in zip(widths, dtypes)]
        + [jax.ShapeDtypeStruct((n, H_A, DV_A), F32)] * 2,
        compiler_params=pltpu.CompilerParams(
            dimension_semantics=("arbitrary",), vmem_limit_bytes=VMEM_LIMIT),
        name="in_proj",
    )(x2d, g_norm, w_perm, g_mat, g_qa, g_ka, g_qm, b_if)


def _lambda_value(lq1_ref, lk1_ref, lq2_ref, lk2_ref, lam_init):
    s1 = jnp.sum(lq1_ref[...] * lk1_ref[...], axis=-1, keepdims=True)
    s2 = jnp.sum(lq2_ref[...] * lk2_ref[...], axis=-1, keepdims=True)
    return jnp.exp(s1) - jnp.exp(s2) + lam_init


def _subln(o, gsub_row, lam_init):
    ms = jnp.mean(o * o, axis=-1, keepdims=True)
    return o * lax.rsqrt(ms + EPS) * gsub_row * (1.0 - lam_init)


def _prompt_attn_kernel(lq1_ref, lk1_ref, lq2_ref, lk2_ref, gsub_ref, q_ref, k_ref, v_ref,
                        ga_ref, o_ref, kaug_ref, vt_ref, m_ref, acc_ref, s_ref, qt_ref,
                        *, tile, lam_init):
    seq = k_ref.shape[0]
    n_tiles = seq // tile
    n_chain = 2 * H_A

    lane = lax.broadcasted_iota(jnp.int32, (seq, LANES), 1)
    pos = lax.broadcasted_iota(jnp.int32, (seq, LANES), 0)
    in_tile = (pos % tile).astype(F32)
    tile_base = (pos - pos % tile).astype(F32)
    for hd in range(H_A):
        slab = k_ref[:, hd * DV_A:(hd + 1) * DV_A].astype(F32)
        aug = jnp.where((lane >= DK_A) & (lane < DK_A + 3), ALIBI_SLOPES[hd] * in_tile,
                        jnp.where((lane >= DK_A + 3) & (lane < DK_A + 6),
                                  ALIBI_SLOPES[hd] * tile_base, 0.0))
        kaug_ref[2 * hd] = jnp.where(lane < DK_A, slab, aug).astype(BF16)
        kaug_ref[2 * hd + 1] = jnp.where(
            lane < DK_A, pltpu.roll(slab, DK_A, 1), aug).astype(BF16)
    for t in range(n_tiles):
        v_t = v_ref[t * tile:(t + 1) * tile, :].astype(F32).T.astype(BF16)
        for hd in range(H_A):
            vt_ref[t, hd, 0:DV_A, :] = v_t[hd * DV_A:(hd + 1) * DV_A, :]
            vt_ref[t, hd, DV_A:VT_ROWS, :] = jnp.ones((VT_ROWS - DV_A, tile), BF16)

    lam = _lambda_value(lq1_ref, lk1_ref, lq2_ref, lk2_ref, lam_init)

    key = lax.broadcasted_iota(jnp.int32, (tile, tile), 0)
    qry = lax.broadcasted_iota(jnp.int32, (tile, tile), 1)
    ahead = (key - qry).astype(F32)
    diag_bias = [jnp.where(key // CHUNK > qry // CHUNK, NEG_BIG,
                           jnp.where(key > qry, (-2.0 * LOG2E * ALIBI_SLOPES[hd]) * ahead, 0.0))
                 for hd in range(H_A)]

    def rows(i):
        if isinstance(i, int):
            return pl.ds(i * tile, tile)
        return pl.ds(pl.multiple_of(i * tile, tile), tile)

    def scores(par, c, h, j):
        s_ref[par, c, h] = _dot(kaug_ref[c, rows(j), :], qt_ref[par, c, h])

    def softmax_values(par, c, h, j, diagonal):
        s = s_ref[par, c, h]
        if diagonal:
            s = s + diag_bias[c // 2]
        mx = m_ref[par, c, h]
        mx_new = jnp.maximum(mx, jnp.max(s, axis=0, keepdims=True))
        p = jnp.exp2(s - mx_new)
        alpha = jnp.exp2(mx - mx_new)
        m_ref[par, c, h] = mx_new
        acc_ref[par, c, h] = (alpha * acc_ref[par, c, h]
                              + _dot(vt_ref[j, c // 2], p.astype(BF16)))

    def begin_chain(par, c, duo):
        for h in range(2):
            q = q_ref[rows(2 * duo + h), c * LANES:(c + 1) * LANES]
            qt_ref[par, c, h] = q.astype(F32).T.astype(BF16)
            m_ref[par, c, h] = jnp.full((1, tile), NEG_BIG, F32)
            acc_ref[par, c, h] = jnp.zeros((VT_ROWS, tile), F32)
            scores(par, c, h, 0)

    def finish_head(par, hd, duo):
        for h in range(2):
            a1 = acc_ref[par, 2 * hd, h]
            a2 = acc_ref[par, 2 * hd + 1, h]
            o1 = a1[0:DV_A] * (1.0 / a1[DV_A:DV_A + 1])
            o2 = a2[0:DV_A] * (1.0 / a2[DV_A:DV_A + 1])
            o = (o1 - lam * o2).T
            gate = ga_ref[rows(2 * duo + h), hd * DV_A:(hd + 1) * DV_A]
            o_ref[rows(2 * duo + h), hd * DV_A:(hd + 1) * DV_A] = (
                _subln(o, gsub_ref[...], lam_init) * gate).astype(BF16)

    for c in range(n_chain):
        begin_chain(0, c, 0)

    def duo_pair(pair, carry):
        for par in range(2):
            duo = 2 * pair + par
            first_diag = 2 * duo

            def past_tile(j, inner, par=par):
                for c in range(n_chain):
                    for h in range(2):
                        softmax_values(par, c, h, j, False)
                        scores(par, c, h, j + 1)
                return inner

            lax.fori_loop(0, first_diag, past_tile, 0)
            for c in range(n_chain):
                softmax_values(par, c, 0, first_diag, True)
                softmax_values(par, c, 1, first_diag, False)
                scores(par, c, 1, first_diag + 1)
            nxt = lax.rem(duo + 1, n_tiles // 2)
            for c in range(n_chain):
                softmax_values(par, c, 1, first_diag + 1, True)
                begin_chain(1 - par, c, nxt)
                if c % 2 == 1:
                    finish_head(par, c // 2, duo)
        return carry

    lax.fori_loop(0, n_tiles // 4, duo_pair, 0)


def _prompt_attn(qaug, k, v, ga, lam_vecs, g_subln, lam_init, tile):
    b, s, _ = k.shape
    assert (s // tile) % 4 == 0
    const = lambda bi: (0, 0)
    per_b = lambda bi: (bi, 0, 0)
    kern = functools.partial(_prompt_attn_kernel, tile=tile, lam_init=lam_init)
    return pl.pallas_call(
        kern,
        grid=(b,),
        in_specs=[pl.BlockSpec((1, DK_A), const)] * 4 + [
            pl.BlockSpec((1, DV_A), const),
            pl.BlockSpec((None, s, 2 * W_A), per_b),
            pl.BlockSpec((None, s, W_A), per_b),
            pl.BlockSpec((None, s, W_A), per_b),
            pl.BlockSpec((None, s, W_A), per_b),
        ],
        out_specs=pl.BlockSpec((None, s, W_A), per_b),
        out_shape=jax.ShapeDtypeStruct((b, s, W_A), BF16),
        scratch_shapes=[
            pltpu.VMEM((2 * H_A, s, LANES), BF16),
            pltpu.VMEM((s // tile, H_A, VT_ROWS, tile), BF16),
            pltpu.VMEM((2, 2 * H_A, 2, 1, tile), F32),
            pltpu.VMEM((2, 2 * H_A, 2, VT_ROWS, tile), F32),
            pltpu.VMEM((2, 2 * H_A, 2, tile, tile), F32),
            pltpu.VMEM((2, 2 * H_A, 2, LANES, tile), BF16),
        ],
        compiler_params=pltpu.CompilerParams(
            dimension_semantics=("arbitrary",), vmem_limit_bytes=VMEM_LIMIT),
        name="diff_attn_prompt",
    )(*lam_vecs, g_subln, qaug, k, v, ga)


def _sample_attn_kernel(lq1_ref, lk1_ref, lq2_ref, lk2_ref, gsub_ref, q_ref, kc_ref, vc_ref,
                        kn_ref, vn_ref, ga_ref, o_ref, *, lam_init):
    hd = pl.program_id(1)
    sq = q_ref.shape[0]
    past = kc_ref.shape[0] // H_A
    lam = _lambda_value(lq1_ref, lk1_ref, lq2_ref, lk2_ref, lam_init)
    lane = lax.broadcasted_iota(jnp.int32, (sq, LANES), 1)
    qpos_c = past + lax.broadcasted_iota(jnp.int32, (sq, past), 0)
    dist_c = jnp.abs(qpos_c - lax.broadcasted_iota(jnp.int32, (sq, past), 1)).astype(F32)
    dist_n = jnp.abs(lax.broadcasted_iota(jnp.int32, (sq, sq), 0)
                     - lax.broadcasted_iota(jnp.int32, (sq, sq), 1)).astype(F32)
    slope = jnp.float32(ALIBI_SLOPES[H_A - 1])
    for h in range(H_A - 1):
        slope = jnp.where(hd == h, jnp.float32(ALIBI_SLOPES[h]), slope)
    kc = kc_ref[pl.ds(hd, past, stride=H_A), :].astype(BF16)
    kn = kn_ref[...]
    vc = vc_ref[pl.ds(hd, past, stride=H_A), :].astype(BF16)
    vn = vn_ref[...]
    outs = []
    for m in range(2):
        q = q_ref[:, m * LANES:(m + 1) * LANES].astype(F32)
        if m == 0:
            q = jnp.where(lane < DK_A, q, 0.0)
        else:
            q = jnp.where(lane >= DK_A, pltpu.roll(q, DK_A, 1), 0.0)
        q = q.astype(BF16)
        s_c = _dot_nt(q, kc) - (slope * LOG2E) * dist_c
        s_n = _dot_nt(q, kn) - (slope * LOG2E) * dist_n
        mx = jnp.maximum(jnp.max(s_c, axis=-1, keepdims=True),
                         jnp.max(s_n, axis=-1, keepdims=True))
        p_c = jnp.exp2(s_c - mx)
        p_n = jnp.exp2(s_n - mx)
        l = jnp.sum(p_c, axis=-1, keepdims=True) + jnp.sum(p_n, axis=-1, keepdims=True)
        acc = _dot(p_c.astype(BF16), vc) + _dot(p_n.astype(BF16), vn)
        outs.append(acc * (1.0 / l))
    o = outs[0] - lam * outs[1]
    o_ref[...] = (_subln(o, gsub_ref[...], lam_init) * ga_ref[...]).astype(BF16)


def _sample_attn(qaug, k_cache, v_cache, k_new, v_new, ga, lam_vecs, g_subln, lam_init):
    b, past = k_cache.shape[:2]
    sq = k_new.shape[1]
    assert (past + sq - 1) // CHUNK <= past // CHUNK
    k_cache = k_cache.reshape(b, past * H_A, DV_A)
    v_cache = v_cache.reshape(b, past * H_A, DV_A)
    const = lambda bi, hi: (0, 0)
    head = lambda bi, hi: (bi, 0, hi)
    cache = lambda bi, hi: (bi, 0, 0)
    kern = functools.partial(_sample_attn_kernel, lam_init=lam_init)
    return pl.pallas_call(
        kern,
        grid=(b, H_A),
        in_specs=[pl.BlockSpec((1, DK_A), const)] * 4 + [
            pl.BlockSpec((1, DV_A), const),
            pl.BlockSpec((None, sq, 2 * LANES), head),
            pl.BlockSpec((None, past * H_A, DV_A), cache),
            pl.BlockSpec((None, past * H_A, DV_A), cache),
            pl.BlockSpec((None, sq, DV_A), head),
            pl.BlockSpec((None, sq, DV_A), head),
            pl.BlockSpec((None, sq, DV_A), head),
        ],
        out_specs=pl.BlockSpec((None, sq, DV_A), head),
        out_shape=jax.ShapeDtypeStruct((b, sq, W_A), BF16),
        compiler_params=pltpu.CompilerParams(
            dimension_semantics=("arbitrary", "arbitrary"), vmem_limit_bytes=VMEM_LIMIT),
        name="diff_attn_sample",
    )(*lam_vecs, g_subln, qaug, k_cache, v_cache, k_new, v_new, ga)


def _mlstm_kernel(mb_ref, gl_ref, og_ref, gmh_ref, c0_ref, n0_ref, m0_ref,
                  h_ref, c_out_ref, n_ref, m_ref, c_ref, *, valid):
    c_idx = pl.program_id(1)
    L = CHUNK
    W = W_B
    G = mb_ref.shape[0]
    R = G * L

    def head_block(hd):
        return slice(hd * DH_B, (hd + 1) * DH_B)

    @pl.when(c_idx == 0)
    def _load_state():
        c_ref[...] = jnp.zeros(c_ref.shape, F32)
        for g in range(G):
            for hd in range(H_B):
                c_ref[g, head_block(hd), head_block(hd)] = c0_ref[g, hd]
        n_ref[...] = n0_ref[...]
        m_ref[...] = m0_ref[...]

    lane_head = lax.broadcasted_iota(jnp.int32, (R, W), 1) // DH_B
    row_t = lax.broadcasted_iota(jnp.int32, (R, W), 0) % L
    lane_s = lax.broadcasted_iota(jnp.int32, (R, W), 1) % DH_B
    wi = lax.broadcasted_iota(jnp.int32, (W, W), 0)
    wj = lax.broadcasted_iota(jnp.int32, (W, W), 1)
    blk = wi // DH_B == wj // DH_B
    ones_bd = blk.astype(BF16)
    ei = lax.broadcasted_iota(jnp.int32, (LANES, 2 * W), 0)
    ej = lax.broadcasted_iota(jnp.int32, (LANES, 2 * W), 1)
    expand = (ei == ej // DH_B).astype(BF16)
    tri = (lax.broadcasted_iota(jnp.int32, (L, L), 1)
           <= lax.broadcasted_iota(jnp.int32, (L, L), 0)).astype(BF16)
    nar_lane = lax.broadcasted_iota(jnp.int32, (R, LANES), 1)
    nar_t = lax.broadcasted_iota(jnp.int32, (R, LANES), 0) % L

    def per_seq(x):
        return x.reshape(G, L, x.shape[-1])

    def stacked(x):
        return x.reshape(R, x.shape[-1])

    def seq_rows(x, g):
        return x[g * L:(g + 1) * L]

    def exact_matmul(xs, mat):
        n = xs[0].shape[0]
        pieces = [part for x in xs for part in _split3(x)]
        out = _dot(jnp.concatenate(pieces, axis=0), mat)
        return [out[(3 * i) * n:(3 * i + 1) * n] + out[(3 * i + 1) * n:(3 * i + 2) * n]
                + out[(3 * i + 2) * n:(3 * i + 3) * n] for i in range(len(xs))]

    def head_max(x):
        out = jnp.zeros((R, W), F32)
        for hd in range(H_B):
            mx = jnp.max(jnp.where(lane_head == hd, x, NEG_BIG), axis=-1, keepdims=True)
            out = jnp.where(lane_head == hd, mx, out)
        return out

    q = stacked(mb_ref[:, :, 0:W])
    k = stacked(mb_ref[:, :, W:2 * W])
    v = stacked(mb_ref[:, :, 2 * W:3 * W])
    gl = jnp.where(nar_lane < 2 * H_B, stacked(gl_ref[...]), 0.0)
    gl = jnp.where(nar_t < valid, gl, jnp.where(nar_lane < H_B, NEG_BIG, 0.0))
    cums = []
    for g in range(G):
        out = _dot(tri, jnp.concatenate(_split3(seq_rows(gl, g)), axis=1))
        cums.append(out[:, 0:LANES] + out[:, LANES:2 * LANES] + out[:, 2 * LANES:3 * LANES])
    narrow = jnp.where(nar_lane < H_B, gl, jnp.concatenate(cums, axis=0))
    (wide,) = exact_matmul([narrow], expand)
    ig_all = wide[:, 0:W]
    b_all = wide[:, W:2 * W]

    r_all = b_all - ig_all
    r_row = jnp.sum(per_seq(jnp.where(lane_s == row_t, r_all, 0.0)), axis=1, keepdims=True)
    m0_row = m_ref[...]
    n0_row = n_ref[...]
    b_seq = per_seq(b_all)
    d_mat = jnp.where(lane_s <= row_t, stacked(b_seq - r_row), NEG_BIG)
    g_all = stacked(b_seq + m0_row)
    m_all = jnp.maximum(g_all, head_max(d_mat))
    d_w = jnp.exp(d_mat - m_all)
    g_w = jnp.exp(g_all - m_all)

    q_bf = q.astype(BF16)
    zero_bf = jnp.zeros((W, W), BF16)
    scores, inter, v_bds = [], [], []
    for g in range(G):
        k_t = seq_rows(k, g).T.astype(BF16)
        k_bd = jnp.where(blk, jnp.concatenate([k_t] * H_B, axis=1), zero_bf)
        v_bds.append(jnp.where(
            blk, jnp.concatenate([seq_rows(v, g).astype(BF16)] * H_B, axis=0), zero_bf))
        c0_bd = jnp.where(blk, c_ref[g].astype(BF16), zero_bf)
        scores.append(_dot(seq_rows(q_bf, g), k_bd))
        inter.append(_dot(seq_rows(q_bf, g), c0_bd))
    a_mat = jnp.concatenate(scores, axis=0) * d_w
    a_bf = a_mat.astype(BF16)
    intra = [_dot(seq_rows(a_bf, g), v_bds[g]) for g in range(G)]
    num = g_w * jnp.concatenate(inter, axis=0) + jnp.concatenate(intra, axis=0)

    qn, a_sum = exact_matmul([stacked(per_seq(q) * n0_row), a_mat], ones_bd)
    den = g_w * qn + a_sum
    h = num / jnp.maximum(jnp.abs(den), jnp.exp(-m_all))
    h_ms = _dot((h * h).astype(BF16), ones_bd) * (1.0 / DH_B)
    h = h * lax.rsqrt(h_ms + EPS) * gmh_ref[...]
    og = stacked(og_ref[...])
    h_ref[...] = per_seq((h * og[:, 0:W] * og[:, W:2 * W]).astype(BF16))

    m_last = per_seq(m_all)[:, L - 1:L, :]
    b_last = b_seq[:, L - 1:L, :]
    w_s = jnp.exp(stacked(b_last - b_seq + per_seq(ig_all) - m_last))
    decay = jnp.exp(b_last + m0_row - m_last)
    kw = k * w_s
    v_bf = v.astype(BF16)
    for g in range(G):
        c_ref[g] = decay[g] * c_ref[g] + _dot(seq_rows(kw, g).T.astype(BF16), seq_rows(v_bf, g))
    n_ref[...] = decay * n0_row + jnp.sum(per_seq(kw), axis=1, keepdims=True)
    m_ref[...] = m_last

    @pl.when(c_idx == pl.num_programs(1) - 1)
    def _store_state():
        for g in range(G):
            for hd in range(H_B):
                c_out_ref[g, hd] = c_ref[g, head_block(hd), head_block(hd)]


def _mlstm(mb, gl, og, g_mh_row, c0, n0_row, m0_row, valid, n_seq):
    b, s, _ = mb.shape
    nc = s // CHUNK
    tok = lambda bi, ci: (bi, ci, 0)
    st = lambda bi, ci: (bi, 0, 0)
    st4 = lambda bi, ci: (bi, 0, 0, 0)
    kern = functools.partial(_mlstm_kernel, valid=valid)
    return pl.pallas_call(
        kern,
        grid=(b // n_seq, nc),
        in_specs=[
            pl.BlockSpec((n_seq, CHUNK, 3 * W_B), tok),
            pl.BlockSpec((n_seq, CHUNK, LANES), tok),
            pl.BlockSpec((n_seq, CHUNK, 2 * W_B), tok),
            pl.BlockSpec((1, W_B), lambda bi, ci: (0, 0)),
            pl.BlockSpec((n_seq, H_B, DH_B, DH_B), st4),
            pl.BlockSpec((n_seq, 1, W_B), st),
            pl.BlockSpec((n_seq, 1, W_B), st),
        ],
        out_specs=[
            pl.BlockSpec((n_seq, CHUNK, W_B), tok),
            pl.BlockSpec((n_seq, H_B, DH_B, DH_B), st4),
            pl.BlockSpec((n_seq, 1, W_B), st),
            pl.BlockSpec((n_seq, 1, W_B), st),
        ],
        out_shape=[
            jax.ShapeDtypeStruct((b, s, W_B), BF16),
            jax.ShapeDtypeStruct((b, H_B, DH_B, DH_B), F32),
            jax.ShapeDtypeStruct((b, 1, W_B), F32),
            jax.ShapeDtypeStruct((b, 1, W_B), F32),
        ],
        scratch_shapes=[pltpu.VMEM((n_seq, W_B, W_B), F32)],
        compiler_params=pltpu.CompilerParams(
            dimension_semantics=("arbitrary", "arbitrary"), vmem_limit_bytes=VMEM_LIMIT),
        name="mlstm",
    )(mb, gl, og, g_mh_row, c0, n0_row, m0_row)


def _mix_out_kernel(x_ref, oa_ref, hb_ref, qm_ref, gm_ref, mk_ref, mvt_ref, wo_ref, y_ref,
                    *, sub_rows):
    tm = x_ref.shape[0]
    sub = min(tm, sub_rows)
    mk = mk_ref[...]
    mvt = mvt_ref[...]
    k_feat_head = lax.broadcasted_iota(jnp.int32, mk.shape, 1) // DH_M
    v_feat_head = lax.broadcasted_iota(jnp.int32, mvt.shape, 0) // DH_M
    k_of_head = [jnp.where(k_feat_head == hd, mk, 0.0).astype(BF16) for hd in range(H_M)]
    vt_of_head = [jnp.where(v_feat_head == hd, mvt, 0.0).astype(BF16) for hd in range(H_M)]

    def probs(s):
        p = jnp.exp2(s - jnp.max(s, axis=0, keepdims=True))
        return (p * (1.0 / jnp.sum(p, axis=0, keepdims=True))).astype(BF16)

    def sub_tile_stages(r0):
        rows = slice(r0, r0 + sub)
        qm = qm_ref[rows, :]
        if sub < LANES:
            qm = jnp.concatenate([qm, jnp.zeros((LANES - sub, W_M), BF16)], axis=0)
        s0 = _dot_nt(k_of_head[0], qm)
        y = x_ref[rows, :] + _dot(oa_ref[rows, :], wo_ref[0:W_A, :])
        yield
        s1 = _dot_nt(k_of_head[1], qm)
        p0 = probs(s0)
        y = y + _dot(hb_ref[rows, :], wo_ref[W_A:W_A + W_B, :])
        yield
        s2 = _dot_nt(k_of_head[2], qm)
        p1 = probs(s1)
        omt = _dot(vt_of_head[0], p0)
        yield
        s3 = _dot_nt(k_of_head[3], qm)
        p2 = probs(s2)
        omt = omt + _dot(vt_of_head[1], p1)
        yield
        p3 = probs(s3)
        omt = omt + _dot(vt_of_head[2], p2)
        omt = omt + _dot(vt_of_head[3], p3)
        yield
        om = omt.T[0:sub] * gm_ref[rows, :]
        y_ref[rows, :] = y + _dot(om.astype(BF16), wo_ref[W_A + W_B:D_MIX, :])
        yield

    n_stage = 6
    gens = [sub_tile_stages(r0) for r0 in range(0, tm, sub)]
    for step in range(n_stage + len(gens) - 1):
        for idx, gen in enumerate(gens):
            if 0 <= step - idx < n_stage:
                next(gen)


def _mix_out(x, oa, hb, qm, gm, mem_k, mem_vt, w_out_bf, tm):
    b, s, _ = x.shape
    tok = lambda bi, ti: (bi, ti, 0)
    per_b = lambda bi, ti: (bi, 0, 0)
    const = lambda bi, ti: (0, 0)
    return pl.pallas_call(
        functools.partial(_mix_out_kernel, sub_rows=MXU_DIM),
        grid=(b, s // tm),
        in_specs=[
            pl.BlockSpec((None, tm, D_MODEL), tok),
            pl.BlockSpec((None, tm, W_A), tok),
            pl.BlockSpec((None, tm, W_B), tok),
            pl.BlockSpec((None, tm, W_M), tok),
            pl.BlockSpec((None, tm, W_M), tok),
            pl.BlockSpec((None, N_MEM, W_M), per_b),
            pl.BlockSpec((None, N_MEM, W_M), per_b),
            pl.BlockSpec((D_MIX, D_MODEL), const),
        ],
        out_specs=pl.BlockSpec((None, tm, D_MODEL), tok),
        out_shape=jax.ShapeDtypeStruct((b, s, D_MODEL), F32),
        compiler_params=pltpu.CompilerParams(
            dimension_semantics=("arbitrary", "arbitrary"), vmem_limit_bytes=VMEM_LIMIT),
        name="mix_out",
    )(x, oa, hb, qm, gm, mem_k, mem_vt, w_out_bf)


def _mixer_layer(x, past_kv, mlstm_state, mem_k, mem_vt, lam_init, p):
    b, s, _ = x.shape
    n = b * s
    tm = min(512, n)
    qaug, k_new, v_new, ga, mb, gl, og, qm, gm, k4, v4 = _in_proj(
        x.reshape(n, D_MODEL), tm, p["g_norm"], p["w_perm"], p["g_mat"],
        p["g_qa"], p["g_ka"], p["g_qm"], p["b_if"])
    k_new = k_new.reshape(b, s, W_A)
    v_new = v_new.reshape(b, s, W_A)
    qaug = qaug.reshape(b, s, 2 * W_A)
    ga = ga.reshape(b, s, W_A)

    if past_kv is None:
        oa = _prompt_attn(qaug, k_new, v_new, ga, p["lam_vecs"], p["g_subln"], lam_init,
                          tile=256)
    else:
        oa = _sample_attn(qaug, past_kv[0], past_kv[1], k_new, v_new, ga,
                          p["lam_vecs"], p["g_subln"], lam_init)

    c0, n0, m0 = mlstm_state
    s_pad = -(-s // CHUNK) * CHUNK
    mb3 = mb.reshape(b, s, 3 * W_B)
    gl3 = gl.reshape(b, s, LANES)
    og3 = og.reshape(b, s, 2 * W_B)
    if s_pad != s:
        pad = ((0, 0), (0, s_pad - s), (0, 0))
        mb3, gl3, og3 = jnp.pad(mb3, pad), jnp.pad(gl3, pad), jnp.pad(og3, pad)
    valid = CHUNK if s_pad == s else s
    hb, c_new, n_row, m_row = _mlstm(
        mb3, gl3, og3, p["g_mh"], c0, n0.reshape(b, 1, W_B),
        jnp.repeat(m0, DH_B, axis=-1).reshape(b, 1, W_B), valid, n_seq=min(b, 16))
    hb = hb[:, :s]
    new_state = (c_new, n_row.reshape(b, H_B, DH_B), m_row.reshape(b, H_B, DH_B)[:, :, 0])

    y = _mix_out(x, oa, hb, qm.reshape(b, s, W_M), gm.reshape(b, s, W_M), mem_k, mem_vt,
                 p["w_out"], tm=min(1024, s))
    return (y, k4.reshape(b, s, H_A, DV_A), v4.reshape(b, s, H_A, DV_A), new_state)


def kernel(x_prompt, x_sample, cache_attn_k, cache_attn_v, state_mlstm_C, state_mlstm_n,
           state_mlstm_m, cache_mem_k, cache_mem_v, mem_prompt, g_norm, w_in, w_out, g_qa,
           g_ka, lam_q1, lam_k1, lam_q2, lam_k2, g_subln, b_i, b_f, g_mh, g_qm, g_km, g_mem,
           w_mk, w_mv):
    depth = w_in.shape[0]
    bp = x_prompt.shape[0]
    bs = x_sample.shape[0]
    past = cache_attn_k.shape[2]
    gi = lax.broadcasted_iota(jnp.int32, (MXU_DIM, MXU_DIM), 0) // DH_B
    gj = lax.broadcasted_iota(jnp.int32, (MXU_DIM, MXU_DIM), 1) // DH_B
    g_mat = jnp.where(gi == gj, 1.0 / DH_B, 0.0).astype(BF16)

    xp, xs = x_prompt, x_sample
    outs = {name: [] for name in ("pk", "pv", "pC", "pn", "pm", "pmk", "pmv",
                                  "sk", "sv", "sC", "sn", "sm")}
    for l in range(depth):
        lam_init = 0.8 - 0.6 * math.exp(-0.3 * l)
        w = w_in[l]
        w_perm = jnp.concatenate(
            [w[:, :4 * W_A + 4 * W_B], w[:, 4 * W_A + 4 * W_B + 2 * H_B:],
             w[:, 4 * W_A + 4 * W_B:4 * W_A + 4 * W_B + 2 * H_B],
             jnp.zeros((D_MODEL, LANES - 2 * H_B), F32)], axis=1).astype(BF16)
        p = {
            "g_norm": g_norm[l].reshape(1, D_MODEL),
            "w_perm": w_perm,
            "g_mat": g_mat,
            "g_qa": jnp.tile(g_qa[l], 2 * H_A).reshape(1, W_A),
            "g_ka": jnp.tile(g_ka[l], 2 * H_A).reshape(1, W_A),
            "g_qm": jnp.tile(g_qm[l], H_M).reshape(1, W_M),
            "b_if": jnp.concatenate(
                [b_i[l], b_f[l], jnp.zeros((LANES - 2 * H_B,), F32)]).reshape(1, LANES),
            "lam_vecs": tuple(v[l].reshape(1, DK_A) for v in (lam_q1, lam_k1, lam_q2, lam_k2)),
            "g_subln": g_subln[l].reshape(1, DV_A),
            "g_mh": jnp.tile(g_mh[l], H_B).reshape(1, W_B),
            "w_out": w_out[l].astype(BF16),
        }
        mk, mkt, mvt = _memory_kv(mem_prompt, g_mem[l], w_mk[l], w_mv[l], g_km[l], g_mat)
        zero_state = (jnp.zeros((bp, H_B, DH_B, DH_B), F32), jnp.zeros((bp, H_B, DH_B), F32),
                      jnp.zeros((bp, H_B), F32))
        xp, k_p, v_p, st_p = _mixer_layer(xp, None, zero_state, mk, mvt, lam_init, p)
        xs, k_s, v_s, st_s = _mixer_layer(
            xs, (cache_attn_k[l], cache_attn_v[l]),
            (state_mlstm_C[l], state_mlstm_n[l], state_mlstm_m[l]),
            cache_mem_k[l].reshape(bs, N_MEM, W_M),
            cache_mem_v[l].transpose(0, 2, 3, 1).reshape(bs, W_M, N_MEM),
            lam_init, p)

        def tokens_first(t):
            return t.reshape(bp, H_M, DH_M, N_MEM).transpose(0, 3, 1, 2)

        outs["pk"].append(k_p); outs["pv"].append(v_p)
        outs["pC"].append(st_p[0]); outs["pn"].append(st_p[1]); outs["pm"].append(st_p[2])
        outs["pmk"].append(tokens_first(mkt))
        outs["pmv"].append(tokens_first(mvt))
        outs["sk"].append(k_s); outs["sv"].append(v_s)
        outs["sC"].append(st_s[0]); outs["sn"].append(st_s[1]); outs["sm"].append(st_s[2])
    stk = {name: jnp.stack(vals) for name, vals in outs.items()}
    return (xp, xs, stk["pk"], stk["pv"], stk["pC"], stk["pn"], stk["pm"], stk["pmk"],
            stk["pmv"], stk["sk"], stk["sv"], stk["sC"], stk["sn"], stk["sm"])
```

```python
import functools
import math

import jax
import jax.numpy as jnp
import numpy as np
from jax import lax
from jax.experimental import pallas as pl
from jax.experimental.pallas import tpu as pltpu

F32 = jnp.float32
BF16 = jnp.bfloat16

D_MODEL = 1024
CHUNK = 64
N_MEM = 256
H_A, DK_A = 4, 64
DV_A = 2 * DK_A
W_A = H_A * DV_A
H_B, DH_B = 4, 64
W_B = H_B * DH_B
H_M, DH_M = 4, 64
W_M = H_M * DH_M
D_MIX = W_A + W_B + W_M
EPS = 1e-6
ALIBI_SLOPES = tuple(2.0 ** (-8.0 * (h + 1) / H_A) for h in range(H_A))
N_IN = 4 * W_A + 5 * W_B + 2 * H_B + 2 * W_M

LANES = 128
MXU_DIM = 256
BF16_SUBLANES = 16
VT_ROWS = DV_A + BF16_SUBLANES
NEG_BIG = -1e30
LOG2E = math.log2(math.e)


def _bf16_pieces(value, n=3):
    pieces, rest = [], np.float32(value)
    for _ in range(n):
        piece = np.float32(rest.astype(jnp.bfloat16))
        pieces.append(float(piece))
        rest = np.float32(rest - piece)
    return tuple(pieces)


LOG2E_BF16_PIECES = _bf16_pieces(LOG2E)

C_QA, C_KA, C_VA, C_GA = 0, W_A, 2 * W_A, 3 * W_A
C_QB = 4 * W_A
C_KB, C_VB, C_OB, C_GB = C_QB + W_B, C_QB + 2 * W_B, C_QB + 3 * W_B, C_QB + 4 * W_B
C_QM = C_GB + W_B
C_GM = C_QM + W_M
C_IF = C_GM + W_M
N_IN_PAD = C_IF + LANES

VMEM_LIMIT = 56 * 1024 * 1024


def _dot(a, b):
    return jnp.dot(a, b, preferred_element_type=F32)


def _dot_nt(a, b):
    return lax.dot_general(a, b, (((1,), (1,)), ((), ())), preferred_element_type=F32)


def _split3(x):
    hi = x.astype(BF16)
    r1 = x - hi.astype(F32)
    mid = r1.astype(BF16)
    lo = (r1 - mid.astype(F32)).astype(BF16)
    return hi, mid, lo


def _group_mean_sq(z, g_mat):
    zz = (z * z).astype(BF16)
    parts = [_dot(zz[:, c:c + MXU_DIM], g_mat) for c in range(0, z.shape[1], MXU_DIM)]
    return parts[0] if len(parts) == 1 else jnp.concatenate(parts, axis=1)


def _group_rms_norm(z, gain_row, g_mat):
    return z * lax.rsqrt(_group_mean_sq(z, g_mat) + EPS) * gain_row


def _log_sigmoid(u):
    return -(jnp.maximum(-u, 0.0) + jnp.log1p(jnp.exp(-jnp.abs(u))))


def _silu(u):
    return u * jax.nn.sigmoid(u)


def _memory_kv_kernel(mem_ref, gmem_ref, wk_ref, wv_ref, gkm_ref, gmat_ref,
                      mk_ref, mkt_ref, mvt_ref, mv_scr):
    x = mem_ref[...]
    ms = jnp.mean(x * x, axis=-1, keepdims=True)
    hm = (x * lax.rsqrt(ms + EPS) * gmem_ref[...]).astype(BF16)
    mk_ref[...] = _group_rms_norm(_dot(hm, wk_ref[...]), gkm_ref[...], gmat_ref[...])
    mv_scr[...] = _dot(hm, wv_ref[...])
    mkt_ref[...] = mk_ref[...].T
    mvt_ref[...] = mv_scr[...].T


def _memory_kv(mem, g_mem, w_mk, w_mv, g_km, g_mat):
    b, n, d = mem.shape
    row = lambda i: (i, 0, 0)
    const2 = lambda i: (0, 0)
    return pl.pallas_call(
        _memory_kv_kernel,
        grid=(b,),
        in_specs=[
            pl.BlockSpec((None, n, d), row),
            pl.BlockSpec((1, d), const2),
            pl.BlockSpec((d, W_M), const2),
            pl.BlockSpec((d, W_M), const2),
            pl.BlockSpec((1, W_M), const2),
            pl.BlockSpec((MXU_DIM, MXU_DIM), const2),
        ],
        out_specs=[pl.BlockSpec((None, n, W_M), row), pl.BlockSpec((None, W_M, n), row),
                   pl.BlockSpec((None, W_M, n), row)],
        out_shape=[jax.ShapeDtypeStruct((b, n, W_M), F32),
                   jax.ShapeDtypeStruct((b, W_M, n), F32),
                   jax.ShapeDtypeStruct((b, W_M, n), F32)],
        scratch_shapes=[pltpu.VMEM((n, W_M), F32)],
        compiler_params=pltpu.CompilerParams(
            dimension_semantics=("arbitrary",), vmem_limit_bytes=VMEM_LIMIT),
        name="memory_kv",
    )(mem, g_mem.reshape(1, d), w_mk.astype(BF16), w_mv.astype(BF16),
      jnp.tile(g_km, H_M).reshape(1, W_M), g_mat)


def _in_proj_kernel(x_ref, gn_ref, w_ref, gmat_ref, gqa_ref, gka_ref, gqm_ref, bif_ref,
                    qaug_ref, k_ref, v_ref, ga_ref, mb_ref, gl_ref, og_ref, qm_ref, gm_ref,
                    k4_ref, v4_ref):
    x = x_ref[...]
    h = (x * gn_ref[...]).astype(BF16)
    inv_rms = lax.rsqrt(jnp.mean(x * x, axis=-1, keepdims=True) + EPS)
    g_mat = gmat_ref[...]

    def proj(start, width):
        return _dot(h, w_ref[:, start:start + width]) * inv_rms

    qn = _group_rms_norm(proj(C_QA, W_A), gqa_ref[...], g_mat) * (DK_A ** -0.5 * LOG2E)
    lane = lax.broadcasted_iota(jnp.int32, (x.shape[0], LANES), 1)
    aug = jnp.zeros((x.shape[0], LANES), F32)
    for rep in range(2):
        for piece, val in enumerate(LOG2E_BF16_PIECES):
            aug = jnp.where(lane == DK_A + 3 * rep + piece, val, aug)
    for hd in range(H_A):
        slab = qn[:, hd * DV_A:(hd + 1) * DV_A]
        q1 = jnp.where(lane < DK_A, slab, aug)
        q2 = jnp.where(lane < DK_A, pltpu.roll(slab, DK_A, 1), aug)
        qaug_ref[:, (2 * hd) * LANES:(2 * hd + 1) * LANES] = q1.astype(BF16)
        qaug_ref[:, (2 * hd + 1) * LANES:(2 * hd + 2) * LANES] = q2.astype(BF16)

    kn = _group_rms_norm(proj(C_KA, W_A), gka_ref[...], g_mat)
    k_ref[...] = kn.astype(BF16)

    def cache_copy(dst_ref, src, hd, later):
        zero = pltpu.bitcast(
            lax.shift_right_logical(pltpu.bitcast(later[:, 0:DV_A], jnp.uint32), jnp.uint32(32)),
            F32)
        dst_ref[:, hd, :] = src[:, hd * DV_A:(hd + 1) * DV_A] + zero

    vn = proj(C_VA, W_A)
    v_ref[...] = vn.astype(BF16)
    cache_copy(k4_ref, kn, 0, vn)
    z = proj(C_GA, W_A)
    ga_ref[...] = _silu(z)
    cache_copy(k4_ref, kn, 1, z)
    z = proj(C_QB, W_B)
    mb_ref[:, 0:W_B] = z
    cache_copy(k4_ref, kn, 2, z)
    z = proj(C_KB, W_B)
    mb_ref[:, W_B:2 * W_B] = z * (DH_B ** -0.5)
    cache_copy(k4_ref, kn, 3, z)
    z = proj(C_VB, W_B)
    mb_ref[:, 2 * W_B:3 * W_B] = z
    cache_copy(v4_ref, vn, 0, z)
    z = proj(C_OB, W_B)
    og_ref[:, 0:W_B] = jax.nn.sigmoid(z)
    cache_copy(v4_ref, vn, 1, z)
    z = proj(C_GB, W_B)
    og_ref[:, W_B:2 * W_B] = _silu(z)
    cache_copy(v4_ref, vn, 2, z)
    z = proj(C_QM, W_M)
    qm_ref[...] = (_group_rms_norm(z, gqm_ref[...], g_mat)
                   * (DH_M ** -0.5 * LOG2E)).astype(BF16)
    cache_copy(v4_ref, vn, 3, z)
    gm_ref[...] = _silu(proj(C_GM, W_M))

    u = proj(C_IF, LANES) + bif_ref[...]
    gl_ref[...] = jnp.where(lane < H_B, u, _log_sigmoid(u))


def _in_proj(x2d, tm, g_norm, w_perm, g_mat, g_qa, g_ka, g_qm, b_if):
    n = x2d.shape[0]
    row = lambda i: (i, 0)
    const = lambda i: (0, 0)
    widths = (2 * W_A, W_A, W_A, W_A, 3 * W_B, LANES, 2 * W_B, W_M, W_M)
    dtypes = (BF16, BF16, BF16) + (F32,) * 4 + (BF16, F32)
    return pl.pallas_call(
        _in_proj_kernel,
        grid=(n // tm,),
        in_specs=[
            pl.BlockSpec((tm, D_MODEL), row),
            pl.BlockSpec((1, D_MODEL), const),
            pl.BlockSpec((D_MODEL, N_IN_PAD), const),
            pl.BlockSpec((MXU_DIM, MXU_DIM), const),
            pl.BlockSpec((1, W_A), const),
            pl.BlockSpec((1, W_A), const),
            pl.BlockSpec((1, W_M), const),
            pl.BlockSpec((1, LANES), const),
        ],
        out_specs=[pl.BlockSpec((tm, w), row) for w in widths]
        + [pl.BlockSpec((tm, H_A, DV_A), lambda i: (i, 0, 0))] * 2,
        out_shape=[jax.ShapeDtypeStruct((n, w), dt) for w, dt in zip(widths, dtypes)]
        + [jax.ShapeDtypeStruct((n, H_A, DV_A), F32)] * 2,
        compiler_params=pltpu.CompilerParams(
            dimension_semantics=("arbitrary",), vmem_limit_bytes=VMEM_LIMIT),
        name="in_proj",
    )(x2d, g_norm, w_perm, g_mat, g_qa, g_ka, g_qm, b_if)


def _lambda_value(lq1_ref, lk1_ref, lq2_ref, lk2_ref, lam_init):
    s1 = jnp.sum(lq1_ref[...] * lk1_ref[...], axis=-1, keepdims=True)
    s2 = jnp.sum(lq2_ref[...] * lk2_ref[...], axis=-1, keepdims=True)
    return jnp.exp(s1) - jnp.exp(s2) + lam_init


def _subln(o, gsub_row, lam_init):
    ms = jnp.mean(o * o, axis=-1, keepdims=True)
    return o * lax.rsqrt(ms + EPS) * gsub_row * (1.0 - lam_init)


def _prompt_attn_kernel(lq1_ref, lk1_ref, lq2_ref, lk2_ref, gsub_ref, q_ref, k_ref, v_ref,
                        ga_ref, o_ref, kaug_ref, vt_ref, m_ref, acc_ref, s_ref, qt_ref,
                        *, tile, lam_init):
    seq = k_ref.shape[0]
    n_tiles = seq // tile
    n_chain = 2 * H_A

    lane = lax.broadcasted_iota(jnp.int32, (seq, LANES), 1)
    pos = lax.broadcasted_iota(jnp.int32, (seq, LANES), 0)
    in_tile = (pos % tile).astype(F32)
    tile_base = (pos - pos % tile).astype(F32)
    for hd in range(H_A):
        slab = k_ref[:, hd * DV_A:(hd + 1) * DV_A].astype(F32)
        aug = jnp.where((lane >= DK_A) & (lane < DK_A + 3), ALIBI_SLOPES[hd] * in_tile,
                        jnp.where((lane >= DK_A + 3) & (lane < DK_A + 6),
                                  ALIBI_SLOPES[hd] * tile_base, 0.0))
        kaug_ref[2 * hd] = jnp.where(lane < DK_A, slab, aug).astype(BF16)
        kaug_ref[2 * hd + 1] = jnp.where(
            lane < DK_A, pltpu.roll(slab, DK_A, 1), aug).astype(BF16)
    for t in range(n_tiles):
        v_t = v_ref[t * tile:(t + 1) * tile, :].astype(F32).T.astype(BF16)
        for hd in range(H_A):
            vt_ref[t, hd, 0:DV_A, :] = v_t[hd * DV_A:(hd + 1) * DV_A, :]
            vt_ref[t, hd, DV_A:VT_ROWS, :] = jnp.ones((VT_ROWS - DV_A, tile), BF16)

    lam = _lambda_value(lq1_ref, lk1_ref, lq2_ref, lk2_ref, lam_init)

    key = lax.broadcasted_iota(jnp.int32, (tile, tile), 0)
    qry = lax.broadcasted_iota(jnp.int32, (tile, tile), 1)
    ahead = (key - qry).astype(F32)
    diag_bias = [jnp.where(key // CHUNK > qry // CHUNK, NEG_BIG,
                           jnp.where(key > qry, (-2.0 * LOG2E * ALIBI_SLOPES[hd]) * ahead, 0.0))
                 for hd in range(H_A)]

    def rows(i):
        if isinstance(i, int):
            return pl.ds(i * tile, tile)
        return pl.ds(pl.multiple_of(i * tile, tile), tile)

    def scores(par, c, h, j):
        s_ref[par, c, h] = _dot(kaug_ref[c, rows(j), :], qt_ref[par, c, h])

    def softmax_values(par, c, h, j, diagonal):
        s = s_ref[par, c, h]
        if diagonal:
            s = s + diag_bias[c // 2]
        mx = m_ref[par, c, h]
        mx_new = jnp.maximum(mx, jnp.max(s, axis=0, keepdims=True))
        p = jnp.exp2(s - mx_new)
        alpha = jnp.exp2(mx - mx_new)
        m_ref[par, c, h] = mx_new
        acc_ref[par, c, h] = (alpha * acc_ref[par, c, h]
                              + _dot(vt_ref[j, c // 2], p.astype(BF16)))

    def begin_chain(par, c, duo):
        for h in range(2):
            q = q_ref[rows(2 * duo + h), c * LANES:(c + 1) * LANES]
            qt_ref[par, c, h] = q.astype(F32).T.astype(BF16)
            m_ref[par, c, h] = jnp.full((1, tile), NEG_BIG, F32)
            acc_ref[par, c, h] = jnp.zeros((VT_ROWS, tile), F32)
            scores(par, c, h, 0)

    def finish_head(par, hd, duo):
        for h in range(2):
            a1 = acc_ref[par, 2 * hd, h]
            a2 = acc_ref[par, 2 * hd + 1, h]
            o1 = a1[0:DV_A] * (1.0 / a1[DV_A:DV_A + 1])
            o2 = a2[0:DV_A] * (1.0 / a2[DV_A:DV_A + 1])
            o = (o1 - lam * o2).T
            gate = ga_ref[rows(2 * duo + h), hd * DV_A:(hd + 1) * DV_A]
            o_ref[rows(2 * duo + h), hd * DV_A:(hd + 1) * DV_A] = (
                _subln(o, gsub_ref[...], lam_init) * gate).astype(BF16)

    for c in range(n_chain):
        begin_chain(0, c, 0)

    def duo_pair(pair, carry):
        for par in range(2):
            duo = 2 * pair + par
            first_diag = 2 * duo

            def past_tile(j, inner, par=par):
                for c in range(n_chain):
                    for h in range(2):
                        softmax_values(par, c, h, j, False)
                        scores(par, c, h, j + 1)
                return inner

            lax.fori_loop(0, first_diag, past_tile, 0)
            for c in range(n_chain):
                softmax_values(par, c, 0, first_diag, True)
                softmax_values(par, c, 1, first_diag, False)
                scores(par, c, 1, first_diag + 1)
            nxt = lax.rem(duo + 1, n_tiles // 2)
            for c in range(n_chain):
                softmax_values(par, c, 1, first_diag + 1, True)
                begin_chain(1 - par, c, nxt)
                if c % 2 == 1:
                    finish_head(par, c // 2, duo)
        return carry

    lax.fori_loop(0, n_tiles // 4, duo_pair, 0)


def _prompt_attn(qaug, k, v, ga, lam_vecs, g_subln, lam_init, tile):
    b, s, _ = k.shape
    assert (s // tile) % 4 == 0
    const = lambda bi: (0, 0)
    per_b = lambda bi: (bi, 0, 0)
    kern = functools.partial(_prompt_attn_kernel, tile=tile, lam_init=lam_init)
    return pl.pallas_call(
        kern,
        grid=(b,),
        in_specs=[pl.BlockSpec((1, DK_A), const)] * 4 + [
            pl.BlockSpec((1, DV_A), const),
            pl.BlockSpec((None, s, 2 * W_A), per_b),
            pl.BlockSpec((None, s, W_A), per_b),
            pl.BlockSpec((None, s, W_A), per_b),
            pl.BlockSpec((None, s, W_A), per_b),
        ],
        out_specs=pl.BlockSpec((None, s, W_A), per_b),
        out_shape=jax.ShapeDtypeStruct((b, s, W_A), BF16),
        scratch_shapes=[
            pltpu.VMEM((2 * H_A, s, LANES), BF16),
            pltpu.VMEM((s // tile, H_A, VT_ROWS, tile), BF16),
            pltpu.VMEM((2, 2 * H_A, 2, 1, tile), F32),
            pltpu.VMEM((2, 2 * H_A, 2, VT_ROWS, tile), F32),
            pltpu.VMEM((2, 2 * H_A, 2, tile, tile), F32),
            pltpu.VMEM((2, 2 * H_A, 2, LANES, tile), BF16),
        ],
        compiler_params=pltpu.CompilerParams(
            dimension_semantics=("arbitrary",), vmem_limit_bytes=VMEM_LIMIT),
        name="diff_attn_prompt",
    )(*lam_vecs, g_subln, qaug, k, v, ga)


def _sample_attn_kernel(lq1_ref, lk1_ref, lq2_ref, lk2_ref, gsub_ref, q_ref, kc_ref, vc_ref,
                        kn_ref, vn_ref, ga_ref, o_ref, *, lam_init):
    hd = pl.program_id(1)
    sq = q_ref.shape[0]
    past = kc_ref.shape[0] // H_A
    lam = _lambda_value(lq1_ref, lk1_ref, lq2_ref, lk2_ref, lam_init)
    lane = lax.broadcasted_iota(jnp.int32, (sq, LANES), 1)
    qpos_c = past + lax.broadcasted_iota(jnp.int32, (sq, past), 0)
    dist_c = jnp.abs(qpos_c - lax.broadcasted_iota(jnp.int32, (sq, past), 1)).astype(F32)
    dist_n = jnp.abs(lax.broadcasted_iota(jnp.int32, (sq, sq), 0)
                     - lax.broadcasted_iota(jnp.int32, (sq, sq), 1)).astype(F32)
    slope = jnp.float32(ALIBI_SLOPES[H_A - 1])
    for h in range(H_A - 1):
        slope = jnp.where(hd == h, jnp.float32(ALIBI_SLOPES[h]), slope)
    kc = kc_ref[pl.ds(hd, past, stride=H_A), :].astype(BF16)
    kn = kn_ref[...]
    vc = vc_ref[pl.ds(hd, past, stride=H_A), :].astype(BF16)
    vn = vn_ref[...]
    outs = []
    for m in range(2):
        q = q_ref[:, m * LANES:(m + 1) * LANES].astype(F32)
        if m == 0:
            q = jnp.where(lane < DK_A, q, 0.0)
        else:
            q = jnp.where(lane >= DK_A, pltpu.roll(q, DK_A, 1), 0.0)
        q = q.astype(BF16)
        s_c = _dot_nt(q, kc) - (slope * LOG2E) * dist_c
        s_n = _dot_nt(q, kn) - (slope * LOG2E) * dist_n
        mx = jnp.maximum(jnp.max(s_c, axis=-1, keepdims=True),
                         jnp.max(s_n, axis=-1, keepdims=True))
        p_c = jnp.exp2(s_c - mx)
        p_n = jnp.exp2(s_n - mx)
        l = jnp.sum(p_c, axis=-1, keepdims=True) + jnp.sum(p_n, axis=-1, keepdims=True)
        acc = _dot(p_c.astype(BF16), vc) + _dot(p_n.astype(BF16), vn)
        outs.append(acc * (1.0 / l))
    o = outs[0] - lam * outs[1]
    o_ref[...] = (_subln(o, gsub_ref[...], lam_init) * ga_ref[...]).astype(BF16)


def _sample_attn(qaug, k_cache, v_cache, k_new, v_new, ga, lam_vecs, g_subln, lam_init):
    b, past = k_cache.shape[:2]
    sq = k_new.shape[1]
    assert (past + sq - 1) // CHUNK <= past // CHUNK
    k_cache = k_cache.reshape(b, past * H_A, DV_A)
    v_cache = v_cache.reshape(b, past * H_A, DV_A)
    const = lambda bi, hi: (0, 0)
    head = lambda bi, hi: (bi, 0, hi)
    cache = lambda bi, hi: (bi, 0, 0)
    kern = functools.partial(_sample_attn_kernel, lam_init=lam_init)
    return pl.pallas_call(
        kern,
        grid=(b, H_A),
        in_specs=[pl.BlockSpec((1, DK_A), const)] * 4 + [
            pl.BlockSpec((1, DV_A), const),
            pl.BlockSpec((None, sq, 2 * LANES), head),
            pl.BlockSpec((None, past * H_A, DV_A), cache),
            pl.BlockSpec((None, past * H_A, DV_A), cache),
            pl.BlockSpec((None, sq, DV_A), head),
            pl.BlockSpec((None, sq, DV_A), head),
            pl.BlockSpec((None, sq, DV_A), head),
        ],
        out_specs=pl.BlockSpec((None, sq, DV_A), head),
        out_shape=jax.ShapeDtypeStruct((b, sq, W_A), BF16),
        compiler_params=pltpu.CompilerParams(
            dimension_semantics=("arbitrary", "arbitrary"), vmem_limit_bytes=VMEM_LIMIT),
        name="diff_attn_sample",
    )(*lam_vecs, g_subln, qaug, k_cache, v_cache, k_new, v_new, ga)


def _mlstm_kernel(mb_ref, gl_ref, og_ref, gmh_ref, c0_ref, n0_ref, m0_ref,
                  h_ref, c_out_ref, n_ref, m_ref, c_ref, *, valid):
    c_idx = pl.program_id(1)
    L = CHUNK
    W = W_B
    G = mb_ref.shape[0]
    R = G * L

    def head_block(hd):
        return slice(hd * DH_B, (hd + 1) * DH_B)

    @pl.when(c_idx == 0)
    def _load_state():
        c_ref[...] = jnp.zeros(c_ref.shape, F32)
        for g in range(G):
            for hd in range(H_B):
                c_ref[g, head_block(hd), head_block(hd)] = c0_ref[g, hd]
        n_ref[...] = n0_ref[...]
        m_ref[...] = m0_ref[...]

    lane_head = lax.broadcasted_iota(jnp.int32, (R, W), 1) // DH_B
    row_t = lax.broadcasted_iota(jnp.int32, (R, W), 0) % L
    lane_s = lax.broadcasted_iota(jnp.int32, (R, W), 1) % DH_B
    wi = lax.broadcasted_iota(jnp.int32, (W, W), 0)
    wj = lax.broadcasted_iota(jnp.int32, (W, W), 1)
    blk = wi // DH_B == wj // DH_B
    ones_bd = blk.astype(BF16)
    ei = lax.broadcasted_iota(jnp.int32, (LANES, 2 * W), 0)
    ej = lax.broadcasted_iota(jnp.int32, (LANES, 2 * W), 1)
    expand = (ei == ej // DH_B).astype(BF16)
    tri = (lax.broadcasted_iota(jnp.int32, (L, L), 1)
           <= lax.broadcasted_iota(jnp.int32, (L, L), 0)).astype(BF16)
    nar_lane = lax.broadcasted_iota(jnp.int32, (R, LANES), 1)
    nar_t = lax.broadcasted_iota(jnp.int32, (R, LANES), 0) % L

    def per_seq(x):
        return x.reshape(G, L, x.shape[-1])

    def stacked(x):
        return x.reshape(R, x.shape[-1])

    def seq_rows(x, g):
        return x[g * L:(g + 1) * L]

    def exact_matmul(xs, mat):
        n = xs[0].shape[0]
        pieces = [part for x in xs for part in _split3(x)]
        out = _dot(jnp.concatenate(pieces, axis=0), mat)
        return [out[(3 * i) * n:(3 * i + 1) * n] + out[(3 * i + 1) * n:(3 * i + 2) * n]
                + out[(3 * i + 2) * n:(3 * i + 3) * n] for i in range(len(xs))]

    def head_max(x):
        out = jnp.zeros((R, W), F32)
        for hd in range(H_B):
            mx = jnp.max(jnp.where(lane_head == hd, x, NEG_BIG), axis=-1, keepdims=True)
            out = jnp.where(lane_head == hd, mx, out)
        return out

    q = stacked(mb_ref[:, :, 0:W])
    k = stacked(mb_ref[:, :, W:2 * W])
    v = stacked(mb_ref[:, :, 2 * W:3 * W])
    gl = jnp.where(nar_lane < 2 * H_B, stacked(gl_ref[...]), 0.0)
    gl = jnp.where(nar_t < valid, gl, jnp.where(nar_lane < H_B, NEG_BIG, 0.0))
    cums = []
    for g in range(G):
        out = _dot(tri, jnp.concatenate(_split3(seq_rows(gl, g)), axis=1))
        cums.append(out[:, 0:LANES] + out[:, LANES:2 * LANES] + out[:, 2 * LANES:3 * LANES])
    narrow = jnp.where(nar_lane < H_B, gl, jnp.concatenate(cums, axis=0))
    (wide,) = exact_matmul([narrow], expand)
    ig_all = wide[:, 0:W]
    b_all = wide[:, W:2 * W]

    r_all = b_all - ig_all
    r_row = jnp.sum(per_seq(jnp.where(lane_s == row_t, r_all, 0.0)), axis=1, keepdims=True)
    m0_row = m_ref[...]
    n0_row = n_ref[...]
    b_seq = per_seq(b_all)
    d_mat = jnp.where(lane_s <= row_t, stacked(b_seq - r_row), NEG_BIG)
    g_all = stacked(b_seq + m0_row)
    m_all = jnp.maximum(g_all, head_max(d_mat))
    d_w = jnp.exp(d_mat - m_all)
    g_w = jnp.exp(g_all - m_all)

    q_bf = q.astype(BF16)
    zero_bf = jnp.zeros((W, W), BF16)
    scores, inter, v_bds = [], [], []
    for g in range(G):
        k_t = seq_rows(k, g).T.astype(BF16)
        k_bd = jnp.where(blk, jnp.concatenate([k_t] * H_B, axis=1), zero_bf)
        v_bds.append(jnp.where(
            blk, jnp.concatenate([seq_rows(v, g).astype(BF16)] * H_B, axis=0), zero_bf))
        c0_bd = jnp.where(blk, c_ref[g].astype(BF16), zero_bf)
        scores.append(_dot(seq_rows(q_bf, g), k_bd))
        inter.append(_dot(seq_rows(q_bf, g), c0_bd))
    a_mat = jnp.concatenate(scores, axis=0) * d_w
    a_bf = a_mat.astype(BF16)
    intra = [_dot(seq_rows(a_bf, g), v_bds[g]) for g in range(G)]
    num = g_w * jnp.concatenate(inter, axis=0) + jnp.concatenate(intra, axis=0)

    qn_bf = stacked(per_seq(q) * n0_row).astype(BF16)
    sums = _dot(jnp.concatenate([qn_bf, a_bf], axis=0), ones_bd)
    den = g_w * sums[0:R] + sums[R:2 * R]
    h = num / jnp.maximum(jnp.abs(den), jnp.exp(-m_all))
    h_ms = _dot((h * h).astype(BF16), ones_bd) * (1.0 / DH_B)
    h = h * lax.rsqrt(h_ms + EPS) * gmh_ref[...]
    og = stacked(og_ref[...])
    h_ref[...] = per_seq((h * og[:, 0:W] * og[:, W:2 * W]).astype(BF16))

    m_last = per_seq(m_all)[:, L - 1:L, :]
    b_last = b_seq[:, L - 1:L, :]
    w_s = jnp.exp(stacked(b_last - b_seq + per_seq(ig_all) - m_last))
    decay = jnp.exp(b_last + m0_row - m_last)
    kw = k * w_s
    v_bf = v.astype(BF16)
    for g in range(G):
        c_ref[g] = decay[g] * c_ref[g] + _dot(seq_rows(kw, g).T.astype(BF16), seq_rows(v_bf, g))
    n_ref[...] = decay * n0_row + jnp.sum(per_seq(kw), axis=1, keepdims=True)
    m_ref[...] = m_last

    @pl.when(c_idx == pl.num_programs(1) - 1)
    def _store_state():
        for g in range(G):
            for hd in range(H_B):
                c_out_ref[g, hd] = c_ref[g, head_block(hd), head_block(hd)]


def _mlstm(mb, gl, og, g_mh_row, c0, n0_row, m0_row, valid, n_seq):
    b, s, _ = mb.shape
    nc = s // CHUNK
    tok = lambda bi, ci: (bi, ci, 0)
    st = lambda bi, ci: (bi, 0, 0)
    st4 = lambda bi, ci: (bi, 0, 0, 0)
    kern = functools.partial(_mlstm_kernel, valid=valid)
    return pl.pallas_call(
        kern,
        grid=(b // n_seq, nc),
        in_specs=[
            pl.BlockSpec((n_seq, CHUNK, 3 * W_B), tok),
            pl.BlockSpec((n_seq, CHUNK, LANES), tok),
            pl.BlockSpec((n_seq, CHUNK, 2 * W_B), tok),
            pl.BlockSpec((1, W_B), lambda bi, ci: (0, 0)),
            pl.BlockSpec((n_seq, H_B, DH_B, DH_B), st4),
            pl.BlockSpec((n_seq, 1, W_B), st),
            pl.BlockSpec((n_seq, 1, W_B), st),
        ],
        out_specs=[
            pl.BlockSpec((n_seq, CHUNK, W_B), tok),
            pl.BlockSpec((n_seq, H_B, DH_B, DH_B), st4),
            pl.BlockSpec((n_seq, 1, W_B), st),
            pl.BlockSpec((n_seq, 1, W_B), st),
        ],
        out_shape=[
            jax.ShapeDtypeStruct((b, s, W_B), BF16),
            jax.ShapeDtypeStruct((b, H_B, DH_B, DH_B), F32),
            jax.ShapeDtypeStruct((b, 1, W_B), F32),
            jax.ShapeDtypeStruct((b, 1, W_B), F32),
        ],
        scratch_shapes=[pltpu.VMEM((n_seq, W_B, W_B), F32)],
        compiler_params=pltpu.CompilerParams(
            dimension_semantics=("arbitrary", "arbitrary"), vmem_limit_bytes=VMEM_LIMIT),
        name="mlstm",
    )(mb, gl, og, g_mh_row, c0, n0_row, m0_row)


def _mix_out_kernel(x_ref, oa_ref, hb_ref, qm_ref, gm_ref, mk_ref, mvt_ref, wo_ref, y_ref,
                    *, sub_rows):
    tm = x_ref.shape[0]
    sub = min(tm, sub_rows)
    mk = mk_ref[...]
    mvt = mvt_ref[...]
    k_feat_head = lax.broadcasted_iota(jnp.int32, mk.shape, 1) // DH_M
    v_feat_head = lax.broadcasted_iota(jnp.int32, mvt.shape, 0) // DH_M
    k_of_head = [jnp.where(k_feat_head == hd, mk, 0.0).astype(BF16) for hd in range(H_M)]
    vt_of_head = [jnp.where(v_feat_head == hd, mvt, 0.0).astype(BF16) for hd in range(H_M)]

    def probs(s):
        p = jnp.exp2(s - jnp.max(s, axis=0, keepdims=True))
        return (p * (1.0 / jnp.sum(p, axis=0, keepdims=True))).astype(BF16)

    def sub_tile_stages(r0):
        rows = slice(r0, r0 + sub)
        qm = qm_ref[rows, :]
        if sub < LANES:
            qm = jnp.concatenate([qm, jnp.zeros((LANES - sub, W_M), BF16)], axis=0)
        s0 = _dot_nt(k_of_head[0], qm)
        y = x_ref[rows, :] + _dot(oa_ref[rows, :], wo_ref[0:W_A, :])
        yield
        s1 = _dot_nt(k_of_head[1], qm)
        p0 = probs(s0)
        y = y + _dot(hb_ref[rows, :], wo_ref[W_A:W_A + W_B, :])
        yield
        s2 = _dot_nt(k_of_head[2], qm)
        p1 = probs(s1)
        omt = _dot(vt_of_head[0], p0)
        yield
        s3 = _dot_nt(k_of_head[3], qm)
        p2 = probs(s2)
        omt = omt + _dot(vt_of_head[1], p1)
        yield
        p3 = probs(s3)
        omt = omt + _dot(vt_of_head[2], p2)
        omt = omt + _dot(vt_of_head[3], p3)
        yield
        om = omt.T[0:sub] * gm_ref[rows, :]
        y_ref[rows, :] = y + _dot(om.astype(BF16), wo_ref[W_A + W_B:D_MIX, :])
        yield

    n_stage = 6
    gens = [sub_tile_stages(r0) for r0 in range(0, tm, sub)]
    for step in range(n_stage + len(gens) - 1):
        for idx, gen in enumerate(gens):
            if 0 <= step - idx < n_stage:
                next(gen)


def _mix_out(x, oa, hb, qm, gm, mem_k, mem_vt, w_out_bf, tm):
    b, s, _ = x.shape
    tok = lambda bi, ti: (bi, ti, 0)
    per_b = lambda bi, ti: (bi, 0, 0)
    const = lambda bi, ti: (0, 0)
    return pl.pallas_call(
        functools.partial(_mix_out_kernel, sub_rows=MXU_DIM),
        grid=(b, s // tm),
        in_specs=[
            pl.BlockSpec((None, tm, D_MODEL), tok),
            pl.BlockSpec((None, tm, W_A), tok),
            pl.BlockSpec((None, tm, W_B), tok),
            pl.BlockSpec((None, tm, W_M), tok),
            pl.BlockSpec((None, tm, W_M), tok),
            pl.BlockSpec((None, N_MEM, W_M), per_b),
            pl.BlockSpec((None, N_MEM, W_M), per_b),
            pl.BlockSpec((D_MIX, D_MODEL), const),
        ],
        out_specs=pl.BlockSpec((None, tm, D_MODEL), tok),
        out_shape=jax.ShapeDtypeStruct((b, s, D_MODEL), F32),
        compiler_params=pltpu.CompilerParams(
            dimension_semantics=("arbitrary", "arbitrary"), vmem_limit_bytes=VMEM_LIMIT),
        name="mix_out",
    )(x, oa, hb, qm, gm, mem_k, mem_vt, w_out_bf)


def _mixer_layer(x, past_kv, mlstm_state, mem_k, mem_vt, lam_init, p):
    b, s, _ = x.shape
    n = b * s
    tm = min(512, n)
    qaug, k_new, v_new, ga, mb, gl, og, qm, gm, k4, v4 = _in_proj(
        x.reshape(n, D_MODEL), tm, p["g_norm"], p["w_perm"], p["g_mat"],
        p["g_qa"], p["g_ka"], p["g_qm"], p["b_if"])
    k_new = k_new.reshape(b, s, W_A)
    v_new = v_new.reshape(b, s, W_A)
    qaug = qaug.reshape(b, s, 2 * W_A)
    ga = ga.reshape(b, s, W_A)

    if past_kv is None:
        oa = _prompt_attn(qaug, k_new, v_new, ga, p["lam_vecs"], p["g_subln"], lam_init,
                          tile=256)
    else:
        oa = _sample_attn(qaug, past_kv[0], past_kv[1], k_new, v_new, ga,
                          p["lam_vecs"], p["g_subln"], lam_init)

    c0, n0, m0 = mlstm_state
    s_pad = -(-s // CHUNK) * CHUNK
    mb3 = mb.reshape(b, s, 3 * W_B)
    gl3 = gl.reshape(b, s, LANES)
    og3 = og.reshape(b, s, 2 * W_B)
    if s_pad != s:
        pad = ((0, 0), (0, s_pad - s), (0, 0))
        mb3, gl3, og3 = jnp.pad(mb3, pad), jnp.pad(gl3, pad), jnp.pad(og3, pad)
    valid = CHUNK if s_pad == s else s
    hb, c_new, n_row, m_row = _mlstm(
        mb3, gl3, og3, p["g_mh"], c0, n0.reshape(b, 1, W_B),
        jnp.repeat(m0, DH_B, axis=-1).reshape(b, 1, W_B), valid, n_seq=min(b, 16))
    hb = hb[:, :s]
    new_state = (c_new, n_row.reshape(b, H_B, DH_B), m_row.reshape(b, H_B, DH_B)[:, :, 0])

    y = _mix_out(x, oa, hb, qm.reshape(b, s, W_M), gm.reshape(b, s, W_M), mem_k, mem_vt,
                 p["w_out"], tm=min(1024, s))
    return (y, k4.reshape(b, s, H_A, DV_A), v4.reshape(b, s, H_A, DV_A), new_state)


def kernel(x_prompt, x_sample, cache_attn_k, cache_attn_v, state_mlstm_C, state_mlstm_n,
           state_mlstm_m, cache_mem_k, cache_mem_v, mem_prompt, g_norm, w_in, w_out, g_qa,
           g_ka, lam_q1, lam_k1, lam_q2, lam_k2, g_subln, b_i, b_f, g_mh, g_qm, g_km, g_mem,
           w_mk, w_mv):
    depth = w_in.shape[0]
    bp = x_prompt.shape[0]
    bs = x_sample.shape[0]
    past = cache_attn_k.shape[2]
    gi = lax.broadcasted_iota(jnp.int32, (MXU_DIM, MXU_DIM), 0) // DH_B
    gj = lax.broadcasted_iota(jnp.int32, (MXU_DIM, MXU_DIM), 1) // DH_B
    g_mat = jnp.where(gi == gj, 1.0 / DH_B, 0.0).astype(BF16)

    xp, xs = x_prompt, x_sample
    outs = {name: [] for name in ("pk", "pv", "pC", "pn", "pm", "pmk", "pmv",
                                  "sk", "sv", "sC", "sn", "sm")}
    for l in range(depth):
        lam_init = 0.8 - 0.6 * math.exp(-0.3 * l)
        w = w_in[l]
        w_perm = jnp.concatenate(
            [w[:, :4 * W_A + 4 * W_B], w[:, 4 * W_A + 4 * W_B + 2 * H_B:],
             w[:, 4 * W_A + 4 * W_B:4 * W_A + 4 * W_B + 2 * H_B],
             jnp.zeros((D_MODEL, LANES - 2 * H_B), F32)], axis=1).astype(BF16)
        p = {
            "g_norm": g_norm[l].reshape(1, D_MODEL),
            "w_perm": w_perm,
            "g_mat": g_mat,
            "g_qa": jnp.tile(g_qa[l], 2 * H_A).reshape(1, W_A),
            "g_ka": jnp.tile(g_ka[l], 2 * H_A).reshape(1, W_A),
            "g_qm": jnp.tile(g_qm[l], H_M).reshape(1, W_M),
            "b_if": jnp.concatenate(
                [b_i[l], b_f[l], jnp.zeros((LANES - 2 * H_B,), F32)]).reshape(1, LANES),
            "lam_vecs": tuple(v[l].reshape(1, DK_A) for v in (lam_q1, lam_k1, lam_q2, lam_k2)),
            "g_subln": g_subln[l].reshape(1, DV_A),
            "g_mh": jnp.tile(g_mh[l], H_B).reshape(1, W_B),
            "w_out": w_out[l].astype(BF16),
        }
        mk, mkt, mvt = _memory_kv(mem_prompt, g_mem[l], w_mk[l], w_mv[l], g_km[l], g_mat)
        zero_state = (jnp.zeros((bp, H_B, DH_B, DH_B), F32), jnp.zeros((bp, H_B, DH_B), F32),
                      jnp.zeros((bp, H_B), F32))
        xp, k_p, v_p, st_p = _mixer_layer(xp, None, zero_state, mk, mvt, lam_init, p)
        xs, k_s, v_s, st_s = _mixer_layer(
            xs, (cache_attn_k[l], cache_attn_v[l]),
            (state_mlstm_C[l], state_mlstm_n[l], state_mlstm_m[l]),
            cache_mem_k[l].reshape(bs, N_MEM, W_M),
            cache_mem_v[l].transpose(0, 2, 3, 1).reshape(bs, W_M, N_MEM),
            lam_init, p)

        def tokens_first(t):
            return t.reshape(bp, H_M, DH_M, N_MEM).transpose(0, 3, 1, 2)

        outs["pk"].append(k_p); outs["pv"].append(v_p)
        outs["pC"].append(st_p[0]); outs["pn"].append(st_p[1]); outs["pm"].append(st_p[2])
        outs["pmk"].append(tokens_first(mkt))
        outs["pmv"].append(tokens_first(mvt))
        outs["sk"].append(k_s); outs["sv"].append(v_s)
        outs["sC"].append(st_s[0]); outs["sn"].append(st_s[1]); outs["sm"].append(st_s[2])
    stk = {name: jnp.stack(vals) for name, vals in outs.items()}
    return (xp, xs, stk["pk"], stk["pv"], stk["pC"], stk["pn"], stk["pm"], stk["pmk"],
            stk["pmv"], stk["sk"], stk["sv"], stk["sC"], stk["sn"], stk["sm"])
```

```python
import functools
import math

import jax
import jax.numpy as jnp
import numpy as np
from jax import lax
from jax.experimental import pallas as pl
from jax.experimental.pallas import tpu as pltpu

F32 = jnp.float32
BF16 = jnp.bfloat16

D_MODEL = 1024
CHUNK = 64
N_MEM = 256
H_A, DK_A = 4, 64
DV_A = 2 * DK_A
W_A = H_A * DV_A
H_B, DH_B = 4, 64
W_B = H_B * DH_B
H_M, DH_M = 4, 64
W_M = H_M * DH_M
D_MIX = W_A + W_B + W_M
EPS = 1e-6
ALIBI_SLOPES = tuple(2.0 ** (-8.0 * (h + 1) / H_A) for h in range(H_A))
N_IN = 4 * W_A + 5 * W_B + 2 * H_B + 2 * W_M

LANES = 128
MXU_DIM = 256
BF16_SUBLANES = 16
VT_ROWS = DV_A + BF16_SUBLANES
NEG_BIG = -1e30
LOG2E = math.log2(math.e)


def _bf16_pieces(value, n=3):
    pieces, rest = [], np.float32(value)
    for _ in range(n):
        piece = np.float32(rest.astype(jnp.bfloat16))
        pieces.append(float(piece))
        rest = np.float32(rest - piece)
    return tuple(pieces)


LOG2E_BF16_PIECES = _bf16_pieces(LOG2E)

C_QA, C_KA, C_VA, C_GA = 0, W_A, 2 * W_A, 3 * W_A
C_QB = 4 * W_A
C_KB, C_VB, C_OB, C_GB = C_QB + W_B, C_QB + 2 * W_B, C_QB + 3 * W_B, C_QB + 4 * W_B
C_QM = C_GB + W_B
C_GM = C_QM + W_M
C_IF = C_GM + W_M
N_IN_PAD = C_IF + LANES

VMEM_LIMIT = 56 * 1024 * 1024


def _dot(a, b):
    return jnp.dot(a, b, preferred_element_type=F32)


def _dot_nt(a, b):
    return lax.dot_general(a, b, (((1,), (1,)), ((), ())), preferred_element_type=F32)


def _split3(x):
    hi = x.astype(BF16)
    r1 = x - hi.astype(F32)
    mid = r1.astype(BF16)
    lo = (r1 - mid.astype(F32)).astype(BF16)
    return hi, mid, lo


def _group_mean_sq(z, g_mat):
    zz = (z * z).astype(BF16)
    parts = [_dot(zz[:, c:c + MXU_DIM], g_mat) for c in range(0, z.shape[1], MXU_DIM)]
    return parts[0] if len(parts) == 1 else jnp.concatenate(parts, axis=1)


def _group_rms_norm(z, gain_row, g_mat):
    return z * lax.rsqrt(_group_mean_sq(z, g_mat) + EPS) * gain_row


def _log_sigmoid(u):
    return -(jnp.maximum(-u, 0.0) + jnp.log1p(jnp.exp(-jnp.abs(u))))


def _silu(u):
    return u * jax.nn.sigmoid(u)


def _memory_kv_kernel(mem_ref, gmem_ref, wk_ref, wv_ref, gkm_ref, gmat_ref,
                      mk_ref, mkt_ref, mvt_ref, mv_scr):
    x = mem_ref[...]
    ms = jnp.mean(x * x, axis=-1, keepdims=True)
    hm = (x * lax.rsqrt(ms + EPS) * gmem_ref[...]).astype(BF16)
    mk_ref[...] = _group_rms_norm(_dot(hm, wk_ref[...]), gkm_ref[...], gmat_ref[...])
    mv_scr[...] = _dot(hm, wv_ref[...])
    mkt_ref[...] = mk_ref[...].T
    mvt_ref[...] = mv_scr[...].T


def _memory_kv(mem, g_mem, w_mk, w_mv, g_km, g_mat):
    b, n, d = mem.shape
    row = lambda i: (i, 0, 0)
    const2 = lambda i: (0, 0)
    return pl.pallas_call(
        _memory_kv_kernel,
        grid=(b,),
        in_specs=[
            pl.BlockSpec((None, n, d), row),
            pl.BlockSpec((1, d), const2),
            pl.BlockSpec((d, W_M), const2),
            pl.BlockSpec((d, W_M), const2),
            pl.BlockSpec((1, W_M), const2),
            pl.BlockSpec((MXU_DIM, MXU_DIM), const2),
        ],
        out_specs=[pl.BlockSpec((None, n, W_M), row), pl.BlockSpec((None, W_M, n), row),
                   pl.BlockSpec((None, W_M, n), row)],
        out_shape=[jax.ShapeDtypeStruct((b, n, W_M), F32),
                   jax.ShapeDtypeStruct((b, W_M, n), F32),
                   jax.ShapeDtypeStruct((b, W_M, n), F32)],
        scratch_shapes=[pltpu.VMEM((n, W_M), F32)],
        compiler_params=pltpu.CompilerParams(
            dimension_semantics=("arbitrary",), vmem_limit_bytes=VMEM_LIMIT),
        name="memory_kv",
    )(mem, g_mem.reshape(1, d), w_mk.astype(BF16), w_mv.astype(BF16),
      jnp.tile(g_km, H_M).reshape(1, W_M), g_mat)


def _in_proj_kernel(x_ref, gn_ref, w_ref, gmat_ref, gqa_ref, gka_ref, gqm_ref, bif_ref,
                    qaug_ref, k_ref, v_ref, ga_ref, mb_ref, gl_ref, og_ref, qm_ref, gm_ref,
                    k4_ref, v4_ref):
    x = x_ref[...]
    h = (x * gn_ref[...]).astype(BF16)
    inv_rms = lax.rsqrt(jnp.mean(x * x, axis=-1, keepdims=True) + EPS)
    g_mat = gmat_ref[...]

    def proj(start, width):
        return _dot(h, w_ref[:, start:start + width]) * inv_rms

    qn = _group_rms_norm(proj(C_QA, W_A), gqa_ref[...], g_mat) * (DK_A ** -0.5 * LOG2E)
    lane = lax.broadcasted_iota(jnp.int32, (x.shape[0], LANES), 1)
    aug = jnp.zeros((x.shape[0], LANES), F32)
    for rep in range(2):
        for piece, val in enumerate(LOG2E_BF16_PIECES):
            aug = jnp.where(lane == DK_A + 3 * rep + piece, val, aug)
    for hd in range(H_A):
        slab = qn[:, hd * DV_A:(hd + 1) * DV_A]
        q1 = jnp.where(lane < DK_A, slab, aug)
        q2 = jnp.where(lane < DK_A, pltpu.roll(slab, DK_A, 1), aug)
        qaug_ref[:, (2 * hd) * LANES:(2 * hd + 1) * LANES] = q1.astype(BF16)
        qaug_ref[:, (2 * hd + 1) * LANES:(2 * hd + 2) * LANES] = q2.astype(BF16)

    kn = _group_rms_norm(proj(C_KA, W_A), gka_ref[...], g_mat)
    k_ref[...] = kn.astype(BF16)

    def cache_copy(dst_ref, src, hd, later):
        zero = pltpu.bitcast(
            lax.shift_right_logical(pltpu.bitcast(later[:, 0:DV_A], jnp.uint32), jnp.uint32(32)),
            F32)
        dst_ref[:, hd, :] = src[:, hd * DV_A:(hd + 1) * DV_A] + zero

    vn = proj(C_VA, W_A)
    v_ref[...] = vn.astype(BF16)
    cache_copy(k4_ref, kn, 0, vn)
    z = proj(C_GA, W_A)
    ga_ref[...] = _silu(z)
    cache_copy(k4_ref, kn, 1, z)
    z = proj(C_QB, W_B)
    mb_ref[:, 0:W_B] = z
    cache_copy(k4_ref, kn, 2, z)
    z = proj(C_KB, W_B)
    mb_ref[:, W_B:2 * W_B] = z * (DH_B ** -0.5)
    cache_copy(k4_ref, kn, 3, z)
    z = proj(C_VB, W_B)
    mb_ref[:, 2 * W_B:3 * W_B] = z
    cache_copy(v4_ref, vn, 0, z)
    z = proj(C_OB, W_B)
    og_ref[:, 0:W_B] = jax.nn.sigmoid(z)
    cache_copy(v4_ref, vn, 1, z)
    z = proj(C_GB, W_B)
    og_ref[:, W_B:2 * W_B] = _silu(z)
    cache_copy(v4_ref, vn, 2, z)
    z = proj(C_QM, W_M)
    qm_ref[...] = (_group_rms_norm(z, gqm_ref[...], g_mat)
                   * (DH_M ** -0.5 * LOG2E)).astype(BF16)
    cache_copy(v4_ref, vn, 3, z)
    gm_ref[...] = _silu(proj(C_GM, W_M))

    u = proj(C_IF, LANES) + bif_ref[...]
    gl_ref[...] = jnp.where(lane < H_B, u, _log_sigmoid(u))


def _in_proj(x2d, tm, g_norm, w_perm, g_mat, g_qa, g_ka, g_qm, b_if):
    n = x2d.shape[0]
    row = lambda i: (i, 0)
    const = lambda i: (0, 0)
    widths = (2 * W_A, W_A, W_A, W_A, 3 * W_B, LANES, 2 * W_B, W_M, W_M)
    dtypes = (BF16, BF16, BF16) + (F32,) * 4 + (BF16, F32)
    return pl.pallas_call(
        _in_proj_kernel,
        grid=(n // tm,),
        in_specs=[
            pl.BlockSpec((tm, D_MODEL), row),
            pl.BlockSpec((1, D_MODEL), const),
            pl.BlockSpec((D_MODEL, N_IN_PAD), const),
            pl.BlockSpec((MXU_DIM, MXU_DIM), const),
            pl.BlockSpec((1, W_A), const),
            pl.BlockSpec((1, W_A), const),
            pl.BlockSpec((1, W_M), const),
            pl.BlockSpec((1, LANES), const),
        ],
        out_specs=[pl.BlockSpec((tm, w), row) for w in widths]
        + [pl.BlockSpec((tm, H_A, DV_A), lambda i: (i, 0, 0))] * 2,
        out_shape=[jax.ShapeDtypeStruct((n, w), dt) for w, dt in zip(widths, dtypes)]
        + [jax.ShapeDtypeStruct((n, H_A, DV_A), F32)] * 2,
        compiler_params=pltpu.CompilerParams(
            dimension_semantics=("arbitrary",), vmem_limit_bytes=VMEM_LIMIT),
        name="in_proj",
    )(x2d, g_norm, w_perm, g_mat, g_qa, g_ka, g_qm, b_if)


def _lambda_value(lq1_ref, lk1_ref, lq2_ref, lk2_ref, lam_init):
    s1 = jnp.sum(lq1_ref[...] * lk1_ref[...], axis=-1, keepdims=True)
    s2 = jnp.sum(lq2_ref[...] * lk2_ref[...], axis=-1, keepdims=True)
    return jnp.exp(s1) - jnp.exp(s2) + lam_init


def _subln(o, gsub_row, lam_init):
    ms = jnp.mean(o * o, axis=-1, keepdims=True)
    return o * lax.rsqrt(ms + EPS) * gsub_row * (1.0 - lam_init)


def _prompt_attn_kernel(lq1_ref, lk1_ref, lq2_ref, lk2_ref, gsub_ref, q_ref, k_ref, v_ref,
                        ga_ref, o_ref, kaug_ref, vt_ref, m_ref, acc_ref, s_ref, qt_ref,
                        *, tile, lam_init):
    seq = k_ref.shape[0]
    n_tiles = seq // tile
    n_chain = 2 * H_A

    lane = lax.broadcasted_iota(jnp.int32, (seq, LANES), 1)
    pos = lax.broadcasted_iota(jnp.int32, (seq, LANES), 0)
    in_tile = (pos % tile).astype(F32)
    tile_base = (pos - pos % tile).astype(F32)
    for hd in range(H_A):
        slab = k_ref[:, hd * DV_A:(hd + 1) * DV_A].astype(F32)
        aug = jnp.where((lane >= DK_A) & (lane < DK_A + 3), ALIBI_SLOPES[hd] * in_tile,
                        jnp.where((lane >= DK_A + 3) & (lane < DK_A + 6),
                                  ALIBI_SLOPES[hd] * tile_base, 0.0))
        kaug_ref[2 * hd] = jnp.where(lane < DK_A, slab, aug).astype(BF16)
        kaug_ref[2 * hd + 1] = jnp.where(
            lane < DK_A, pltpu.roll(slab, DK_A, 1), aug).astype(BF16)
    for t in range(n_tiles):
        v_t = v_ref[t * tile:(t + 1) * tile, :].astype(F32).T.astype(BF16)
        for hd in range(H_A):
            vt_ref[t, hd, 0:DV_A, :] = v_t[hd * DV_A:(hd + 1) * DV_A, :]
            vt_ref[t, hd, DV_A:VT_ROWS, :] = jnp.ones((VT_ROWS - DV_A, tile), BF16)

    lam = _lambda_value(lq1_ref, lk1_ref, lq2_ref, lk2_ref, lam_init)

    key = lax.broadcasted_iota(jnp.int32, (tile, tile), 0)
    qry = lax.broadcasted_iota(jnp.int32, (tile, tile), 1)
    ahead = (key - qry).astype(F32)
    diag_bias = [jnp.where(key // CHUNK > qry // CHUNK, NEG_BIG,
                           jnp.where(key > qry, (-2.0 * LOG2E * ALIBI_SLOPES[hd]) * ahead, 0.0))
                 for hd in range(H_A)]

    def rows(i):
        if isinstance(i, int):
            return pl.ds(i * tile, tile)
        return pl.ds(pl.multiple_of(i * tile, tile), tile)

    def scores(par, c, h, j):
        s_ref[par, c, h] = _dot(kaug_ref[c, rows(j), :], qt_ref[par, c, h])

    def softmax_values(par, c, h, j, diagonal):
        s = s_ref[par, c, h]
        if diagonal:
            s = s + diag_bias[c // 2]
        mx = m_ref[par, c, h]
        mx_new = jnp.maximum(mx, jnp.max(s, axis=0, keepdims=True))
        p = jnp.exp2(s - mx_new)
        alpha = jnp.exp2(mx - mx_new)
        m_ref[par, c, h] = mx_new
        acc_ref[par, c, h] = (alpha * acc_ref[par, c, h]
                              + _dot(vt_ref[j, c // 2], p.astype(BF16)))

    def begin_chain(par, c, duo):
        for h in range(2):
            q = q_ref[rows(2 * duo + h), c * LANES:(c + 1) * LANES]
            qt_ref[par, c, h] = q.astype(F32).T.astype(BF16)
            m_ref[par, c, h] = jnp.full((1, tile), NEG_BIG, F32)
            acc_ref[par, c, h] = jnp.zeros((VT_ROWS, tile), F32)
            scores(par, c, h, 0)

    def finish_head(par, hd, duo):
        for h in range(2):
            a1 = acc_ref[par, 2 * hd, h]
            a2 = acc_ref[par, 2 * hd + 1, h]
            o1 = a1[0:DV_A] * (1.0 / a1[DV_A:DV_A + 1])
            o2 = a2[0:DV_A] * (1.0 / a2[DV_A:DV_A + 1])
            o = (o1 - lam * o2).T
            gate = ga_ref[rows(2 * duo + h), hd * DV_A:(hd + 1) * DV_A]
            o_ref[rows(2 * duo + h), hd * DV_A:(hd + 1) * DV_A] = (
                _subln(o, gsub_ref[...], lam_init) * gate).astype(BF16)

    for c in range(n_chain):
        begin_chain(0, c, 0)

    def duo_pair(pair, carry):
        for par in range(2):
            duo = 2 * pair + par
            first_diag = 2 * duo

            def past_tile(j, inner, par=par):
                for c in range(n_chain):
                    for h in range(2):
                        softmax_values(par, c, h, j, False)
                        scores(par, c, h, j + 1)
                return inner

            lax.fori_loop(0, first_diag, past_tile, 0)
            for c in range(n_chain):
                softmax_values(par, c, 0, first_diag, True)
                softmax_values(par, c, 1, first_diag, False)
                scores(par, c, 1, first_diag + 1)
            nxt = lax.rem(duo + 1, n_tiles // 2)
            for c in range(n_chain):
                softmax_values(par, c, 1, first_diag + 1, True)
                begin_chain(1 - par, c, nxt)
                if c % 2 == 1:
                    finish_head(par, c // 2, duo)
        return carry

    lax.fori_loop(0, n_tiles // 4, duo_pair, 0)


def _prompt_attn(qaug, k, v, ga, lam_vecs, g_subln, lam_init, tile):
    b, s, _ = k.shape
    assert (s // tile) % 4 == 0
    const = lambda bi: (0, 0)
    per_b = lambda bi: (bi, 0, 0)
    kern = functools.partial(_prompt_attn_kernel, tile=tile, lam_init=lam_init)
    return pl.pallas_call(
        kern,
        grid=(b,),
        in_specs=[pl.BlockSpec((1, DK_A), const)] * 4 + [
            pl.BlockSpec((1, DV_A), const),
            pl.BlockSpec((None, s, 2 * W_A), per_b),
            pl.BlockSpec((None, s, W_A), per_b),
            pl.BlockSpec((None, s, W_A), per_b),
            pl.BlockSpec((None, s, W_A), per_b),
        ],
        out_specs=pl.BlockSpec((None, s, W_A), per_b),
        out_shape=jax.ShapeDtypeStruct((b, s, W_A), BF16),
        scratch_shapes=[
            pltpu.VMEM((2 * H_A, s, LANES), BF16),
            pltpu.VMEM((s // tile, H_A, VT_ROWS, tile), BF16),
            pltpu.VMEM((2, 2 * H_A, 2, 1, tile), F32),
            pltpu.VMEM((2, 2 * H_A, 2, VT_ROWS, tile), F32),
            pltpu.VMEM((2, 2 * H_A, 2, tile, tile), F32),
            pltpu.VMEM((2, 2 * H_A, 2, LANES, tile), BF16),
        ],
        compiler_params=pltpu.CompilerParams(
            dimension_semantics=("arbitrary",), vmem_limit_bytes=VMEM_LIMIT),
        name="diff_attn_prompt",
    )(*lam_vecs, g_subln, qaug, k, v, ga)


def _sample_attn_kernel(lq1_ref, lk1_ref, lq2_ref, lk2_ref, gsub_ref, q_ref, kc_ref, vc_ref,
                        kn_ref, vn_ref, ga_ref, o_ref, *, lam_init):
    sq = q_ref.shape[0]
    past = kc_ref.shape[0] // H_A
    lam = _lambda_value(lq1_ref, lk1_ref, lq2_ref, lk2_ref, lam_init)
    lane = lax.broadcasted_iota(jnp.int32, (sq, LANES), 1)
    qpos_c = past + lax.broadcasted_iota(jnp.int32, (sq, past), 0)
    dist_c = jnp.abs(qpos_c - lax.broadcasted_iota(jnp.int32, (sq, past), 1)).astype(F32)
    dist_n = jnp.abs(lax.broadcasted_iota(jnp.int32, (sq, sq), 0)
                     - lax.broadcasted_iota(jnp.int32, (sq, sq), 1)).astype(F32)
    for hd in range(H_A):
        slope = ALIBI_SLOPES[hd]
        head_lanes = slice(hd * DV_A, (hd + 1) * DV_A)
        kc = kc_ref[pl.ds(hd, past, stride=H_A), :].astype(BF16)
        kn = kn_ref[:, head_lanes]
        vc = vc_ref[pl.ds(hd, past, stride=H_A), :].astype(BF16)
        vn = vn_ref[:, head_lanes]
        outs = []
        for m in range(2):
            q = q_ref[:, (2 * hd + m) * LANES:(2 * hd + m + 1) * LANES].astype(F32)
            if m == 0:
                q = jnp.where(lane < DK_A, q, 0.0)
            else:
                q = jnp.where(lane >= DK_A, pltpu.roll(q, DK_A, 1), 0.0)
            q = q.astype(BF16)
            s_c = _dot_nt(q, kc) - (slope * LOG2E) * dist_c
            s_n = _dot_nt(q, kn) - (slope * LOG2E) * dist_n
            mx = jnp.maximum(jnp.max(s_c, axis=-1, keepdims=True),
                             jnp.max(s_n, axis=-1, keepdims=True))
            p_c = jnp.exp2(s_c - mx)
            p_n = jnp.exp2(s_n - mx)
            l = jnp.sum(p_c, axis=-1, keepdims=True) + jnp.sum(p_n, axis=-1, keepdims=True)
            acc = _dot(p_c.astype(BF16), vc) + _dot(p_n.astype(BF16), vn)
            outs.append(acc * (1.0 / l))
        o = outs[0] - lam * outs[1]
        o_ref[:, head_lanes] = (
            _subln(o, gsub_ref[...], lam_init) * ga_ref[:, head_lanes]).astype(BF16)


def _sample_attn(qaug, k_cache, v_cache, k_new, v_new, ga, lam_vecs, g_subln, lam_init):
    b, past = k_cache.shape[:2]
    sq = k_new.shape[1]
    assert (past + sq - 1) // CHUNK <= past // CHUNK
    k_cache = k_cache.reshape(b, past * H_A, DV_A)
    v_cache = v_cache.reshape(b, past * H_A, DV_A)
    const = lambda bi: (0, 0)
    per_b = lambda bi: (bi, 0, 0)
    kern = functools.partial(_sample_attn_kernel, lam_init=lam_init)
    return pl.pallas_call(
        kern,
        grid=(b,),
        in_specs=[pl.BlockSpec((1, DK_A), const)] * 4 + [
            pl.BlockSpec((1, DV_A), const),
            pl.BlockSpec((None, sq, 2 * W_A), per_b),
            pl.BlockSpec((None, past * H_A, DV_A), per_b),
            pl.BlockSpec((None, past * H_A, DV_A), per_b),
            pl.BlockSpec((None, sq, W_A), per_b),
            pl.BlockSpec((None, sq, W_A), per_b),
            pl.BlockSpec((None, sq, W_A), per_b),
        ],
        out_specs=pl.BlockSpec((None, sq, W_A), per_b),
        out_shape=jax.ShapeDtypeStruct((b, sq, W_A), BF16),
        compiler_params=pltpu.CompilerParams(
            dimension_semantics=("arbitrary",), vmem_limit_bytes=VMEM_LIMIT),
        name="diff_attn_sample",
    )(*lam_vecs, g_subln, qaug, k_cache, v_cache, k_new, v_new, ga)


def _mlstm_kernel(mb_ref, gl_ref, og_ref, gmh_ref, c0_ref, n0_ref, m0_ref,
                  h_ref, c_out_ref, n_ref, m_ref, c_ref, *, valid):
    c_idx = pl.program_id(1)
    L = CHUNK
    W = W_B
    G = mb_ref.shape[0]
    R = G * L

    def head_block(hd):
        return slice(hd * DH_B, (hd + 1) * DH_B)

    @pl.when(c_idx == 0)
    def _load_state():
        c_ref[...] = jnp.zeros(c_ref.shape, F32)
        for g in range(G):
            for hd in range(H_B):
                c_ref[g, head_block(hd), head_block(hd)] = c0_ref[g, hd]
        n_ref[...] = n0_ref[...]
        m_ref[...] = m0_ref[...]

    lane_head = lax.broadcasted_iota(jnp.int32, (R, W), 1) // DH_B
    row_t = lax.broadcasted_iota(jnp.int32, (R, W), 0) % L
    lane_s = lax.broadcasted_iota(jnp.int32, (R, W), 1) % DH_B
    wi = lax.broadcasted_iota(jnp.int32, (W, W), 0)
    wj = lax.broadcasted_iota(jnp.int32, (W, W), 1)
    blk = wi // DH_B == wj // DH_B
    ones_bd = blk.astype(BF16)
    ei = lax.broadcasted_iota(jnp.int32, (LANES, 2 * W), 0)
    ej = lax.broadcasted_iota(jnp.int32, (LANES, 2 * W), 1)
    expand = (ei == ej // DH_B).astype(BF16)
    tri = (lax.broadcasted_iota(jnp.int32, (L, L), 1)
           <= lax.broadcasted_iota(jnp.int32, (L, L), 0)).astype(BF16)
    nar_lane = lax.broadcasted_iota(jnp.int32, (R, LANES), 1)
    nar_t = lax.broadcasted_iota(jnp.int32, (R, LANES), 0) % L

    def per_seq(x):
        return x.reshape(G, L, x.shape[-1])

    def stacked(x):
        return x.reshape(R, x.shape[-1])

    def seq_rows(x, g):
        return x[g * L:(g + 1) * L]

    def exact_matmul(xs, mat):
        n = xs[0].shape[0]
        pieces = [part for x in xs for part in _split3(x)]
        out = _dot(jnp.concatenate(pieces, axis=0), mat)
        return [out[(3 * i) * n:(3 * i + 1) * n] + out[(3 * i + 1) * n:(3 * i + 2) * n]
                + out[(3 * i + 2) * n:(3 * i + 3) * n] for i in range(len(xs))]

    def head_max(x):
        out = jnp.zeros((R, W), F32)
        for hd in range(H_B):
            mx = jnp.max(jnp.where(lane_head == hd, x, NEG_BIG), axis=-1, keepdims=True)
            out = jnp.where(lane_head == hd, mx, out)
        return out

    q = stacked(mb_ref[:, :, 0:W])
    k = stacked(mb_ref[:, :, W:2 * W])
    v = stacked(mb_ref[:, :, 2 * W:3 * W])
    gl = jnp.where(nar_lane < 2 * H_B, stacked(gl_ref[...]), 0.0)
    gl = jnp.where(nar_t < valid, gl, jnp.where(nar_lane < H_B, NEG_BIG, 0.0))
    cums = []
    for g in range(G):
        out = _dot(tri, jnp.concatenate(_split3(seq_rows(gl, g)), axis=1))
        cums.append(out[:, 0:LANES] + out[:, LANES:2 * LANES] + out[:, 2 * LANES:3 * LANES])
    narrow = jnp.where(nar_lane < H_B, gl, jnp.concatenate(cums, axis=0))
    (wide,) = exact_matmul([narrow], expand)
    ig_all = wide[:, 0:W]
    b_all = wide[:, W:2 * W]

    r_all = b_all - ig_all
    r_row = jnp.sum(per_seq(jnp.where(lane_s == row_t, r_all, 0.0)), axis=1, keepdims=True)
    m0_row = m_ref[...]
    n0_row = n_ref[...]
    b_seq = per_seq(b_all)
    d_mat = jnp.where(lane_s <= row_t, stacked(b_seq - r_row), NEG_BIG)
    g_all = stacked(b_seq + m0_row)
    m_all = jnp.maximum(g_all, head_max(d_mat))
    d_w = jnp.exp(d_mat - m_all)
    g_w = jnp.exp(g_all - m_all)

    q_bf = q.astype(BF16)
    zero_bf = jnp.zeros((W, W), BF16)
    scores, inter, v_bds = [], [], []
    for g in range(G):
        k_t = seq_rows(k, g).T.astype(BF16)
        k_bd = jnp.where(blk, jnp.concatenate([k_t] * H_B, axis=1), zero_bf)
        v_bds.append(jnp.where(
            blk, jnp.concatenate([seq_rows(v, g).astype(BF16)] * H_B, axis=0), zero_bf))
        c0_bd = jnp.where(blk, c_ref[g].astype(BF16), zero_bf)
        scores.append(_dot(seq_rows(q_bf, g), k_bd))
        inter.append(_dot(seq_rows(q_bf, g), c0_bd))
    a_mat = jnp.concatenate(scores, axis=0) * d_w
    a_bf = a_mat.astype(BF16)
    intra = [_dot(seq_rows(a_bf, g), v_bds[g]) for g in range(G)]
    num = g_w * jnp.concatenate(inter, axis=0) + jnp.concatenate(intra, axis=0)

    qn_bf = stacked(per_seq(q) * n0_row).astype(BF16)
    sums = _dot(jnp.concatenate([qn_bf, a_bf], axis=0), ones_bd)
    den = g_w * sums[0:R] + sums[R:2 * R]
    h = num / jnp.maximum(jnp.abs(den), jnp.exp(-m_all))
    h_ms = _dot((h * h).astype(BF16), ones_bd) * (1.0 / DH_B)
    h = h * lax.rsqrt(h_ms + EPS) * gmh_ref[...]
    og = stacked(og_ref[...])
    h_ref[...] = per_seq((h * og[:, 0:W] * og[:, W:2 * W]).astype(BF16))

    m_last = per_seq(m_all)[:, L - 1:L, :]
    b_last = b_seq[:, L - 1:L, :]
    w_s = jnp.exp(stacked(b_last - b_seq + per_seq(ig_all) - m_last))
    decay = jnp.exp(b_last + m0_row - m_last)
    kw = k * w_s
    v_bf = v.astype(BF16)
    for g in range(G):
        c_ref[g] = decay[g] * c_ref[g] + _dot(seq_rows(kw, g).T.astype(BF16), seq_rows(v_bf, g))
    n_ref[...] = decay * n0_row + jnp.sum(per_seq(kw), axis=1, keepdims=True)
    m_ref[...] = m_last

    @pl.when(c_idx == pl.num_programs(1) - 1)
    def _store_state():
        for g in range(G):
            for hd in range(H_B):
                c_out_ref[g, hd] = c_ref[g, head_block(hd), head_block(hd)]


def _mlstm(mb, gl, og, g_mh_row, c0, n0_row, m0_row, valid, n_seq):
    b, s, _ = mb.shape
    nc = s // CHUNK
    tok = lambda bi, ci: (bi, ci, 0)
    st = lambda bi, ci: (bi, 0, 0)
    st4 = lambda bi, ci: (bi, 0, 0, 0)
    kern = functools.partial(_mlstm_kernel, valid=valid)
    return pl.pallas_call(
        kern,
        grid=(b // n_seq, nc),
        in_specs=[
            pl.BlockSpec((n_seq, CHUNK, 3 * W_B), tok),
            pl.BlockSpec((n_seq, CHUNK, LANES), tok),
            pl.BlockSpec((n_seq, CHUNK, 2 * W_B), tok),
            pl.BlockSpec((1, W_B), lambda bi, ci: (0, 0)),
            pl.BlockSpec((n_seq, H_B, DH_B, DH_B), st4),
            pl.BlockSpec((n_seq, 1, W_B), st),
            pl.BlockSpec((n_seq, 1, W_B), st),
        ],
        out_specs=[
            pl.BlockSpec((n_seq, CHUNK, W_B), tok),
            pl.BlockSpec((n_seq, H_B, DH_B, DH_B), st4),
            pl.BlockSpec((n_seq, 1, W_B), st),
            pl.BlockSpec((n_seq, 1, W_B), st),
        ],
        out_shape=[
            jax.ShapeDtypeStruct((b, s, W_B), BF16),
            jax.ShapeDtypeStruct((b, H_B, DH_B, DH_B), F32),
            jax.ShapeDtypeStruct((b, 1, W_B), F32),
            jax.ShapeDtypeStruct((b, 1, W_B), F32),
        ],
        scratch_shapes=[pltpu.VMEM((n_seq, W_B, W_B), F32)],
        compiler_params=pltpu.CompilerParams(
            dimension_semantics=("arbitrary", "arbitrary"), vmem_limit_bytes=VMEM_LIMIT),
        name="mlstm",
    )(mb, gl, og, g_mh_row, c0, n0_row, m0_row)


def _mix_out_kernel(x_ref, oa_ref, hb_ref, qm_ref, gm_ref, mk_ref, mvt_ref, wo_ref, y_ref,
                    *, sub_rows):
    tm = x_ref.shape[0]
    sub = min(tm, sub_rows)
    mk = mk_ref[...]
    mvt = mvt_ref[...]
    k_feat_head = lax.broadcasted_iota(jnp.int32, mk.shape, 1) // DH_M
    v_feat_head = lax.broadcasted_iota(jnp.int32, mvt.shape, 0) // DH_M
    k_of_head = [jnp.where(k_feat_head == hd, mk, 0.0).astype(BF16) for hd in range(H_M)]
    vt_of_head = [jnp.where(v_feat_head == hd, mvt, 0.0).astype(BF16) for hd in range(H_M)]

    def probs(s):
        p = jnp.exp2(s - jnp.max(s, axis=0, keepdims=True))
        return (p * (1.0 / jnp.sum(p, axis=0, keepdims=True))).astype(BF16)

    def sub_tile_stages(r0):
        rows = slice(r0, r0 + sub)
        qm = qm_ref[rows, :]
        if sub < LANES:
            qm = jnp.concatenate([qm, jnp.zeros((LANES - sub, W_M), BF16)], axis=0)
        s0 = _dot_nt(k_of_head[0], qm)
        y = x_ref[rows, :] + _dot(oa_ref[rows, :], wo_ref[0:W_A, :])
        yield
        s1 = _dot_nt(k_of_head[1], qm)
        p0 = probs(s0)
        y = y + _dot(hb_ref[rows, :], wo_ref[W_A:W_A + W_B, :])
        yield
        s2 = _dot_nt(k_of_head[2], qm)
        p1 = probs(s1)
        omt = _dot(vt_of_head[0], p0)
        yield
        s3 = _dot_nt(k_of_head[3], qm)
        p2 = probs(s2)
        omt = omt + _dot(vt_of_head[1], p1)
        yield
        p3 = probs(s3)
        omt = omt + _dot(vt_of_head[2], p2)
        omt = omt + _dot(vt_of_head[3], p3)
        yield
        om = omt.T[0:sub] * gm_ref[rows, :]
        y_ref[rows, :] = y + _dot(om.astype(BF16), wo_ref[W_A + W_B:D_MIX, :])
        yield

    n_stage = 6
    gens = [sub_tile_stages(r0) for r0 in range(0, tm, sub)]
    for step in range(n_stage + len(gens) - 1):
        for idx, gen in enumerate(gens):
            if 0 <= step - idx < n_stage:
                next(gen)


def _mix_out(x, oa, hb, qm, gm, mem_k, mem_vt, w_out_bf, tm):
    b, s, _ = x.shape
    tok = lambda bi, ti: (bi, ti, 0)
    per_b = lambda bi, ti: (bi, 0, 0)
    const = lambda bi, ti: (0, 0)
    return pl.pallas_call(
        functools.partial(_mix_out_kernel, sub_rows=MXU_DIM),
        grid=(b, s // tm),
        in_specs=[
            pl.BlockSpec((None, tm, D_MODEL), tok),
            pl.BlockSpec((None, tm, W_A), tok),
            pl.BlockSpec((None, tm, W_B), tok),
            pl.BlockSpec((None, tm, W_M), tok),
            pl.BlockSpec((None, tm, W_M), tok),
            pl.BlockSpec((None, N_MEM, W_M), per_b),
            pl.BlockSpec((None, N_MEM, W_M), per_b),
            pl.BlockSpec((D_MIX, D_MODEL), const),
        ],
        out_specs=pl.BlockSpec((None, tm, D_MODEL), tok),
        out_shape=jax.ShapeDtypeStruct((b, s, D_MODEL), F32),
        compiler_params=pltpu.CompilerParams(
            dimension_semantics=("arbitrary", "arbitrary"), vmem_limit_bytes=VMEM_LIMIT),
        name="mix_out",
    )(x, oa, hb, qm, gm, mem_k, mem_vt, w_out_bf)


def _mixer_layer(x, past_kv, mlstm_state, mem_k, mem_vt, lam_init, p):
    b, s, _ = x.shape
    n = b * s
    tm = min(512, n)
    qaug, k_new, v_new, ga, mb, gl, og, qm, gm, k4, v4 = _in_proj(
        x.reshape(n, D_MODEL), tm, p["g_norm"], p["w_perm"], p["g_mat"],
        p["g_qa"], p["g_ka"], p["g_qm"], p["b_if"])
    k_new = k_new.reshape(b, s, W_A)
    v_new = v_new.reshape(b, s, W_A)
    qaug = qaug.reshape(b, s, 2 * W_A)
    ga = ga.reshape(b, s, W_A)

    if past_kv is None:
        oa = _prompt_attn(qaug, k_new, v_new, ga, p["lam_vecs"], p["g_subln"], lam_init,
                          tile=256)
    else:
        oa = _sample_attn(qaug, past_kv[0], past_kv[1], k_new, v_new, ga,
                          p["lam_vecs"], p["g_subln"], lam_init)

    c0, n0, m0 = mlstm_state
    s_pad = -(-s // CHUNK) * CHUNK
    mb3 = mb.reshape(b, s, 3 * W_B)
    gl3 = gl.reshape(b, s, LANES)
    og3 = og.reshape(b, s, 2 * W_B)
    if s_pad != s:
        pad = ((0, 0), (0, s_pad - s), (0, 0))
        mb3, gl3, og3 = jnp.pad(mb3, pad), jnp.pad(gl3, pad), jnp.pad(og3, pad)
    valid = CHUNK if s_pad == s else s
    hb, c_new, n_row, m_row = _mlstm(
        mb3, gl3, og3, p["g_mh"], c0, n0.reshape(b, 1, W_B),
        jnp.repeat(m0, DH_B, axis=-1).reshape(b, 1, W_B), valid, n_seq=min(b, 16))
    hb = hb[:, :s]
    new_state = (c_new, n_row.reshape(b, H_B, DH_B), m_row.reshape(b, H_B, DH_B)[:, :, 0])

    y = _mix_out(x, oa, hb, qm.reshape(b, s, W_M), gm.reshape(b, s, W_M), mem_k, mem_vt,
                 p["w_out"], tm=min(1024, s))
    return (y, k4.reshape(b, s, H_A, DV_A), v4.reshape(b, s, H_A, DV_A), new_state)


def kernel(x_prompt, x_sample, cache_attn_k, cache_attn_v, state_mlstm_C, state_mlstm_n,
           state_mlstm_m, cache_mem_k, cache_mem_v, mem_prompt, g_norm, w_in, w_out, g_qa,
           g_ka, lam_q1, lam_k1, lam_q2, lam_k2, g_subln, b_i, b_f, g_mh, g_qm, g_km, g_mem,
           w_mk, w_mv):
    depth = w_in.shape[0]
    bp = x_prompt.shape[0]
    bs = x_sample.shape[0]
    past = cache_attn_k.shape[2]
    gi = lax.broadcasted_iota(jnp.int32, (MXU_DIM, MXU_DIM), 0) // DH_B
    gj = lax.broadcasted_iota(jnp.int32, (MXU_DIM, MXU_DIM), 1) // DH_B
    g_mat = jnp.where(gi == gj, 1.0 / DH_B, 0.0).astype(BF16)

    xp, xs = x_prompt, x_sample
    outs = {name: [] for name in ("pk", "pv", "pC", "pn", "pm", "pmk", "pmv",
                                  "sk", "sv", "sC", "sn", "sm")}
    for l in range(depth):
        lam_init = 0.8 - 0.6 * math.exp(-0.3 * l)
        w = w_in[l]
        w_perm = jnp.concatenate(
            [w[:, :4 * W_A + 4 * W_B], w[:, 4 * W_A + 4 * W_B + 2 * H_B:],
             w[:, 4 * W_A + 4 * W_B:4 * W_A + 4 * W_B + 2 * H_B],
             jnp.zeros((D_MODEL, LANES - 2 * H_B), F32)], axis=1).astype(BF16)
        p = {
            "g_norm": g_norm[l].reshape(1, D_MODEL),
            "w_perm": w_perm,
            "g_mat": g_mat,
            "g_qa": jnp.tile(g_qa[l], 2 * H_A).reshape(1, W_A),
            "g_ka": jnp.tile(g_ka[l], 2 * H_A).reshape(1, W_A),
            "g_qm": jnp.tile(g_qm[l], H_M).reshape(1, W_M),
            "b_if": jnp.concatenate(
                [b_i[l], b_f[l], jnp.zeros((LANES - 2 * H_B,), F32)]).reshape(1, LANES),
            "lam_vecs": tuple(v[l].reshape(1, DK_A) for v in (lam_q1, lam_k1, lam_q2, lam_k2)),
            "g_subln": g_subln[l].reshape(1, DV_A),
            "g_mh": jnp.tile(g_mh[l], H_B).reshape(1, W_B),
            "w_out": w_out[l].astype(BF16),
        }
        mk, mkt, mvt = _memory_kv(mem_prompt, g_mem[l], w_mk[l], w_mv[l], g_km[l], g_mat)
        zero_state = (jnp.zeros((bp, H_B, DH_B, DH_B), F32), jnp.zeros((bp, H_B, DH_B), F32),
                      jnp.zeros((bp, H_B), F32))
        xp, k_p, v_p, st_p = _mixer_layer(xp, None, zero_state, mk, mvt, lam_init, p)
        xs, k_s, v_s, st_s = _mixer_layer(
            xs, (cache_attn_k[l], cache_attn_v[l]),
            (state_mlstm_C[l], state_mlstm_n[l], state_mlstm_m[l]),
            cache_mem_k[l].reshape(bs, N_MEM, W_M),
            cache_mem_v[l].transpose(0, 2, 3, 1).reshape(bs, W_M, N_MEM),
            lam_init, p)

        def tokens_first(t):
            return t.reshape(bp, H_M, DH_M, N_MEM).transpose(0, 3, 1, 2)

        outs["pk"].append(k_p); outs["pv"].append(v_p)
        outs["pC"].append(st_p[0]); outs["pn"].append(st_p[1]); outs["pm"].append(st_p[2])
        outs["pmk"].append(tokens_first(mkt))
        outs["pmv"].append(tokens_first(mvt))
        outs["sk"].append(k_s); outs["sv"].append(v_s)
        outs["sC"].append(st_s[0]); outs["sn"].append(st_s[1]); outs["sm"].append(st_s[2])
    stk = {name: jnp.stack(vals) for name, vals in outs.items()}
    return (xp, xs, stk["pk"], stk["pv"], stk["pC"], stk["pn"], stk["pm"], stk["pmk"],
            stk["pmv"], stk["sk"], stk["sv"], stk["sC"], stk["sn"], stk["sm"])
```

```python
import functools
import math

import jax
import jax.numpy as jnp
import numpy as np
from jax import lax
from jax.experimental import pallas as pl
from jax.experimental.pallas import tpu as pltpu

F32 = jnp.float32
BF16 = jnp.bfloat16

D_MODEL = 1024
CHUNK = 64
N_MEM = 256
H_A, DK_A = 4, 64
DV_A = 2 * DK_A
W_A = H_A * DV_A
H_B, DH_B = 4, 64
W_B = H_B * DH_B
H_M, DH_M = 4, 64
W_M = H_M * DH_M
D_MIX = W_A + W_B + W_M
EPS = 1e-6
ALIBI_SLOPES = tuple(2.0 ** (-8.0 * (h + 1) / H_A) for h in range(H_A))
N_IN = 4 * W_A + 5 * W_B + 2 * H_B + 2 * W_M

LANES = 128
MXU_DIM = 256
BF16_SUBLANES = 16
VT_ROWS = DV_A + BF16_SUBLANES
NEG_BIG = -1e30
LOG2E = math.log2(math.e)


def _bf16_pieces(value, n=3):
    pieces, rest = [], np.float32(value)
    for _ in range(n):
        piece = np.float32(rest.astype(jnp.bfloat16))
        pieces.append(float(piece))
        rest = np.float32(rest - piece)
    return tuple(pieces)


LOG2E_BF16_PIECES = _bf16_pieces(LOG2E)

C_QA, C_KA, C_VA, C_GA = 0, W_A, 2 * W_A, 3 * W_A
C_QB = 4 * W_A
C_KB, C_VB, C_OB, C_GB = C_QB + W_B, C_QB + 2 * W_B, C_QB + 3 * W_B, C_QB + 4 * W_B
C_QM = C_GB + W_B
C_GM = C_QM + W_M
C_IF = C_GM + W_M
N_IN_PAD = C_IF + LANES

VMEM_LIMIT = 56 * 1024 * 1024


def _dot(a, b):
    return jnp.dot(a, b, preferred_element_type=F32)


def _dot_nt(a, b):
    return lax.dot_general(a, b, (((1,), (1,)), ((), ())), preferred_element_type=F32)


def _split3(x):
    hi = x.astype(BF16)
    r1 = x - hi.astype(F32)
    mid = r1.astype(BF16)
    lo = (r1 - mid.astype(F32)).astype(BF16)
    return hi, mid, lo


def _group_mean_sq(z, g_mat):
    zz = (z * z).astype(BF16)
    parts = [_dot(zz[:, c:c + MXU_DIM], g_mat) for c in range(0, z.shape[1], MXU_DIM)]
    return parts[0] if len(parts) == 1 else jnp.concatenate(parts, axis=1)


def _group_rms_norm(z, gain_row, g_mat):
    return z * lax.rsqrt(_group_mean_sq(z, g_mat) + EPS) * gain_row


def _log_sigmoid(u):
    return -(jnp.maximum(-u, 0.0) + jnp.log1p(jnp.exp(-jnp.abs(u))))


def _silu(u):
    return u * jax.nn.sigmoid(u)


def _memory_kv_kernel(mem_ref, gmem_ref, wk_ref, wv_ref, gkm_ref, gmat_ref,
                      mk_ref, mkt_ref, mvt_ref, mv_scr):
    x = mem_ref[...]
    ms = jnp.mean(x * x, axis=-1, keepdims=True)
    hm = (x * lax.rsqrt(ms + EPS) * gmem_ref[...]).astype(BF16)
    mk_ref[...] = _group_rms_norm(_dot(hm, wk_ref[...]), gkm_ref[...], gmat_ref[...])
    mv_scr[...] = _dot(hm, wv_ref[...])
    mkt_ref[...] = mk_ref[...].T
    mvt_ref[...] = mv_scr[...].T


def _memory_kv(mem, g_mem, w_mk, w_mv, g_km, g_mat):
    b, n, d = mem.shape
    row = lambda i: (i, 0, 0)
    const2 = lambda i: (0, 0)
    return pl.pallas_call(
        _memory_kv_kernel,
        grid=(b,),
        in_specs=[
            pl.BlockSpec((None, n, d), row),
            pl.BlockSpec((1, d), const2),
            pl.BlockSpec((d, W_M), const2),
            pl.BlockSpec((d, W_M), const2),
            pl.BlockSpec((1, W_M), const2),
            pl.BlockSpec((MXU_DIM, MXU_DIM), const2),
        ],
        out_specs=[pl.BlockSpec((None, n, W_M), row), pl.BlockSpec((None, W_M, n), row),
                   pl.BlockSpec((None, W_M, n), row)],
        out_shape=[jax.ShapeDtypeStruct((b, n, W_M), F32),
                   jax.ShapeDtypeStruct((b, W_M, n), F32),
                   jax.ShapeDtypeStruct((b, W_M, n), F32)],
        scratch_shapes=[pltpu.VMEM((n, W_M), F32)],
        compiler_params=pltpu.CompilerParams(
            dimension_semantics=("arbitrary",), vmem_limit_bytes=VMEM_LIMIT),
        name="memory_kv",
    )(mem, g_mem.reshape(1, d), w_mk.astype(BF16), w_mv.astype(BF16),
      jnp.tile(g_km, H_M).reshape(1, W_M), g_mat)


def _in_proj_kernel(x_ref, gn_ref, w_ref, gmat_ref, gqa_ref, gka_ref, gqm_ref, bif_ref,
                    qaug_ref, k_ref, v_ref, ga_ref, mb_ref, gl_ref, og_ref, qm_ref, gm_ref,
                    k4_ref, v4_ref):
    x = x_ref[...]
    h = (x * gn_ref[...]).astype(BF16)
    inv_rms = lax.rsqrt(jnp.mean(x * x, axis=-1, keepdims=True) + EPS)
    g_mat = gmat_ref[...]

    def proj(start, width):
        return _dot(h, w_ref[:, start:start + width]) * inv_rms

    qn = _group_rms_norm(proj(C_QA, W_A), gqa_ref[...], g_mat) * (DK_A ** -0.5 * LOG2E)
    lane = lax.broadcasted_iota(jnp.int32, (x.shape[0], LANES), 1)
    aug = jnp.zeros((x.shape[0], LANES), F32)
    for rep in range(2):
        for piece, val in enumerate(LOG2E_BF16_PIECES):
            aug = jnp.where(lane == DK_A + 3 * rep + piece, val, aug)
    for hd in range(H_A):
        slab = qn[:, hd * DV_A:(hd + 1) * DV_A]
        q1 = jnp.where(lane < DK_A, slab, aug)
        q2 = jnp.where(lane < DK_A, pltpu.roll(slab, DK_A, 1), aug)
        qaug_ref[:, (2 * hd) * LANES:(2 * hd + 1) * LANES] = q1.astype(BF16)
        qaug_ref[:, (2 * hd + 1) * LANES:(2 * hd + 2) * LANES] = q2.astype(BF16)

    kn = _group_rms_norm(proj(C_KA, W_A), gka_ref[...], g_mat)
    k_ref[...] = kn.astype(BF16)

    def cache_copy(dst_ref, src, hd, later):
        zero = pltpu.bitcast(
            lax.shift_right_logical(pltpu.bitcast(later[:, 0:DV_A], jnp.uint32), jnp.uint32(32)),
            F32)
        dst_ref[:, hd, :] = src[:, hd * DV_A:(hd + 1) * DV_A] + zero

    vn = proj(C_VA, W_A)
    v_ref[...] = vn.astype(BF16)
    cache_copy(k4_ref, kn, 0, vn)
    z = proj(C_GA, W_A)
    ga_ref[...] = _silu(z)
    cache_copy(k4_ref, kn, 1, z)
    z = proj(C_QB, W_B)
    mb_ref[:, 0:W_B] = z
    cache_copy(k4_ref, kn, 2, z)
    z = proj(C_KB, W_B)
    mb_ref[:, W_B:2 * W_B] = z * (DH_B ** -0.5)
    cache_copy(k4_ref, kn, 3, z)
    z = proj(C_VB, W_B)
    mb_ref[:, 2 * W_B:3 * W_B] = z
    cache_copy(v4_ref, vn, 0, z)
    z = proj(C_OB, W_B)
    og_ref[:, 0:W_B] = jax.nn.sigmoid(z)
    cache_copy(v4_ref, vn, 1, z)
    z = proj(C_GB, W_B)
    og_ref[:, W_B:2 * W_B] = _silu(z)
    cache_copy(v4_ref, vn, 2, z)
    z = proj(C_QM, W_M)
    qm_ref[...] = (_group_rms_norm(z, gqm_ref[...], g_mat)
                   * (DH_M ** -0.5 * LOG2E)).astype(BF16)
    cache_copy(v4_ref, vn, 3, z)
    gm_ref[...] = _silu(proj(C_GM, W_M))

    u = proj(C_IF, LANES) + bif_ref[...]
    gl_ref[...] = jnp.where(lane < H_B, u, _log_sigmoid(u))


def _in_proj(x2d, tm, g_norm, w_perm, g_mat, g_qa, g_ka, g_qm, b_if):
    n = x2d.shape[0]
    row = lambda i: (i, 0)
    const = lambda i: (0, 0)
    widths = (2 * W_A, W_A, W_A, W_A, 3 * W_B, LANES, 2 * W_B, W_M, W_M)
    dtypes = (BF16, BF16, BF16) + (F32,) * 4 + (BF16, F32)
    return pl.pallas_call(
        _in_proj_kernel,
        grid=(n // tm,),
        in_specs=[
            pl.BlockSpec((tm, D_MODEL), row),
            pl.BlockSpec((1, D_MODEL), const),
            pl.BlockSpec((D_MODEL, N_IN_PAD), const),
            pl.BlockSpec((MXU_DIM, MXU_DIM), const),
            pl.BlockSpec((1, W_A), const),
            pl.BlockSpec((1, W_A), const),
            pl.BlockSpec((1, W_M), const),
            pl.BlockSpec((1, LANES), const),
        ],
        out_specs=[pl.BlockSpec((tm, w), row) for w in widths]
        + [pl.BlockSpec((tm, H_A, DV_A), lambda i: (i, 0, 0))] * 2,
        out_shape=[jax.ShapeDtypeStruct((n, w), dt) for w, dt in zip(widths, dtypes)]
        + [jax.ShapeDtypeStruct((n, H_A, DV_A), F32)] * 2,
        compiler_params=pltpu.CompilerParams(
            dimension_semantics=("arbitrary",), vmem_limit_bytes=VMEM_LIMIT),
        name="in_proj",
    )(x2d, g_norm, w_perm, g_mat, g_qa, g_ka, g_qm, b_if)


def _lambda_value(lq1_ref, lk1_ref, lq2_ref, lk2_ref, lam_init):
    s1 = jnp.sum(lq1_ref[...] * lk1_ref[...], axis=-1, keepdims=True)
    s2 = jnp.sum(lq2_ref[...] * lk2_ref[...], axis=-1, keepdims=True)
    return jnp.exp(s1) - jnp.exp(s2) + lam_init


def _subln(o, gsub_row, lam_init):
    ms = jnp.mean(o * o, axis=-1, keepdims=True)
    return o * lax.rsqrt(ms + EPS) * gsub_row * (1.0 - lam_init)


def _prompt_attn_kernel(lq1_ref, lk1_ref, lq2_ref, lk2_ref, gsub_ref, q_ref, k_ref, v_ref,
                        ga_ref, o_ref, kaug_ref, vt_ref, m_ref, acc_ref, s_ref, qt_ref,
                        *, tile, lam_init):
    seq = k_ref.shape[0]
    n_tiles = seq // tile
    n_chain = 2 * H_A

    lane = lax.broadcasted_iota(jnp.int32, (seq, LANES), 1)
    pos = lax.broadcasted_iota(jnp.int32, (seq, LANES), 0)
    in_tile = (pos % tile).astype(F32)
    tile_base = (pos - pos % tile).astype(F32)
    for hd in range(H_A):
        slab = k_ref[:, hd * DV_A:(hd + 1) * DV_A].astype(F32)
        aug = jnp.where((lane >= DK_A) & (lane < DK_A + 3), ALIBI_SLOPES[hd] * in_tile,
                        jnp.where((lane >= DK_A + 3) & (lane < DK_A + 6),
                                  ALIBI_SLOPES[hd] * tile_base, 0.0))
        kaug_ref[2 * hd] = jnp.where(lane < DK_A, slab, aug).astype(BF16)
        kaug_ref[2 * hd + 1] = jnp.where(
            lane < DK_A, pltpu.roll(slab, DK_A, 1), aug).astype(BF16)
    for t in range(n_tiles):
        v_t = v_ref[t * tile:(t + 1) * tile, :].astype(F32).T.astype(BF16)
        for hd in range(H_A):
            vt_ref[t, hd, 0:DV_A, :] = v_t[hd * DV_A:(hd + 1) * DV_A, :]
            vt_ref[t, hd, DV_A:VT_ROWS, :] = jnp.ones((VT_ROWS - DV_A, tile), BF16)

    lam = _lambda_value(lq1_ref, lk1_ref, lq2_ref, lk2_ref, lam_init)

    key = lax.broadcasted_iota(jnp.int32, (tile, tile), 0)
    qry = lax.broadcasted_iota(jnp.int32, (tile, tile), 1)
    ahead = (key - qry).astype(F32)
    diag_bias = [jnp.where(key // CHUNK > qry // CHUNK, NEG_BIG,
                           jnp.where(key > qry, (-2.0 * LOG2E * ALIBI_SLOPES[hd]) * ahead, 0.0))
                 for hd in range(H_A)]

    def rows(i):
        if isinstance(i, int):
            return pl.ds(i * tile, tile)
        return pl.ds(pl.multiple_of(i * tile, tile), tile)

    def scores(par, c, h, j):
        s_ref[par, c, h] = _dot(kaug_ref[c, rows(j), :], qt_ref[par, c, h])

    def softmax_values(par, c, h, j, diagonal):
        s = s_ref[par, c, h]
        if diagonal:
            s = s + diag_bias[c // 2]
        mx = m_ref[par, c, h]
        mx_new = jnp.maximum(mx, jnp.max(s, axis=0, keepdims=True))
        p = jnp.exp2(s - mx_new)
        alpha = jnp.exp2(mx - mx_new)
        m_ref[par, c, h] = mx_new
        acc_ref[par, c, h] = (alpha * acc_ref[par, c, h]
                              + _dot(vt_ref[j, c // 2], p.astype(BF16)))

    def begin_chain(par, c, duo):
        for h in range(2):
            q = q_ref[rows(2 * duo + h), c * LANES:(c + 1) * LANES]
            qt_ref[par, c, h] = q.astype(F32).T.astype(BF16)
            m_ref[par, c, h] = jnp.full((1, tile), NEG_BIG, F32)
            acc_ref[par, c, h] = jnp.zeros((VT_ROWS, tile), F32)
            scores(par, c, h, 0)

    def finish_head(par, hd, duo):
        for h in range(2):
            a1 = acc_ref[par, 2 * hd, h]
            a2 = acc_ref[par, 2 * hd + 1, h]
            o1 = a1[0:DV_A] * (1.0 / a1[DV_A:DV_A + 1])
            o2 = a2[0:DV_A] * (1.0 / a2[DV_A:DV_A + 1])
            o = (o1 - lam * o2).T
            gate = ga_ref[rows(2 * duo + h), hd * DV_A:(hd + 1) * DV_A]
            o_ref[rows(2 * duo + h), hd * DV_A:(hd + 1) * DV_A] = (
                _subln(o, gsub_ref[...], lam_init) * gate).astype(BF16)

    for c in range(n_chain):
        begin_chain(0, c, 0)

    def duo_pair(pair, carry):
        for par in range(2):
            duo = 2 * pair + par
            first_diag = 2 * duo

            def past_tile(j, inner, par=par):
                for c in range(n_chain):
                    for h in range(2):
                        softmax_values(par, c, h, j, False)
                        scores(par, c, h, j + 1)
                return inner

            lax.fori_loop(0, first_diag, past_tile, 0)
            for c in range(n_chain):
                softmax_values(par, c, 0, first_diag, True)
                softmax_values(par, c, 1, first_diag, False)
                scores(par, c, 1, first_diag + 1)
            nxt = lax.rem(duo + 1, n_tiles // 2)
            for c in range(n_chain):
                softmax_values(par, c, 1, first_diag + 1, True)
                begin_chain(1 - par, c, nxt)
                if c % 2 == 1:
                    finish_head(par, c // 2, duo)
        return carry

    lax.fori_loop(0, n_tiles // 4, duo_pair, 0)


def _prompt_attn(qaug, k, v, ga, lam_vecs, g_subln, lam_init, tile):
    b, s, _ = k.shape
    assert (s // tile) % 4 == 0
    const = lambda bi: (0, 0)
    per_b = lambda bi: (bi, 0, 0)
    kern = functools.partial(_prompt_attn_kernel, tile=tile, lam_init=lam_init)
    return pl.pallas_call(
        kern,
        grid=(b,),
        in_specs=[pl.BlockSpec((1, DK_A), const)] * 4 + [
            pl.BlockSpec((1, DV_A), const),
            pl.BlockSpec((None, s, 2 * W_A), per_b),
            pl.BlockSpec((None, s, W_A), per_b),
            pl.BlockSpec((None, s, W_A), per_b),
            pl.BlockSpec((None, s, W_A), per_b),
        ],
        out_specs=pl.BlockSpec((None, s, W_A), per_b),
        out_shape=jax.ShapeDtypeStruct((b, s, W_A), BF16),
        scratch_shapes=[
            pltpu.VMEM((2 * H_A, s, LANES), BF16),
            pltpu.VMEM((s // tile, H_A, VT_ROWS, tile), BF16),
            pltpu.VMEM((2, 2 * H_A, 2, 1, tile), F32),
            pltpu.VMEM((2, 2 * H_A, 2, VT_ROWS, tile), F32),
            pltpu.VMEM((2, 2 * H_A, 2, tile, tile), F32),
            pltpu.VMEM((2, 2 * H_A, 2, LANES, tile), BF16),
        ],
        compiler_params=pltpu.CompilerParams(
            dimension_semantics=("arbitrary",), vmem_limit_bytes=VMEM_LIMIT),
        name="diff_attn_prompt",
    )(*lam_vecs, g_subln, qaug, k, v, ga)


def _sample_attn_kernel(lq1_ref, lk1_ref, lq2_ref, lk2_ref, gsub_ref, q_ref, kc_ref, vc_ref,
                        kn_ref, vn_ref, ga_ref, o_ref, *, lam_init):
    sq = q_ref.shape[0]
    past = kc_ref.shape[0] // H_A
    lam = _lambda_value(lq1_ref, lk1_ref, lq2_ref, lk2_ref, lam_init)
    lane = lax.broadcasted_iota(jnp.int32, (sq, LANES), 1)
    qpos_c = past + lax.broadcasted_iota(jnp.int32, (sq, past), 0)
    dist_c = jnp.abs(qpos_c - lax.broadcasted_iota(jnp.int32, (sq, past), 1)).astype(F32)
    dist_n = jnp.abs(lax.broadcasted_iota(jnp.int32, (sq, sq), 0)
                     - lax.broadcasted_iota(jnp.int32, (sq, sq), 1)).astype(F32)
    for hd in range(H_A):
        slope = ALIBI_SLOPES[hd]
        head_lanes = slice(hd * DV_A, (hd + 1) * DV_A)
        kc = kc_ref[pl.ds(hd, past, stride=H_A), :].astype(BF16)
        kn = kn_ref[:, head_lanes]
        vc = vc_ref[pl.ds(hd, past, stride=H_A), :].astype(BF16)
        vn = vn_ref[:, head_lanes]
        outs = []
        for m in range(2):
            q = q_ref[:, (2 * hd + m) * LANES:(2 * hd + m + 1) * LANES].astype(F32)
            if m == 0:
                q = jnp.where(lane < DK_A, q, 0.0)
            else:
                q = jnp.where(lane >= DK_A, pltpu.roll(q, DK_A, 1), 0.0)
            q = q.astype(BF16)
            s_c = _dot_nt(q, kc) - (slope * LOG2E) * dist_c
            s_n = _dot_nt(q, kn) - (slope * LOG2E) * dist_n
            mx = jnp.maximum(jnp.max(s_c, axis=-1, keepdims=True),
                             jnp.max(s_n, axis=-1, keepdims=True))
            p_c = jnp.exp2(s_c - mx)
            p_n = jnp.exp2(s_n - mx)
            l = jnp.sum(p_c, axis=-1, keepdims=True) + jnp.sum(p_n, axis=-1, keepdims=True)
            acc = _dot(p_c.astype(BF16), vc) + _dot(p_n.astype(BF16), vn)
            outs.append(acc * (1.0 / l))
        o = outs[0] - lam * outs[1]
        o_ref[:, head_lanes] = (
            _subln(o, gsub_ref[...], lam_init) * ga_ref[:, head_lanes]).astype(BF16)


def _sample_attn(qaug, k_cache, v_cache, k_new, v_new, ga, lam_vecs, g_subln, lam_init):
    b, past = k_cache.shape[:2]
    sq = k_new.shape[1]
    assert (past + sq - 1) // CHUNK <= past // CHUNK
    k_cache = k_cache.reshape(b, past * H_A, DV_A)
    v_cache = v_cache.reshape(b, past * H_A, DV_A)
    const = lambda bi: (0, 0)
    per_b = lambda bi: (bi, 0, 0)
    kern = functools.partial(_sample_attn_kernel, lam_init=lam_init)
    return pl.pallas_call(
        kern,
        grid=(b,),
        in_specs=[pl.BlockSpec((1, DK_A), const)] * 4 + [
            pl.BlockSpec((1, DV_A), const),
            pl.BlockSpec((None, sq, 2 * W_A), per_b),
            pl.BlockSpec((None, past * H_A, DV_A), per_b),
            pl.BlockSpec((None, past * H_A, DV_A), per_b),
            pl.BlockSpec((None, sq, W_A), per_b),
            pl.BlockSpec((None, sq, W_A), per_b),
            pl.BlockSpec((None, sq, W_A), per_b),
        ],
        out_specs=pl.BlockSpec((None, sq, W_A), per_b),
        out_shape=jax.ShapeDtypeStruct((b, sq, W_A), BF16),
        compiler_params=pltpu.CompilerParams(
            dimension_semantics=("arbitrary",), vmem_limit_bytes=VMEM_LIMIT),
        name="diff_attn_sample",
    )(*lam_vecs, g_subln, qaug, k_cache, v_cache, k_new, v_new, ga)


def _mlstm_kernel(mb_ref, gl_ref, og_ref, gmh_ref, c0_ref, n0_ref, m0_ref,
                  h_ref, c_out_ref, n_ref, m_ref, c_ref, *, valid):
    c_idx = pl.program_id(1)
    L = CHUNK
    W = W_B
    G = mb_ref.shape[0]
    R = G * L

    def head_block(hd):
        return slice(hd * DH_B, (hd + 1) * DH_B)

    @pl.when(c_idx == 0)
    def _load_state():
        c_ref[...] = jnp.zeros(c_ref.shape, F32)
        for g in range(G):
            for hd in range(H_B):
                c_ref[g, head_block(hd), head_block(hd)] = c0_ref[g, hd]
        n_ref[...] = n0_ref[...]
        m_ref[...] = m0_ref[...]

    lane_head = lax.broadcasted_iota(jnp.int32, (R, W), 1) // DH_B
    row_t = lax.broadcasted_iota(jnp.int32, (R, W), 0) % L
    lane_s = lax.broadcasted_iota(jnp.int32, (R, W), 1) % DH_B
    wi = lax.broadcasted_iota(jnp.int32, (W, W), 0)
    wj = lax.broadcasted_iota(jnp.int32, (W, W), 1)
    blk = wi // DH_B == wj // DH_B
    ones_bd = blk.astype(BF16)
    ei = lax.broadcasted_iota(jnp.int32, (LANES, 2 * W), 0)
    ej = lax.broadcasted_iota(jnp.int32, (LANES, 2 * W), 1)
    expand = (ei == ej // DH_B).astype(BF16)
    tri = (lax.broadcasted_iota(jnp.int32, (L, L), 1)
           <= lax.broadcasted_iota(jnp.int32, (L, L), 0)).astype(BF16)
    nar_lane = lax.broadcasted_iota(jnp.int32, (R, LANES), 1)
    nar_t = lax.broadcasted_iota(jnp.int32, (R, LANES), 0) % L

    def per_seq(x):
        return x.reshape(G, L, x.shape[-1])

    def stacked(x):
        return x.reshape(R, x.shape[-1])

    def seq_rows(x, g):
        return x[g * L:(g + 1) * L]

    def exact_matmul(xs, mat):
        n = xs[0].shape[0]
        pieces = [part for x in xs for part in _split3(x)]
        out = _dot(jnp.concatenate(pieces, axis=0), mat)
        return [out[(3 * i) * n:(3 * i + 1) * n] + out[(3 * i + 1) * n:(3 * i + 2) * n]
                + out[(3 * i + 2) * n:(3 * i + 3) * n] for i in range(len(xs))]

    def head_max(x):
        out = jnp.zeros((R, W), F32)
        for hd in range(H_B):
            mx = jnp.max(jnp.where(lane_head == hd, x, NEG_BIG), axis=-1, keepdims=True)
            out = jnp.where(lane_head == hd, mx, out)
        return out

    q = stacked(mb_ref[:, :, 0:W])
    k = stacked(mb_ref[:, :, W:2 * W])
    v = stacked(mb_ref[:, :, 2 * W:3 * W])
    gl = jnp.where(nar_lane < 2 * H_B, stacked(gl_ref[...]), 0.0)
    gl = jnp.where(nar_t < valid, gl, jnp.where(nar_lane < H_B, NEG_BIG, 0.0))
    cums = []
    for g in range(G):
        out = _dot(tri, jnp.concatenate(_split3(seq_rows(gl, g)), axis=1))
        cums.append(out[:, 0:LANES] + out[:, LANES:2 * LANES] + out[:, 2 * LANES:3 * LANES])
    narrow = jnp.where(nar_lane < H_B, gl, jnp.concatenate(cums, axis=0))
    (wide,) = exact_matmul([narrow], expand)
    ig_all = wide[:, 0:W]
    b_all = wide[:, W:2 * W]

    r_all = b_all - ig_all
    r_row = jnp.sum(per_seq(jnp.where(lane_s == row_t, r_all, 0.0)), axis=1, keepdims=True)
    m0_row = m_ref[...]
    n0_row = n_ref[...]
    b_seq = per_seq(b_all)
    d_mat = jnp.where(lane_s <= row_t, stacked(b_seq - r_row), NEG_BIG)
    g_all = stacked(b_seq + m0_row)
    m_all = jnp.maximum(g_all, head_max(d_mat))
    d_w = jnp.exp(d_mat - m_all)
    g_w = jnp.exp(g_all - m_all)

    q_bf = q.astype(BF16)
    zero_bf = jnp.zeros((W, W), BF16)
    scores, inter, v_bds = [], [], []
    for g in range(G):
        k_t = seq_rows(k, g).T.astype(BF16)
        k_bd = jnp.where(blk, jnp.concatenate([k_t] * H_B, axis=1), zero_bf)
        v_bds.append(jnp.where(
            blk, jnp.concatenate([seq_rows(v, g).astype(BF16)] * H_B, axis=0), zero_bf))
        c0_bd = jnp.where(blk, c_ref[g].astype(BF16), zero_bf)
        scores.append(_dot(seq_rows(q_bf, g), k_bd))
        inter.append(_dot(seq_rows(q_bf, g), c0_bd))
    a_mat = jnp.concatenate(scores, axis=0) * d_w
    a_bf = a_mat.astype(BF16)
    intra = [_dot(seq_rows(a_bf, g), v_bds[g]) for g in range(G)]
    num = g_w * jnp.concatenate(inter, axis=0) + jnp.concatenate(intra, axis=0)

    qn_bf = stacked(per_seq(q) * n0_row).astype(BF16)
    sums = _dot(jnp.concatenate([qn_bf, a_bf], axis=0), ones_bd)
    den = g_w * sums[0:R] + sums[R:2 * R]
    h = num / jnp.maximum(jnp.abs(den), jnp.exp(-m_all))
    h_ms = _dot((h * h).astype(BF16), ones_bd) * (1.0 / DH_B)
    h = h * lax.rsqrt(h_ms + EPS) * gmh_ref[...]
    og = stacked(og_ref[...])
    h_ref[...] = per_seq((h * og[:, 0:W] * og[:, W:2 * W]).astype(BF16))

    m_last = per_seq(m_all)[:, L - 1:L, :]
    b_last = b_seq[:, L - 1:L, :]
    w_s = jnp.exp(stacked(b_last - b_seq + per_seq(ig_all) - m_last))
    decay = jnp.exp(b_last + m0_row - m_last)
    kw = k * w_s
    v_bf = v.astype(BF16)
    for g in range(G):
        c_ref[g] = decay[g] * c_ref[g] + _dot(seq_rows(kw, g).T.astype(BF16), seq_rows(v_bf, g))
    n_ref[...] = decay * n0_row + jnp.sum(per_seq(kw), axis=1, keepdims=True)
    m_ref[...] = m_last

    @pl.when(c_idx == pl.num_programs(1) - 1)
    def _store_state():
        for g in range(G):
            for hd in range(H_B):
                c_out_ref[g, hd] = c_ref[g, head_block(hd), head_block(hd)]


def _mlstm(mb, gl, og, g_mh_row, c0, n0_row, m0_row, valid, n_seq):
    b, s, _ = mb.shape
    nc = s // CHUNK
    tok = lambda bi, ci: (bi, ci, 0)
    st = lambda bi, ci: (bi, 0, 0)
    st4 = lambda bi, ci: (bi, 0, 0, 0)
    kern = functools.partial(_mlstm_kernel, valid=valid)
    return pl.pallas_call(
        kern,
        grid=(b // n_seq, nc),
        in_specs=[
            pl.BlockSpec((n_seq, CHUNK, 3 * W_B), tok),
            pl.BlockSpec((n_seq, CHUNK, LANES), tok),
            pl.BlockSpec((n_seq, CHUNK, 2 * W_B), tok),
            pl.BlockSpec((1, W_B), lambda bi, ci: (0, 0)),
            pl.BlockSpec((n_seq, H_B, DH_B, DH_B), st4),
            pl.BlockSpec((n_seq, 1, W_B), st),
            pl.BlockSpec((n_seq, 1, W_B), st),
        ],
        out_specs=[
            pl.BlockSpec((n_seq, CHUNK, W_B), tok),
            pl.BlockSpec((n_seq, H_B, DH_B, DH_B), st4),
            pl.BlockSpec((n_seq, 1, W_B), st),
            pl.BlockSpec((n_seq, 1, W_B), st),
        ],
        out_shape=[
            jax.ShapeDtypeStruct((b, s, W_B), BF16),
            jax.ShapeDtypeStruct((b, H_B, DH_B, DH_B), F32),
            jax.ShapeDtypeStruct((b, 1, W_B), F32),
            jax.ShapeDtypeStruct((b, 1, W_B), F32),
        ],
        scratch_shapes=[pltpu.VMEM((n_seq, W_B, W_B), F32)],
        compiler_params=pltpu.CompilerParams(
            dimension_semantics=("arbitrary", "arbitrary"), vmem_limit_bytes=VMEM_LIMIT),
        name="mlstm",
    )(mb, gl, og, g_mh_row, c0, n0_row, m0_row)


def _mix_out_kernel(x_ref, oa_ref, hb_ref, qm_ref, gm_ref, mk_ref, mvt_ref, wo_ref, y_ref,
                    *, sub_rows):
    tm = x_ref.shape[0]
    sub = min(tm, sub_rows)
    mk = mk_ref[...]
    mvt = mvt_ref[...]
    k_feat_head = lax.broadcasted_iota(jnp.int32, mk.shape, 1) // DH_M
    v_feat_head = lax.broadcasted_iota(jnp.int32, mvt.shape, 0) // DH_M
    k_of_head = [jnp.where(k_feat_head == hd, mk, 0.0).astype(BF16) for hd in range(H_M)]
    vt_of_head = [jnp.where(v_feat_head == hd, mvt, 0.0).astype(BF16) for hd in range(H_M)]

    def probs(s):
        p = jnp.exp2(s - jnp.max(s, axis=0, keepdims=True))
        return (p * (1.0 / jnp.sum(p, axis=0, keepdims=True))).astype(BF16)

    def sub_tile_stages(r0):
        rows = slice(r0, r0 + sub)
        qm = qm_ref[rows, :]
        if sub < LANES:
            qm = jnp.concatenate([qm, jnp.zeros((LANES - sub, W_M), BF16)], axis=0)
        s0 = _dot_nt(k_of_head[0], qm)
        y = x_ref[rows, :] + _dot(oa_ref[rows, :], wo_ref[0:W_A, :])
        yield
        s1 = _dot_nt(k_of_head[1], qm)
        p0 = probs(s0)
        y = y + _dot(hb_ref[rows, :], wo_ref[W_A:W_A + W_B, :])
        yield
        s2 = _dot_nt(k_of_head[2], qm)
        p1 = probs(s1)
        omt = _dot(vt_of_head[0], p0)
        yield
        s3 = _dot_nt(k_of_head[3], qm)
        p2 = probs(s2)
        omt = omt + _dot(vt_of_head[1], p1)
        yield
        p3 = probs(s3)
        omt = omt + _dot(vt_of_head[2], p2)
        omt = omt + _dot(vt_of_head[3], p3)
        yield
        om = omt.T[0:sub] * gm_ref[rows, :]
        y_ref[rows, :] = y + _dot(om.astype(BF16), wo_ref[W_A + W_B:D_MIX, :])
        yield

    n_stage = 6
    gens = [sub_tile_stages(r0) for r0 in range(0, tm, sub)]
    for step in range(n_stage + len(gens) - 1):
        for idx, gen in enumerate(gens):
            if 0 <= step - idx < n_stage:
                next(gen)


def _mix_out(x, oa, hb, qm, gm, mem_k, mem_vt, w_out_bf, tm):
    b, s, _ = x.shape
    tok = lambda bi, ti: (bi, ti, 0)
    per_b = lambda bi, ti: (bi, 0, 0)
    const = lambda bi, ti: (0, 0)
    return pl.pallas_call(
        functools.partial(_mix_out_kernel, sub_rows=MXU_DIM),
        grid=(b, s // tm),
        in_specs=[
            pl.BlockSpec((None, tm, D_MODEL), tok),
            pl.BlockSpec((None, tm, W_A), tok),
            pl.BlockSpec((None, tm, W_B), tok),
            pl.BlockSpec((None, tm, W_M), tok),
            pl.BlockSpec((None, tm, W_M), tok),
            pl.BlockSpec((None, N_MEM, W_M), per_b),
            pl.BlockSpec((None, N_MEM, W_M), per_b),
            pl.BlockSpec((D_MIX, D_MODEL), const),
        ],
        out_specs=pl.BlockSpec((None, tm, D_MODEL), tok),
        out_shape=jax.ShapeDtypeStruct((b, s, D_MODEL), F32),
        compiler_params=pltpu.CompilerParams(
            dimension_semantics=("arbitrary", "arbitrary"), vmem_limit_bytes=VMEM_LIMIT),
        name="mix_out",
    )(x, oa, hb, qm, gm, mem_k, mem_vt, w_out_bf)


def _mixer_layer(x, past_kv, mlstm_state, mem_k, mem_vt, lam_init, p):
    b, s, _ = x.shape
    n = b * s
    tm = min(512, n)
    qaug, k_new, v_new, ga, mb, gl, og, qm, gm, k4, v4 = _in_proj(
        x.reshape(n, D_MODEL), tm, p["g_norm"], p["w_perm"], p["g_mat"],
        p["g_qa"], p["g_ka"], p["g_qm"], p["b_if"])
    k_new = k_new.reshape(b, s, W_A)
    v_new = v_new.reshape(b, s, W_A)
    qaug = qaug.reshape(b, s, 2 * W_A)
    ga = ga.reshape(b, s, W_A)

    if past_kv is None:
        oa = _prompt_attn(qaug, k_new, v_new, ga, p["lam_vecs"], p["g_subln"], lam_init,
                          tile=256)
    else:
        oa = _sample_attn(qaug, past_kv[0], past_kv[1], k_new, v_new, ga,
                          p["lam_vecs"], p["g_subln"], lam_init)

    c0, n0, m0 = mlstm_state
    s_pad = -(-s // CHUNK) * CHUNK
    mb3 = mb.reshape(b, s, 3 * W_B)
    gl3 = gl.reshape(b, s, LANES)
    og3 = og.reshape(b, s, 2 * W_B)
    if s_pad != s:
        pad = ((0, 0), (0, s_pad - s), (0, 0))
        mb3, gl3, og3 = jnp.pad(mb3, pad), jnp.pad(gl3, pad), jnp.pad(og3, pad)
    valid = CHUNK if s_pad == s else s
    hb, c_new, n_row, m_row = _mlstm(
        mb3, gl3, og3, p["g_mh"], c0, n0.reshape(b, 1, W_B),
        jnp.repeat(m0, DH_B, axis=-1).reshape(b, 1, W_B), valid, n_seq=min(b, 32))
    hb = hb[:, :s]
    new_state = (c_new, n_row.reshape(b, H_B, DH_B), m_row.reshape(b, H_B, DH_B)[:, :, 0])

    y = _mix_out(x, oa, hb, qm.reshape(b, s, W_M), gm.reshape(b, s, W_M), mem_k, mem_vt,
                 p["w_out"], tm=min(2048, s))
    return (y, k4.reshape(b, s, H_A, DV_A), v4.reshape(b, s, H_A, DV_A), new_state)


def kernel(x_prompt, x_sample, cache_attn_k, cache_attn_v, state_mlstm_C, state_mlstm_n,
           state_mlstm_m, cache_mem_k, cache_mem_v, mem_prompt, g_norm, w_in, w_out, g_qa,
           g_ka, lam_q1, lam_k1, lam_q2, lam_k2, g_subln, b_i, b_f, g_mh, g_qm, g_km, g_mem,
           w_mk, w_mv):
    depth = w_in.shape[0]
    bp = x_prompt.shape[0]
    bs = x_sample.shape[0]
    past = cache_attn_k.shape[2]
    gi = lax.broadcasted_iota(jnp.int32, (MXU_DIM, MXU_DIM), 0) // DH_B
    gj = lax.broadcasted_iota(jnp.int32, (MXU_DIM, MXU_DIM), 1) // DH_B
    g_mat = jnp.where(gi == gj, 1.0 / DH_B, 0.0).astype(BF16)

    xp, xs = x_prompt, x_sample
    outs = {name: [] for name in ("pk", "pv", "pC", "pn", "pm", "pmk", "pmv",
                                  "sk", "sv", "sC", "sn", "sm")}
    for l in range(depth):
        lam_init = 0.8 - 0.6 * math.exp(-0.3 * l)
        w = w_in[l]
        w_perm = jnp.concatenate(
            [w[:, :4 * W_A + 4 * W_B], w[:, 4 * W_A + 4 * W_B + 2 * H_B:],
             w[:, 4 * W_A + 4 * W_B:4 * W_A + 4 * W_B + 2 * H_B],
             jnp.zeros((D_MODEL, LANES - 2 * H_B), F32)], axis=1).astype(BF16)
        p = {
            "g_norm": g_norm[l].reshape(1, D_MODEL),
            "w_perm": w_perm,
            "g_mat": g_mat,
            "g_qa": jnp.tile(g_qa[l], 2 * H_A).reshape(1, W_A),
            "g_ka": jnp.tile(g_ka[l], 2 * H_A).reshape(1, W_A),
            "g_qm": jnp.tile(g_qm[l], H_M).reshape(1, W_M),
            "b_if": jnp.concatenate(
                [b_i[l], b_f[l], jnp.zeros((LANES - 2 * H_B,), F32)]).reshape(1, LANES),
            "lam_vecs": tuple(v[l].reshape(1, DK_A) for v in (lam_q1, lam_k1, lam_q2, lam_k2)),
            "g_subln": g_subln[l].reshape(1, DV_A),
            "g_mh": jnp.tile(g_mh[l], H_B).reshape(1, W_B),
            "w_out": w_out[l].astype(BF16),
        }
        mk, mkt, mvt = _memory_kv(mem_prompt, g_mem[l], w_mk[l], w_mv[l], g_km[l], g_mat)
        zero_state = (jnp.zeros((bp, H_B, DH_B, DH_B), F32), jnp.zeros((bp, H_B, DH_B), F32),
                      jnp.zeros((bp, H_B), F32))
        xp, k_p, v_p, st_p = _mixer_layer(xp, None, zero_state, mk, mvt, lam_init, p)
        xs, k_s, v_s, st_s = _mixer_layer(
            xs, (cache_attn_k[l], cache_attn_v[l]),
            (state_mlstm_C[l], state_mlstm_n[l], state_mlstm_m[l]),
            cache_mem_k[l].reshape(bs, N_MEM, W_M),
            cache_mem_v[l].transpose(0, 2, 3, 1).reshape(bs, W_M, N_MEM),
            lam_init, p)

        def tokens_first(t):
            return t.reshape(bp, H_M, DH_M, N_MEM).transpose(0, 3, 1, 2)

        outs["pk"].append(k_p); outs["pv"].append(v_p)
        outs["pC"].append(st_p[0]); outs["pn"].append(st_p[1]); outs["pm"].append(st_p[2])
        outs["pmk"].append(tokens_first(mkt))
        outs["pmv"].append(tokens_first(mvt))
        outs["sk"].append(k_s); outs["sv"].append(v_s)
        outs["sC"].append(st_s[0]); outs["sn"].append(st_s[1]); outs["sm"].append(st_s[2])
    stk = {name: jnp.stack(vals) for name, vals in outs.items()}
    return (xp, xs, stk["pk"], stk["pv"], stk["pC"], stk["pn"], stk["pm"], stk["pmk"],
            stk["pmv"], stk["sk"], stk["sv"], stk["sC"], stk["sn"], stk["sm"])
```

```python
import functools
import math

import jax
import jax.numpy as jnp
import numpy as np
from jax import lax
from jax.experimental import pallas as pl
from jax.experimental.pallas import tpu as pltpu

F32 = jnp.float32
BF16 = jnp.bfloat16

D_MODEL = 1024
CHUNK = 64
N_MEM = 256
H_A, DK_A = 4, 64
DV_A = 2 * DK_A
W_A = H_A * DV_A
H_B, DH_B = 4, 64
W_B = H_B * DH_B
H_M, DH_M = 4, 64
W_M = H_M * DH_M
D_MIX = W_A + W_B + W_M
EPS = 1e-6
ALIBI_SLOPES = tuple(2.0 ** (-8.0 * (h + 1) / H_A) for h in range(H_A))
N_IN = 4 * W_A + 5 * W_B + 2 * H_B + 2 * W_M

LANES = 128
MXU_DIM = 256
BF16_SUBLANES = 16
VT_ROWS = DV_A + BF16_SUBLANES
NEG_BIG = -1e30
LOG2E = math.log2(math.e)


def _bf16_pieces(value, n=3):
    pieces, rest = [], np.float32(value)
    for _ in range(n):
        piece = np.float32(rest.astype(jnp.bfloat16))
        pieces.append(float(piece))
        rest = np.float32(rest - piece)
    return tuple(pieces)


LOG2E_BF16_PIECES = _bf16_pieces(LOG2E)

C_QA, C_KA, C_VA, C_GA = 0, W_A, 2 * W_A, 3 * W_A
C_QB = 4 * W_A
C_KB, C_VB, C_OB, C_GB = C_QB + W_B, C_QB + 2 * W_B, C_QB + 3 * W_B, C_QB + 4 * W_B
C_QM = C_GB + W_B
C_GM = C_QM + W_M
C_IF = C_GM + W_M
N_IN_PAD = C_IF + LANES

VMEM_LIMIT = 56 * 1024 * 1024


def _dot(a, b):
    return jnp.dot(a, b, preferred_element_type=F32)


def _dot_nt(a, b):
    return lax.dot_general(a, b, (((1,), (1,)), ((), ())), preferred_element_type=F32)


def _split3(x):
    hi = x.astype(BF16)
    r1 = x - hi.astype(F32)
    mid = r1.astype(BF16)
    lo = (r1 - mid.astype(F32)).astype(BF16)
    return hi, mid, lo


def _group_mean_sq(z, g_mat):
    zz = (z * z).astype(BF16)
    parts = [_dot(zz[:, c:c + MXU_DIM], g_mat) for c in range(0, z.shape[1], MXU_DIM)]
    return parts[0] if len(parts) == 1 else jnp.concatenate(parts, axis=1)


def _group_rms_norm(z, gain_row, g_mat):
    return z * lax.rsqrt(_group_mean_sq(z, g_mat) + EPS) * gain_row


def _log_sigmoid(u):
    return -(jnp.maximum(-u, 0.0) + jnp.log1p(jnp.exp(-jnp.abs(u))))


def _silu(u):
    return u * jax.nn.sigmoid(u)


def _memory_kv_kernel(mem_ref, gmem_ref, wk_ref, wv_ref, gkm_ref, gmat_ref,
                      mk_ref, mkt_ref, mvt_ref, mv_scr):
    g_seq, n, d = mem_ref.shape
    x = mem_ref[...].reshape(g_seq * n, d)
    ms = jnp.mean(x * x, axis=-1, keepdims=True)
    hm = (x * lax.rsqrt(ms + EPS) * gmem_ref[...]).astype(BF16)
    mk = _group_rms_norm(_dot(hm, wk_ref[...]), gkm_ref[...], gmat_ref[...])
    mk_ref[...] = mk.reshape(g_seq, n, W_M)
    mv_scr[...] = _dot(hm, wv_ref[...]).reshape(g_seq, n, W_M)
    for g in range(g_seq):
        mkt_ref[g] = mk_ref[g].T
        mvt_ref[g] = mv_scr[g].T


def _memory_kv(mem, g_mem, w_mk, w_mv, g_km, g_mat, g_seq=4):
    b, n, d = mem.shape
    row = lambda i: (i, 0, 0)
    const2 = lambda i: (0, 0)
    return pl.pallas_call(
        _memory_kv_kernel,
        grid=(b // g_seq,),
        in_specs=[
            pl.BlockSpec((g_seq, n, d), row),
            pl.BlockSpec((1, d), const2),
            pl.BlockSpec((d, W_M), const2),
            pl.BlockSpec((d, W_M), const2),
            pl.BlockSpec((1, W_M), const2),
            pl.BlockSpec((MXU_DIM, MXU_DIM), const2),
        ],
        out_specs=[pl.BlockSpec((g_seq, n, W_M), row), pl.BlockSpec((g_seq, W_M, n), row),
                   pl.BlockSpec((g_seq, W_M, n), row)],
        out_shape=[jax.ShapeDtypeStruct((b, n, W_M), F32),
                   jax.ShapeDtypeStruct((b, W_M, n), F32),
                   jax.ShapeDtypeStruct((b, W_M, n), F32)],
        scratch_shapes=[pltpu.VMEM((g_seq, n, W_M), F32)],
        compiler_params=pltpu.CompilerParams(
            dimension_semantics=("arbitrary",), vmem_limit_bytes=VMEM_LIMIT),
        name="memory_kv",
    )(mem, g_mem.reshape(1, d), w_mk.astype(BF16), w_mv.astype(BF16),
      jnp.tile(g_km, H_M).reshape(1, W_M), g_mat)


def _in_proj_kernel(x_ref, gn_ref, w_ref, gmat_ref, gqa_ref, gka_ref, gqm_ref, bif_ref,
                    qaug_ref, k_ref, v_ref, ga_ref, mb_ref, gl_ref, og_ref, qm_ref, gm_ref,
                    k4_ref, v4_ref):
    x = x_ref[...]
    h = (x * gn_ref[...]).astype(BF16)
    inv_rms = lax.rsqrt(jnp.mean(x * x, axis=-1, keepdims=True) + EPS)
    g_mat = gmat_ref[...]

    def proj(start, width):
        return _dot(h, w_ref[:, start:start + width]) * inv_rms

    qn = _group_rms_norm(proj(C_QA, W_A), gqa_ref[...], g_mat) * (DK_A ** -0.5 * LOG2E)
    lane = lax.broadcasted_iota(jnp.int32, (x.shape[0], LANES), 1)
    aug = jnp.zeros((x.shape[0], LANES), F32)
    for rep in range(2):
        for piece, val in enumerate(LOG2E_BF16_PIECES):
            aug = jnp.where(lane == DK_A + 3 * rep + piece, val, aug)
    for hd in range(H_A):
        slab = qn[:, hd * DV_A:(hd + 1) * DV_A]
        q1 = jnp.where(lane < DK_A, slab, aug)
        q2 = jnp.where(lane < DK_A, pltpu.roll(slab, DK_A, 1), aug)
        qaug_ref[:, (2 * hd) * LANES:(2 * hd + 1) * LANES] = q1.astype(BF16)
        qaug_ref[:, (2 * hd + 1) * LANES:(2 * hd + 2) * LANES] = q2.astype(BF16)

    kn = _group_rms_norm(proj(C_KA, W_A), gka_ref[...], g_mat)
    k_ref[...] = kn.astype(BF16)

    def cache_copy(dst_ref, src, hd, later):
        zero = pltpu.bitcast(
            lax.shift_right_logical(pltpu.bitcast(later[:, 0:DV_A], jnp.uint32), jnp.uint32(32)),
            F32)
        dst_ref[:, hd, :] = src[:, hd * DV_A:(hd + 1) * DV_A] + zero

    vn = proj(C_VA, W_A)
    v_ref[...] = vn.astype(BF16)
    cache_copy(k4_ref, kn, 0, vn)
    z = proj(C_GA, W_A)
    ga_ref[...] = _silu(z)
    cache_copy(k4_ref, kn, 1, z)
    z = proj(C_QB, W_B)
    mb_ref[:, 0:W_B] = z
    cache_copy(k4_ref, kn, 2, z)
    z = proj(C_KB, W_B)
    mb_ref[:, W_B:2 * W_B] = z * (DH_B ** -0.5)
    cache_copy(k4_ref, kn, 3, z)
    z = proj(C_VB, W_B)
    mb_ref[:, 2 * W_B:3 * W_B] = z
    cache_copy(v4_ref, vn, 0, z)
    z = proj(C_OB, W_B)
    og_ref[:, 0:W_B] = jax.nn.sigmoid(z)
    cache_copy(v4_ref, vn, 1, z)
    z = proj(C_GB, W_B)
    og_ref[:, W_B:2 * W_B] = _silu(z)
    cache_copy(v4_ref, vn, 2, z)
    z = proj(C_QM, W_M)
    qm_ref[...] = (_group_rms_norm(z, gqm_ref[...], g_mat)
                   * (DH_M ** -0.5 * LOG2E)).astype(BF16)
    cache_copy(v4_ref, vn, 3, z)
    gm_ref[...] = _silu(proj(C_GM, W_M))

    u = proj(C_IF, LANES) + bif_ref[...]
    gl_ref[...] = jnp.where(lane < H_B, u, _log_sigmoid(u))


def _in_proj(x2d, tm, g_norm, w_perm, g_mat, g_qa, g_ka, g_qm, b_if):
    n = x2d.shape[0]
    row = lambda i: (i, 0)
    const = lambda i: (0, 0)
    widths = (2 * W_A, W_A, W_A, W_A, 3 * W_B, LANES, 2 * W_B, W_M, W_M)
    dtypes = (BF16, BF16, BF16) + (F32,) * 4 + (BF16, F32)
    return pl.pallas_call(
        _in_proj_kernel,
        grid=(n // tm,),
        in_specs=[
            pl.BlockSpec((tm, D_MODEL), row),
            pl.BlockSpec((1, D_MODEL), const),
            pl.BlockSpec((D_MODEL, N_IN_PAD), const),
            pl.BlockSpec((MXU_DIM, MXU_DIM), const),
            pl.BlockSpec((1, W_A), const),
            pl.BlockSpec((1, W_A), const),
            pl.BlockSpec((1, W_M), const),
            pl.BlockSpec((1, LANES), const),
        ],
        out_specs=[pl.BlockSpec((tm, w), row) for w in widths]
        + [pl.BlockSpec((tm, H_A, DV_A), lambda i: (i, 0, 0))] * 2,
        out_shape=[jax.ShapeDtypeStruct((n, w), dt) for w, dt in zip(widths, dtypes)]
        + [jax.ShapeDtypeStruct((n, H_A, DV_A), F32)] * 2,
        compiler_params=pltpu.CompilerParams(
            dimension_semantics=("arbitrary",), vmem_limit_bytes=VMEM_LIMIT),
        name="in_proj",
    )(x2d, g_norm, w_perm, g_mat, g_qa, g_ka, g_qm, b_if)


def _lambda_value(lq1_ref, lk1_ref, lq2_ref, lk2_ref, lam_init):
    s1 = jnp.sum(lq1_ref[...] * lk1_ref[...], axis=-1, keepdims=True)
    s2 = jnp.sum(lq2_ref[...] * lk2_ref[...], axis=-1, keepdims=True)
    return jnp.exp(s1) - jnp.exp(s2) + lam_init


def _subln(o, gsub_row, lam_init):
    ms = jnp.mean(o * o, axis=-1, keepdims=True)
    return o * lax.rsqrt(ms + EPS) * gsub_row * (1.0 - lam_init)


def _prompt_attn_kernel(lq1_ref, lk1_ref, lq2_ref, lk2_ref, gsub_ref, q_ref, k_ref, v_ref,
                        ga_ref, o_ref, kaug_ref, vt_ref, m_ref, acc_ref, s_ref, qt_ref,
                        *, tile, lam_init):
    seq = k_ref.shape[0]
    n_tiles = seq // tile
    n_chain = 2 * H_A

    lane = lax.broadcasted_iota(jnp.int32, (seq, LANES), 1)
    pos = lax.broadcasted_iota(jnp.int32, (seq, LANES), 0)
    in_tile = (pos % tile).astype(F32)
    tile_base = (pos - pos % tile).astype(F32)
    for hd in range(H_A):
        slab = k_ref[:, hd * DV_A:(hd + 1) * DV_A].astype(F32)
        aug = jnp.where((lane >= DK_A) & (lane < DK_A + 3), ALIBI_SLOPES[hd] * in_tile,
                        jnp.where((lane >= DK_A + 3) & (lane < DK_A + 6),
                                  ALIBI_SLOPES[hd] * tile_base, 0.0))
        kaug_ref[2 * hd] = jnp.where(lane < DK_A, slab, aug).astype(BF16)
        kaug_ref[2 * hd + 1] = jnp.where(
            lane < DK_A, pltpu.roll(slab, DK_A, 1), aug).astype(BF16)
    for t in range(n_tiles):
        v_t = v_ref[t * tile:(t + 1) * tile, :].astype(F32).T.astype(BF16)
        for hd in range(H_A):
            vt_ref[t, hd, 0:DV_A, :] = v_t[hd * DV_A:(hd + 1) * DV_A, :]
            vt_ref[t, hd, DV_A:VT_ROWS, :] = jnp.ones((VT_ROWS - DV_A, tile), BF16)

    lam = _lambda_value(lq1_ref, lk1_ref, lq2_ref, lk2_ref, lam_init)

    key = lax.broadcasted_iota(jnp.int32, (tile, tile), 0)
    qry = lax.broadcasted_iota(jnp.int32, (tile, tile), 1)
    ahead = (key - qry).astype(F32)
    diag_bias = [jnp.where(key // CHUNK > qry // CHUNK, NEG_BIG,
                           jnp.where(key > qry, (-2.0 * LOG2E * ALIBI_SLOPES[hd]) * ahead, 0.0))
                 for hd in range(H_A)]

    def rows(i):
        if isinstance(i, int):
            return pl.ds(i * tile, tile)
        return pl.ds(pl.multiple_of(i * tile, tile), tile)

    def scores(par, c, h, j):
        s_ref[par, c, h] = _dot(kaug_ref[c, rows(j), :], qt_ref[par, c, h])

    def softmax_values(par, c, h, j, diagonal):
        s = s_ref[par, c, h]
        if diagonal:
            s = s + diag_bias[c // 2]
        mx = m_ref[par, c, h]
        mx_new = jnp.maximum(mx, jnp.max(s, axis=0, keepdims=True))
        p = jnp.exp2(s - mx_new)
        alpha = jnp.exp2(mx - mx_new)
        m_ref[par, c, h] = mx_new
        acc_ref[par, c, h] = (alpha * acc_ref[par, c, h]
                              + _dot(vt_ref[j, c // 2], p.astype(BF16)))

    def begin_chain(par, c, duo):
        for h in range(2):
            q = q_ref[rows(2 * duo + h), c * LANES:(c + 1) * LANES]
            qt_ref[par, c, h] = q.astype(F32).T.astype(BF16)
            m_ref[par, c, h] = jnp.full((1, tile), NEG_BIG, F32)
            acc_ref[par, c, h] = jnp.zeros((VT_ROWS, tile), F32)
            scores(par, c, h, 0)

    def finish_head(par, hd, duo):
        for h in range(2):
            a1 = acc_ref[par, 2 * hd, h]
            a2 = acc_ref[par, 2 * hd + 1, h]
            o1 = a1[0:DV_A] * (1.0 / a1[DV_A:DV_A + 1])
            o2 = a2[0:DV_A] * (1.0 / a2[DV_A:DV_A + 1])
            o = (o1 - lam * o2).T
            gate = ga_ref[rows(2 * duo + h), hd * DV_A:(hd + 1) * DV_A]
            o_ref[rows(2 * duo + h), hd * DV_A:(hd + 1) * DV_A] = (
                _subln(o, gsub_ref[...], lam_init) * gate).astype(BF16)

    for c in range(n_chain):
        begin_chain(0, c, 0)

    def duo_pair(pair, carry):
        for par in range(2):
            duo = 2 * pair + par
            first_diag = 2 * duo

            def past_tile(j, inner, par=par):
                for c in range(n_chain):
                    for h in range(2):
                        softmax_values(par, c, h, j, False)
                        scores(par, c, h, j + 1)
                return inner

            lax.fori_loop(0, first_diag, past_tile, 0)
            for c in range(n_chain):
                softmax_values(par, c, 0, first_diag, True)
                softmax_values(par, c, 1, first_diag, False)
                scores(par, c, 1, first_diag + 1)
            nxt = lax.rem(duo + 1, n_tiles // 2)
            for c in range(n_chain):
                softmax_values(par, c, 1, first_diag + 1, True)
                begin_chain(1 - par, c, nxt)
                if c % 2 == 1:
                    finish_head(par, c // 2, duo)
        return carry

    lax.fori_loop(0, n_tiles // 4, duo_pair, 0)


def _prompt_attn(qaug, k, v, ga, lam_vecs, g_subln, lam_init, tile):
    b, s, _ = k.shape
    assert (s // tile) % 4 == 0
    const = lambda bi: (0, 0)
    per_b = lambda bi: (bi, 0, 0)
    kern = functools.partial(_prompt_attn_kernel, tile=tile, lam_init=lam_init)
    return pl.pallas_call(
        kern,
        grid=(b,),
        in_specs=[pl.BlockSpec((1, DK_A), const)] * 4 + [
            pl.BlockSpec((1, DV_A), const),
            pl.BlockSpec((None, s, 2 * W_A), per_b),
            pl.BlockSpec((None, s, W_A), per_b),
            pl.BlockSpec((None, s, W_A), per_b),
            pl.BlockSpec((None, s, W_A), per_b),
        ],
        out_specs=pl.BlockSpec((None, s, W_A), per_b),
        out_shape=jax.ShapeDtypeStruct((b, s, W_A), BF16),
        scratch_shapes=[
            pltpu.VMEM((2 * H_A, s, LANES), BF16),
            pltpu.VMEM((s // tile, H_A, VT_ROWS, tile), BF16),
            pltpu.VMEM((2, 2 * H_A, 2, 1, tile), F32),
            pltpu.VMEM((2, 2 * H_A, 2, VT_ROWS, tile), F32),
            pltpu.VMEM((2, 2 * H_A, 2, tile, tile), F32),
            pltpu.VMEM((2, 2 * H_A, 2, LANES, tile), BF16),
        ],
        compiler_params=pltpu.CompilerParams(
            dimension_semantics=("arbitrary",), vmem_limit_bytes=VMEM_LIMIT),
        name="diff_attn_prompt",
    )(*lam_vecs, g_subln, qaug, k, v, ga)


def _sample_attn_kernel(lq1_ref, lk1_ref, lq2_ref, lk2_ref, gsub_ref, q_ref, kc_ref, vc_ref,
                        kn_ref, vn_ref, ga_ref, o_ref, *, lam_init):
    sq = q_ref.shape[0]
    past = kc_ref.shape[0] // H_A
    lam = _lambda_value(lq1_ref, lk1_ref, lq2_ref, lk2_ref, lam_init)
    lane = lax.broadcasted_iota(jnp.int32, (sq, LANES), 1)
    qpos_c = past + lax.broadcasted_iota(jnp.int32, (sq, past), 0)
    dist_c = jnp.abs(qpos_c - lax.broadcasted_iota(jnp.int32, (sq, past), 1)).astype(F32)
    dist_n = jnp.abs(lax.broadcasted_iota(jnp.int32, (sq, sq), 0)
                     - lax.broadcasted_iota(jnp.int32, (sq, sq), 1)).astype(F32)
    for hd in range(H_A):
        slope = ALIBI_SLOPES[hd]
        head_lanes = slice(hd * DV_A, (hd + 1) * DV_A)
        kc = kc_ref[pl.ds(hd, past, stride=H_A), :].astype(BF16)
        kn = kn_ref[:, head_lanes]
        vc = vc_ref[pl.ds(hd, past, stride=H_A), :].astype(BF16)
        vn = vn_ref[:, head_lanes]
        outs = []
        for m in range(2):
            q = q_ref[:, (2 * hd + m) * LANES:(2 * hd + m + 1) * LANES].astype(F32)
            if m == 0:
                q = jnp.where(lane < DK_A, q, 0.0)
            else:
                q = jnp.where(lane >= DK_A, pltpu.roll(q, DK_A, 1), 0.0)
            q = q.astype(BF16)
            s_c = _dot_nt(q, kc) - (slope * LOG2E) * dist_c
            s_n = _dot_nt(q, kn) - (slope * LOG2E) * dist_n
            mx = jnp.maximum(jnp.max(s_c, axis=-1, keepdims=True),
                             jnp.max(s_n, axis=-1, keepdims=True))
            p_c = jnp.exp2(s_c - mx)
            p_n = jnp.exp2(s_n - mx)
            l = jnp.sum(p_c, axis=-1, keepdims=True) + jnp.sum(p_n, axis=-1, keepdims=True)
            acc = _dot(p_c.astype(BF16), vc) + _dot(p_n.astype(BF16), vn)
            outs.append(acc * (1.0 / l))
        o = outs[0] - lam * outs[1]
        o_ref[:, head_lanes] = (
            _subln(o, gsub_ref[...], lam_init) * ga_ref[:, head_lanes]).astype(BF16)


def _sample_attn(qaug, k_cache, v_cache, k_new, v_new, ga, lam_vecs, g_subln, lam_init):
    b, past = k_cache.shape[:2]
    sq = k_new.shape[1]
    assert (past + sq - 1) // CHUNK <= past // CHUNK
    k_cache = k_cache.reshape(b, past * H_A, DV_A)
    v_cache = v_cache.reshape(b, past * H_A, DV_A)
    const = lambda bi: (0, 0)
    per_b = lambda bi: (bi, 0, 0)
    kern = functools.partial(_sample_attn_kernel, lam_init=lam_init)
    return pl.pallas_call(
        kern,
        grid=(b,),
        in_specs=[pl.BlockSpec((1, DK_A), const)] * 4 + [
            pl.BlockSpec((1, DV_A), const),
            pl.BlockSpec((None, sq, 2 * W_A), per_b),
            pl.BlockSpec((None, past * H_A, DV_A), per_b),
            pl.BlockSpec((None, past * H_A, DV_A), per_b),
            pl.BlockSpec((None, sq, W_A), per_b),
            pl.BlockSpec((None, sq, W_A), per_b),
            pl.BlockSpec((None, sq, W_A), per_b),
        ],
        out_specs=pl.BlockSpec((None, sq, W_A), per_b),
        out_shape=jax.ShapeDtypeStruct((b, sq, W_A), BF16),
        compiler_params=pltpu.CompilerParams(
            dimension_semantics=("arbitrary",), vmem_limit_bytes=VMEM_LIMIT),
        name="diff_attn_sample",
    )(*lam_vecs, g_subln, qaug, k_cache, v_cache, k_new, v_new, ga)


def _mlstm_kernel(mb_ref, gl_ref, og_ref, gmh_ref, c0_ref, n0_ref, m0_ref,
                  h_ref, c_out_ref, n_ref, m_ref, c_ref, *, valid):
    c_idx = pl.program_id(1)
    L = CHUNK
    W = W_B
    G = mb_ref.shape[0]
    R = G * L

    def head_block(hd):
        return slice(hd * DH_B, (hd + 1) * DH_B)

    @pl.when(c_idx == 0)
    def _load_state():
        c_ref[...] = jnp.zeros(c_ref.shape, F32)
        for g in range(G):
            for hd in range(H_B):
                c_ref[g, head_block(hd), head_block(hd)] = c0_ref[g, hd]
        n_ref[...] = n0_ref[...]
        m_ref[...] = m0_ref[...]

    lane_head = lax.broadcasted_iota(jnp.int32, (R, W), 1) // DH_B
    row_t = lax.broadcasted_iota(jnp.int32, (R, W), 0) % L
    lane_s = lax.broadcasted_iota(jnp.int32, (R, W), 1) % DH_B
    wi = lax.broadcasted_iota(jnp.int32, (W, W), 0)
    wj = lax.broadcasted_iota(jnp.int32, (W, W), 1)
    blk = wi // DH_B == wj // DH_B
    ones_bd = blk.astype(BF16)
    ei = lax.broadcasted_iota(jnp.int32, (LANES, 2 * W), 0)
    ej = lax.broadcasted_iota(jnp.int32, (LANES, 2 * W), 1)
    expand = (ei == ej // DH_B).astype(BF16)
    tri = (lax.broadcasted_iota(jnp.int32, (L, L), 1)
           <= lax.broadcasted_iota(jnp.int32, (L, L), 0)).astype(BF16)
    nar_lane = lax.broadcasted_iota(jnp.int32, (R, LANES), 1)
    nar_t = lax.broadcasted_iota(jnp.int32, (R, LANES), 0) % L

    def per_seq(x):
        return x.reshape(G, L, x.shape[-1])

    def stacked(x):
        return x.reshape(R, x.shape[-1])

    def seq_rows(x, g):
        return x[g * L:(g + 1) * L]

    def exact_matmul(xs, mat):
        n = xs[0].shape[0]
        pieces = [part for x in xs for part in _split3(x)]
        out = _dot(jnp.concatenate(pieces, axis=0), mat)
        return [out[(3 * i) * n:(3 * i + 1) * n] + out[(3 * i + 1) * n:(3 * i + 2) * n]
                + out[(3 * i + 2) * n:(3 * i + 3) * n] for i in range(len(xs))]

    def head_max(x):
        out = jnp.zeros((R, W), F32)
        for hd in range(H_B):
            mx = jnp.max(jnp.where(lane_head == hd, x, NEG_BIG), axis=-1, keepdims=True)
            out = jnp.where(lane_head == hd, mx, out)
        return out

    q = stacked(mb_ref[:, :, 0:W])
    k = stacked(mb_ref[:, :, W:2 * W])
    v = stacked(mb_ref[:, :, 2 * W:3 * W])
    gl = jnp.where(nar_lane < 2 * H_B, stacked(gl_ref[...]), 0.0)
    gl = jnp.where(nar_t < valid, gl, jnp.where(nar_lane < H_B, NEG_BIG, 0.0))
    cums = []
    for g in range(G):
        out = _dot(tri, jnp.concatenate(_split3(seq_rows(gl, g)), axis=1))
        cums.append(out[:, 0:LANES] + out[:, LANES:2 * LANES] + out[:, 2 * LANES:3 * LANES])
    narrow = jnp.where(nar_lane < H_B, gl, jnp.concatenate(cums, axis=0))
    (wide,) = exact_matmul([narrow], expand)
    ig_all = wide[:, 0:W]
    b_all = wide[:, W:2 * W]

    r_all = b_all - ig_all
    r_row = jnp.sum(per_seq(jnp.where(lane_s == row_t, r_all, 0.0)), axis=1, keepdims=True)
    m0_row = m_ref[...]
    n0_row = n_ref[...]
    b_seq = per_seq(b_all)
    d_mat = jnp.where(lane_s <= row_t, stacked(b_seq - r_row), NEG_BIG)
    g_all = stacked(b_seq + m0_row)
    m_all = jnp.maximum(g_all, head_max(d_mat))
    d_w = jnp.exp(d_mat - m_all)
    g_w = jnp.exp(g_all - m_all)

    q_bf = q.astype(BF16)
    zero_bf = jnp.zeros((W, W), BF16)
    scores, inter, v_bds = [], [], []
    for g in range(G):
        k_t = seq_rows(k, g).T.astype(BF16)
        k_bd = jnp.where(blk, jnp.concatenate([k_t] * H_B, axis=1), zero_bf)
        v_bds.append(jnp.where(
            blk, jnp.concatenate([seq_rows(v, g).astype(BF16)] * H_B, axis=0), zero_bf))
        c0_bd = jnp.where(blk, c_ref[g].astype(BF16), zero_bf)
        scores.append(_dot(seq_rows(q_bf, g), k_bd))
        inter.append(_dot(seq_rows(q_bf, g), c0_bd))
    a_mat = jnp.concatenate(scores, axis=0) * d_w
    a_bf = a_mat.astype(BF16)
    intra = [_dot(seq_rows(a_bf, g), v_bds[g]) for g in range(G)]
    num = g_w * jnp.concatenate(inter, axis=0) + jnp.concatenate(intra, axis=0)

    qn_bf = stacked(per_seq(q) * n0_row).astype(BF16)
    sums = _dot(jnp.concatenate([qn_bf, a_bf], axis=0), ones_bd)
    den = g_w * sums[0:R] + sums[R:2 * R]
    h = num / jnp.maximum(jnp.abs(den), jnp.exp(-m_all))
    h_ms = _dot((h * h).astype(BF16), ones_bd) * (1.0 / DH_B)
    h = h * lax.rsqrt(h_ms + EPS) * gmh_ref[...]
    og = stacked(og_ref[...])
    h_ref[...] = per_seq((h * og[:, 0:W] * og[:, W:2 * W]).astype(BF16))

    m_last = per_seq(m_all)[:, L - 1:L, :]
    b_last = b_seq[:, L - 1:L, :]
    w_s = jnp.exp(stacked(b_last - b_seq + per_seq(ig_all) - m_last))
    decay = jnp.exp(b_last + m0_row - m_last)
    kw = k * w_s
    v_bf = v.astype(BF16)
    for g in range(G):
        c_ref[g] = decay[g] * c_ref[g] + _dot(seq_rows(kw, g).T.astype(BF16), seq_rows(v_bf, g))
    n_ref[...] = decay * n0_row + jnp.sum(per_seq(kw), axis=1, keepdims=True)
    m_ref[...] = m_last

    @pl.when(c_idx == pl.num_programs(1) - 1)
    def _store_state():
        for g in range(G):
            for hd in range(H_B):
                c_out_ref[g, hd] = c_ref[g, head_block(hd), head_block(hd)]


def _mlstm(mb, gl, og, g_mh_row, c0, n0_row, m0_row, valid, n_seq):
    b, s, _ = mb.shape
    nc = s // CHUNK
    tok = lambda bi, ci: (bi, ci, 0)
    st = lambda bi, ci: (bi, 0, 0)
    st4 = lambda bi, ci: (bi, 0, 0, 0)
    kern = functools.partial(_mlstm_kernel, valid=valid)
    return pl.pallas_call(
        kern,
        grid=(b // n_seq, nc),
        in_specs=[
            pl.BlockSpec((n_seq, CHUNK, 3 * W_B), tok),
            pl.BlockSpec((n_seq, CHUNK, LANES), tok),
            pl.BlockSpec((n_seq, CHUNK, 2 * W_B), tok),
            pl.BlockSpec((1, W_B), lambda bi, ci: (0, 0)),
            pl.BlockSpec((n_seq, H_B, DH_B, DH_B), st4),
            pl.BlockSpec((n_seq, 1, W_B), st),
            pl.BlockSpec((n_seq, 1, W_B), st),
        ],
        out_specs=[
            pl.BlockSpec((n_seq, CHUNK, W_B), tok),
            pl.BlockSpec((n_seq, H_B, DH_B, DH_B), st4),
            pl.BlockSpec((n_seq, 1, W_B), st),
            pl.BlockSpec((n_seq, 1, W_B), st),
        ],
        out_shape=[
            jax.ShapeDtypeStruct((b, s, W_B), BF16),
            jax.ShapeDtypeStruct((b, H_B, DH_B, DH_B), F32),
            jax.ShapeDtypeStruct((b, 1, W_B), F32),
            jax.ShapeDtypeStruct((b, 1, W_B), F32),
        ],
        scratch_shapes=[pltpu.VMEM((n_seq, W_B, W_B), F32)],
        compiler_params=pltpu.CompilerParams(
            dimension_semantics=("arbitrary", "arbitrary"), vmem_limit_bytes=VMEM_LIMIT),
        name="mlstm",
    )(mb, gl, og, g_mh_row, c0, n0_row, m0_row)


def _mix_out_kernel(x_ref, oa_ref, hb_ref, qm_ref, gm_ref, mk_ref, mvt_ref, wo_ref, y_ref,
                    *, sub_rows):
    tm = x_ref.shape[0]
    sub = min(tm, sub_rows)
    mk = mk_ref[...]
    mvt = mvt_ref[...]
    k_feat_head = lax.broadcasted_iota(jnp.int32, mk.shape, 1) // DH_M
    v_feat_head = lax.broadcasted_iota(jnp.int32, mvt.shape, 0) // DH_M
    k_of_head = [jnp.where(k_feat_head == hd, mk, 0.0).astype(BF16) for hd in range(H_M)]
    vt_of_head = [jnp.where(v_feat_head == hd, mvt, 0.0).astype(BF16) for hd in range(H_M)]

    def probs(s):
        p = jnp.exp2(s - jnp.max(s, axis=0, keepdims=True))
        return (p * (1.0 / jnp.sum(p, axis=0, keepdims=True))).astype(BF16)

    def sub_tile_stages(r0):
        rows = slice(r0, r0 + sub)
        qm = qm_ref[rows, :]
        if sub < LANES:
            qm = jnp.concatenate([qm, jnp.zeros((LANES - sub, W_M), BF16)], axis=0)
        s0 = _dot_nt(k_of_head[0], qm)
        y = x_ref[rows, :] + _dot(oa_ref[rows, :], wo_ref[0:W_A, :])
        yield
        s1 = _dot_nt(k_of_head[1], qm)
        p0 = probs(s0)
        y = y + _dot(hb_ref[rows, :], wo_ref[W_A:W_A + W_B, :])
        yield
        s2 = _dot_nt(k_of_head[2], qm)
        p1 = probs(s1)
        omt = _dot(vt_of_head[0], p0)
        yield
        s3 = _dot_nt(k_of_head[3], qm)
        p2 = probs(s2)
        omt = omt + _dot(vt_of_head[1], p1)
        yield
        p3 = probs(s3)
        omt = omt + _dot(vt_of_head[2], p2)
        omt = omt + _dot(vt_of_head[3], p3)
        yield
        om = omt.T[0:sub] * gm_ref[rows, :]
        y_ref[rows, :] = y + _dot(om.astype(BF16), wo_ref[W_A + W_B:D_MIX, :])
        yield

    n_stage = 6
    gens = [sub_tile_stages(r0) for r0 in range(0, tm, sub)]
    for step in range(n_stage + len(gens) - 1):
        for idx, gen in enumerate(gens):
            if 0 <= step - idx < n_stage:
                next(gen)


def _mix_out(x, oa, hb, qm, gm, mem_k, mem_vt, w_out_bf, tm):
    b, s, _ = x.shape
    tok = lambda bi, ti: (bi, ti, 0)
    per_b = lambda bi, ti: (bi, 0, 0)
    const = lambda bi, ti: (0, 0)
    return pl.pallas_call(
        functools.partial(_mix_out_kernel, sub_rows=MXU_DIM),
        grid=(b, s // tm),
        in_specs=[
            pl.BlockSpec((None, tm, D_MODEL), tok),
            pl.BlockSpec((None, tm, W_A), tok),
            pl.BlockSpec((None, tm, W_B), tok),
            pl.BlockSpec((None, tm, W_M), tok),
            pl.BlockSpec((None, tm, W_M), tok),
            pl.BlockSpec((None, N_MEM, W_M), per_b),
            pl.BlockSpec((None, N_MEM, W_M), per_b),
            pl.BlockSpec((D_MIX, D_MODEL), const),
        ],
        out_specs=pl.BlockSpec((None, tm, D_MODEL), tok),
        out_shape=jax.ShapeDtypeStruct((b, s, D_MODEL), F32),
        compiler_params=pltpu.CompilerParams(
            dimension_semantics=("arbitrary", "arbitrary"), vmem_limit_bytes=VMEM_LIMIT),
        name="mix_out",
    )(x, oa, hb, qm, gm, mem_k, mem_vt, w_out_bf)


def _mixer_layer(x, past_kv, mlstm_state, mem_k, mem_vt, lam_init, p):
    b, s, _ = x.shape
    n = b * s
    tm = min(512, n)
    qaug, k_new, v_new, ga, mb, gl, og, qm, gm, k4, v4 = _in_proj(
        x.reshape(n, D_MODEL), tm, p["g_norm"], p["w_perm"], p["g_mat"],
        p["g_qa"], p["g_ka"], p["g_qm"], p["b_if"])
    k_new = k_new.reshape(b, s, W_A)
    v_new = v_new.reshape(b, s, W_A)
    qaug = qaug.reshape(b, s, 2 * W_A)
    ga = ga.reshape(b, s, W_A)

    if past_kv is None:
        oa = _prompt_attn(qaug, k_new, v_new, ga, p["lam_vecs"], p["g_subln"], lam_init,
                          tile=256)
    else:
        oa = _sample_attn(qaug, past_kv[0], past_kv[1], k_new, v_new, ga,
                          p["lam_vecs"], p["g_subln"], lam_init)

    c0, n0, m0 = mlstm_state
    s_pad = -(-s // CHUNK) * CHUNK
    mb3 = mb.reshape(b, s, 3 * W_B)
    gl3 = gl.reshape(b, s, LANES)
    og3 = og.reshape(b, s, 2 * W_B)
    if s_pad != s:
        pad = ((0, 0), (0, s_pad - s), (0, 0))
        mb3, gl3, og3 = jnp.pad(mb3, pad), jnp.pad(gl3, pad), jnp.pad(og3, pad)
    valid = CHUNK if s_pad == s else s
    hb, c_new, n_row, m_row = _mlstm(
        mb3, gl3, og3, p["g_mh"], c0, n0.reshape(b, 1, W_B),
        jnp.repeat(m0, DH_B, axis=-1).reshape(b, 1, W_B), valid, n_seq=min(b, 32))
    hb = hb[:, :s]
    new_state = (c_new, n_row.reshape(b, H_B, DH_B), m_row.reshape(b, H_B, DH_B)[:, :, 0])

    y = _mix_out(x, oa, hb, qm.reshape(b, s, W_M), gm.reshape(b, s, W_M), mem_k, mem_vt,
                 p["w_out"], tm=min(2048, s))
    return (y, k4.reshape(b, s, H_A, DV_A), v4.reshape(b, s, H_A, DV_A), new_state)


def kernel(x_prompt, x_sample, cache_attn_k, cache_attn_v, state_mlstm_C, state_mlstm_n,
           state_mlstm_m, cache_mem_k, cache_mem_v, mem_prompt, g_norm, w_in, w_out, g_qa,
           g_ka, lam_q1, lam_k1, lam_q2, lam_k2, g_subln, b_i, b_f, g_mh, g_qm, g_km, g_mem,
           w_mk, w_mv):
    depth = w_in.shape[0]
    bp = x_prompt.shape[0]
    bs = x_sample.shape[0]
    past = cache_attn_k.shape[2]
    gi = lax.broadcasted_iota(jnp.int32, (MXU_DIM, MXU_DIM), 0) // DH_B
    gj = lax.broadcasted_iota(jnp.int32, (MXU_DIM, MXU_DIM), 1) // DH_B
    g_mat = jnp.where(gi == gj, 1.0 / DH_B, 0.0).astype(BF16)

    xp, xs = x_prompt, x_sample
    outs = {name: [] for name in ("pk", "pv", "pC", "pn", "pm", "pmk", "pmv",
                                  "sk", "sv", "sC", "sn", "sm")}
    for l in range(depth):
        lam_init = 0.8 - 0.6 * math.exp(-0.3 * l)
        w = w_in[l]
        w_perm = jnp.concatenate(
            [w[:, :4 * W_A + 4 * W_B], w[:, 4 * W_A + 4 * W_B + 2 * H_B:],
             w[:, 4 * W_A + 4 * W_B:4 * W_A + 4 * W_B + 2 * H_B],
             jnp.zeros((D_MODEL, LANES - 2 * H_B), F32)], axis=1).astype(BF16)
        p = {
            "g_norm": g_norm[l].reshape(1, D_MODEL),
            "w_perm": w_perm,
            "g_mat": g_mat,
            "g_qa": jnp.tile(g_qa[l], 2 * H_A).reshape(1, W_A),
            "g_ka": jnp.tile(g_ka[l], 2 * H_A).reshape(1, W_A),
            "g_qm": jnp.tile(g_qm[l], H_M).reshape(1, W_M),
            "b_if": jnp.concatenate(
                [b_i[l], b_f[l], jnp.zeros((LANES - 2 * H_B,), F32)]).reshape(1, LANES),
            "lam_vecs": tuple(v[l].reshape(1, DK_A) for v in (lam_q1, lam_k1, lam_q2, lam_k2)),
            "g_subln": g_subln[l].reshape(1, DV_A),
            "g_mh": jnp.tile(g_mh[l], H_B).reshape(1, W_B),
            "w_out": w_out[l].astype(BF16),
        }
        mk, mkt, mvt = _memory_kv(mem_prompt, g_mem[l], w_mk[l], w_mv[l], g_km[l], g_mat)
        zero_state = (jnp.zeros((bp, H_B, DH_B, DH_B), F32), jnp.zeros((bp, H_B, DH_B), F32),
                      jnp.zeros((bp, H_B), F32))
        xp, k_p, v_p, st_p = _mixer_layer(xp, None, zero_state, mk, mvt, lam_init, p)
        xs, k_s, v_s, st_s = _mixer_layer(
            xs, (cache_attn_k[l], cache_attn_v[l]),
            (state_mlstm_C[l], state_mlstm_n[l], state_mlstm_m[l]),
            cache_mem_k[l].reshape(bs, N_MEM, W_M),
            cache_mem_v[l].transpose(0, 2, 3, 1).reshape(bs, W_M, N_MEM),
            lam_init, p)

        def tokens_first(t):
            return t.reshape(bp, H_M, DH_M, N_MEM).transpose(0, 3, 1, 2)

        outs["pk"].append(k_p); outs["pv"].append(v_p)
        outs["pC"].append(st_p[0]); outs["pn"].append(st_p[1]); outs["pm"].append(st_p[2])
        outs["pmk"].append(tokens_first(mkt))
        outs["pmv"].append(tokens_first(mvt))
        outs["sk"].append(k_s); outs["sv"].append(v_s)
        outs["sC"].append(st_s[0]); outs["sn"].append(st_s[1]); outs["sm"].append(st_s[2])
    stk = {name: jnp.stack(vals) for name, vals in outs.items()}
    return (xp, xs, stk["pk"], stk["pv"], stk["pC"], stk["pn"], stk["pm"], stk["pmk"],
            stk["pmv"], stk["sk"], stk["sv"], stk["sC"], stk["sn"], stk["sm"])
```
